```python
import math
import jax, jax.numpy as jnp
from jax import lax
import numpy as np

D_MODEL = 1024
BATCH = 8
SEQ = 2048
DEPTH = 4
DEC_BATCH = 128
DEC_SEQ = 8
PAST_LEN = 2048
PAGE_SIZE = 128

N_MIXERS = 3
CONV_W = 3
FOX_HEADS = 16
FOX_HEAD_DIM = D_MODEL // FOX_HEADS
FOX_BLOCK_Q = 128
ML_HEADS = 8
ML_DV = D_MODEL // ML_HEADS
ML_DQK = ML_DV // 2
ML_CHUNK = 64
GATE_CAP = 15.0
FOX_F_BIAS = 3.0
ML_F_BIAS = 3.0
D_FF = -(-(-(-8 * D_MODEL // 3)) // 256) * 256
EPS = 1e-6
NEG_INF = -1e30
N_CONV = len(range(0, DEPTH, N_MIXERS))
N_FOX = len(range(1, DEPTH, N_MIXERS))
N_ML = len(range(2, DEPTH, N_MIXERS))

kernel_name = 'hybrid_conv_fox_mlstm_decode_step'


def rmsnorm(x, g):
    xf = x.astype(jnp.float32)
    y = xf * lax.rsqrt(jnp.mean(xf * xf, axis=-1, keepdims=True) + EPS)
    return (y * g.astype(jnp.float32)).astype(x.dtype)


def swiglu_ffn(h, w_gu, w_down):
    g, u = jnp.split(h @ w_gu, 2, axis=-1)
    return (jax.nn.silu(g) * u) @ w_down


def short_conv_mixer(h, prev, w_in, w_conv, w_out):
    b_gate, c_gate, xv = jnp.split(h @ w_in, 3, axis=-1)
    u = c_gate * xv
    u_ext = jnp.concatenate([prev.astype(u.dtype), u], axis=1)
    L = h.shape[1]
    conv = w_conv[0] * u_ext[:, 0:L]
    for j in range(1, CONV_W):
        conv = conv + w_conv[j] * u_ext[:, j:j + L]
    y = (b_gate * conv) @ w_out
    return y, u_ext[:, -(CONV_W - 1):]


def fox_project(h, w_in, b_f):
    B, L, _ = h.shape
    q, k, v, f = jnp.split(h @ w_in, [D_MODEL, 2 * D_MODEL, 3 * D_MODEL], axis=-1)
    shp = (B, L, FOX_HEADS, FOX_HEAD_DIM)
    logf = jax.nn.log_sigmoid((f + b_f).astype(jnp.float32))
    return q.reshape(shp), k.reshape(shp), v.reshape(shp), logf


def fox_attend(q, k, v, cq, ck, q_pos, k_pos):
    s = jnp.einsum('bqhd,bkhd->bhqk', q.astype(jnp.float32), k.astype(jnp.float32)) * (FOX_HEAD_DIM ** -0.5)
    s = s + jnp.swapaxes(cq, 1, 2)[..., :, None] - jnp.swapaxes(ck, 1, 2)[..., None, :]
    s = jnp.where(k_pos[None, :] <= q_pos[:, None], s, NEG_INF)
    p = jax.nn.softmax(s, axis=-1)
    return jnp.einsum('bhqk,bkhd->bqhd', p.astype(v.dtype), v)


def fox_prompt(h, w_in, b_f, w_out):
    B, L, _ = h.shape
    q, k, v, logf = fox_project(h, w_in, b_f)
    c = jnp.cumsum(logf, axis=1)
    outs = []
    for blk in range(L // FOX_BLOCK_Q):
        s0, e = blk * FOX_BLOCK_Q, (blk + 1) * FOX_BLOCK_Q
        outs.append(fox_attend(q[:, s0:e], k[:, :e], v[:, :e], c[:, s0:e], c[:, :e],
                               jnp.arange(s0, e), jnp.arange(e)))
    o = jnp.concatenate(outs, axis=1).reshape(B, L, D_MODEL)
    return o @ w_out, k, v, logf


def fox_sample(h, pool_k, pool_v, pool_logf, page_table, w_in, b_f, w_out):
    B, L, _ = h.shape
    q, k, v, logf = fox_project(h, w_in, b_f)
    P = page_table.shape[1] * PAGE_SIZE
    k_past = pool_k[page_table].reshape(B, P, FOX_HEADS, FOX_HEAD_DIM).astype(k.dtype)
    v_past = pool_v[page_table].reshape(B, P, FOX_HEADS, FOX_HEAD_DIM).astype(v.dtype)
    lf_past = pool_logf[page_table].reshape(B, P, FOX_HEADS).astype(jnp.float32)
    k_all = jnp.concatenate([k_past, k], axis=1)
    v_all = jnp.concatenate([v_past, v], axis=1)
    c_all = jnp.cumsum(jnp.concatenate([lf_past, logf], axis=1), axis=1)
    o = fox_attend(q, k_all, v_all, c_all[:, P:], c_all, P + jnp.arange(L), jnp.arange(P + L))
    return o.reshape(B, L, D_MODEL) @ w_out, k, v, logf


def mlstm_chunkwise(q, k, v, logi, logf, C0, n0, m0):
    B, L, H, _ = q.shape
    T = math.gcd(L, ML_CHUNK)
    nc = L // T

    def to_chunks(a):
        return jnp.swapaxes(a.reshape((B, nc, T) + a.shape[2:]), 0, 1)

    causal = jnp.tril(jnp.ones((T, T), dtype=bool))

    def step(carry, inp):
        C, n, m = carry
        qc, kc, vc, ic, fc = inp
        b = jnp.swapaxes(jnp.cumsum(fc, axis=1), 1, 2)
        ih = jnp.swapaxes(ic, 1, 2)
        logw = jnp.where(causal, b[..., :, None] - b[..., None, :] + ih[..., None, :], NEG_INF)
        inter = b + m[..., None]
        m_t = jnp.maximum(inter, jnp.max(logw, axis=-1))
        w_inter = jnp.exp(inter - m_t)
        a = jnp.exp(logw - m_t[..., None]) * jnp.einsum('bthd,bshd->bhts', qc, kc)
        num = w_inter[..., None] * jnp.einsum('bthd,bhde->bhte', qc, C) + jnp.einsum('bhts,bshe->bhte', a, vc)
        den = w_inter * jnp.einsum('bthd,bhd->bht', qc, n) + jnp.sum(a, axis=-1)
        h = num / jnp.maximum(jnp.abs(den), jnp.exp(-m_t))[..., None]
        m_new = m_t[..., -1]
        decay = jnp.exp(b[..., -1] + m - m_new)
        wk = jnp.exp(b[..., -1:] - b + ih - m_new[..., None])
        C_new = decay[..., None, None] * C + jnp.einsum('bhs,bshd,bshe->bhde', wk, kc, vc)
        n_new = decay[..., None] * n + jnp.einsum('bhs,bshd->bhd', wk, kc)
        return (C_new, n_new, m_new), jnp.swapaxes(h, 1, 2)

    (C, n, m), hs = lax.scan(step, (C0, n0, m0), tuple(to_chunks(a) for a in (q, k, v, logi, logf)))
    hs = jnp.swapaxes(hs, 0, 1).reshape(B, L, H, v.shape[-1])
    return hs, C, n, m


def mlstm_mixer(h, C0, n0, m0, w_in, b_gates, g_norm, w_out):
    B, L, _ = h.shape
    hq = ML_HEADS * ML_DQK
    q, k, v, o, g = jnp.split(h @ w_in, [hq, 2 * hq, 2 * hq + D_MODEL, 2 * hq + 2 * D_MODEL], axis=-1)
    gates = (g + b_gates).astype(jnp.float32)
    gates = GATE_CAP * jnp.tanh(gates / GATE_CAP)
    logi, f_pre = jnp.split(gates, 2, axis=-1)
    logf = jax.nn.log_sigmoid(f_pre)
    q = q.reshape(B, L, ML_HEADS, ML_DQK).astype(jnp.float32)
    k = k.reshape(B, L, ML_HEADS, ML_DQK).astype(jnp.float32) * (ML_DQK ** -0.5)
    v = v.reshape(B, L, ML_HEADS, ML_DV).astype(jnp.float32)
    hs, C, n, m = mlstm_chunkwise(q, k, v, logi, logf, C0.astype(jnp.float32),
                                  n0.astype(jnp.float32), m0.astype(jnp.float32))
    hn = hs * lax.rsqrt(jnp.mean(hs * hs, axis=-1, keepdims=True) + EPS)
    hn = hn.reshape(B, L, D_MODEL) * g_norm.astype(jnp.float32)
    y = (jax.nn.sigmoid(o) * hn.astype(o.dtype)) @ w_out
    return y, C, n, m


def setup_inputs(seed: int = 0) -> dict:
    key = jax.random.key(seed)
    ks = iter(jax.random.split(key, 40))
    f32 = jnp.float32

    def nrm(shape, scale):
        return jax.random.normal(next(ks), shape, f32) * scale

    n_pages = PAST_LEN // PAGE_SIZE
    n_used = DEC_BATCH * n_pages
    n_pool = n_used + max(1, n_used // 4)
    d = D_MODEL
    ml_in = 2 * ML_HEADS * ML_DQK + 2 * d + 2 * ML_HEADS
    inp = {}
    inp['x_prompt'] = nrm((BATCH, SEQ, d), 1.0)
    inp['x_sample'] = nrm((DEC_BATCH, DEC_SEQ, d), 1.0)
    inp['state_conv'] = nrm((N_CONV, DEC_BATCH, CONV_W - 1, d), 1.0)
    inp['cache_k'] = nrm((N_FOX, n_pool, PAGE_SIZE, FOX_HEADS, FOX_HEAD_DIM), 1.0)
    inp['cache_v'] = nrm((N_FOX, n_pool, PAGE_SIZE, FOX_HEADS, FOX_HEAD_DIM), 1.0)
    inp['cache_logf'] = jax.nn.log_sigmoid(FOX_F_BIAS + nrm((N_FOX, n_pool, PAGE_SIZE, FOX_HEADS), 1.0))
    inp['page_table'] = jax.random.permutation(next(ks), n_pool)[:n_used].reshape(DEC_BATCH, n_pages).astype(jnp.int32)
    inp['state_C'] = nrm((N_ML, DEC_BATCH, ML_HEADS, ML_DQK, ML_DV), 0.3)
    inp['state_n'] = nrm((N_ML, DEC_BATCH, ML_HEADS, ML_DQK), 0.3)
    inp['state_m'] = nrm((N_ML, DEC_BATCH, ML_HEADS), 1.0)
    inp['norm_mix'] = 1.0 + nrm((DEPTH, d), 0.02)
    inp['norm_ffn'] = 1.0 + nrm((DEPTH, d), 0.02)
    inp['norm_final'] = 1.0 + nrm((d,), 0.02)
    inp['w_conv_in'] = nrm((N_CONV, d, 3 * d), d ** -0.5)
    inp['w_conv'] = nrm((N_CONV, CONV_W, d), CONV_W ** -0.5)
    inp['w_conv_out'] = nrm((N_CONV, d, d), d ** -0.5)
    inp['w_fox_in'] = nrm((N_FOX, d, 3 * d + FOX_HEADS), d ** -0.5)
    inp['b_fox_f'] = FOX_F_BIAS + nrm((N_FOX, FOX_HEADS), 0.1)
    inp['w_fox_out'] = nrm((N_FOX, d, d), d ** -0.5)
    inp['w_ml_in'] = nrm((N_ML, d, ml_in), d ** -0.5)
    inp['b_ml_gates'] = jnp.concatenate([nrm((N_ML, ML_HEADS), 0.1),
                                         ML_F_BIAS + nrm((N_ML, ML_HEADS), 0.1)], axis=-1)
    inp['g_ml_norm'] = 1.0 + nrm((N_ML, d), 0.02)
    inp['w_ml_out'] = nrm((N_ML, d, d), d ** -0.5)
    inp['w_ffn_gu'] = nrm((DEPTH, d, 2 * D_FF), d ** -0.5)
    inp['w_ffn_down'] = nrm((DEPTH, D_FF, d), D_FF ** -0.5)
    return inp


def reference(x_prompt, x_sample, state_conv, cache_k, cache_v, cache_logf, page_table,
              state_C, state_n, state_m, norm_mix, norm_ffn, norm_final,
              w_conv_in, w_conv, w_conv_out, w_fox_in, b_fox_f, w_fox_out,
              w_ml_in, b_ml_gates, g_ml_norm, w_ml_out, w_ffn_gu, w_ffn_down):
    xp, xs = x_prompt, x_sample
    Bp = xp.shape[0]
    conv_p, conv_s = [], []
    kp_l, vp_l, lfp_l, ks_l, vs_l, lfs_l = [], [], [], [], [], []
    Cp_l, np_l, mp_l, Cs_l, ns_l, ms_l = [], [], [], [], [], []
    for i in range(DEPTH):
        kind, j = i % N_MIXERS, i // N_MIXERS
        hp = rmsnorm(xp, norm_mix[i])
        hs = rmsnorm(xs, norm_mix[i])
        if kind == 0:
            zero_prev = jnp.zeros((Bp, CONV_W - 1, D_MODEL), xp.dtype)
            yp, sp = short_conv_mixer(hp, zero_prev, w_conv_in[j], w_conv[j], w_conv_out[j])
            ys, ss = short_conv_mixer(hs, state_conv[j], w_conv_in[j], w_conv[j], w_conv_out[j])
            conv_p.append(sp)
            conv_s.append(ss)
        elif kind == 1:
            yp, kp, vp, lfp = fox_prompt(hp, w_fox_in[j], b_fox_f[j], w_fox_out[j])
            ys, kn, vn, lfn = fox_sample(hs, cache_k[j], cache_v[j], cache_logf[j], page_table,
                                         w_fox_in[j], b_fox_f[j], w_fox_out[j])
            kp_l.append(kp); vp_l.append(vp); lfp_l.append(lfp)
            ks_l.append(kn); vs_l.append(vn); lfs_l.append(lfn)
        else:
            C0 = jnp.zeros((Bp, ML_HEADS, ML_DQK, ML_DV), jnp.float32)
            n0 = jnp.zeros((Bp, ML_HEADS, ML_DQK), jnp.float32)
            m0 = jnp.zeros((Bp, ML_HEADS), jnp.float32)
            yp, Cp, npr, mp = mlstm_mixer(hp, C0, n0, m0, w_ml_in[j], b_ml_gates[j], g_ml_norm[j], w_ml_out[j])
            ys, Cs, nsm, ms = mlstm_mixer(hs, state_C[j], state_n[j], state_m[j],
                                          w_ml_in[j], b_ml_gates[j], g_ml_norm[j], w_ml_out[j])
            Cp_l.append(Cp); np_l.append(npr); mp_l.append(mp)
            Cs_l.append(Cs); ns_l.append(nsm); ms_l.append(ms)
        xp = xp + yp
        xs = xs + ys
        xp = xp + swiglu_ffn(rmsnorm(xp, norm_ffn[i]), w_ffn_gu[i], w_ffn_down[i])
        xs = xs + swiglu_ffn(rmsnorm(xs, norm_ffn[i]), w_ffn_gu[i], w_ffn_down[i])
    y_prompt = rmsnorm(xp, norm_final)
    y_sample = rmsnorm(xs, norm_final)
    return (y_prompt, y_sample,
            jnp.stack(conv_p), jnp.stack(conv_s),
            jnp.stack(kp_l), jnp.stack(vp_l), jnp.stack(lfp_l),
            jnp.stack(ks_l), jnp.stack(vs_l), jnp.stack(lfs_l),
            jnp.stack(Cp_l), jnp.stack(np_l), jnp.stack(mp_l),
            jnp.stack(Cs_l), jnp.stack(ns_l), jnp.stack(ms_l))
```

```python
import functools

import jax
import jax.numpy as jnp
from jax import lax
from jax.experimental import pallas as pl
from jax.experimental.pallas import tpu as pltpu

F32 = jnp.float32
BF16 = jnp.bfloat16
HIGHEST = lax.Precision.HIGHEST

D_MODEL = 1024
CONV_W = 3
FOX_HEADS = 16
FOX_HEAD_DIM = D_MODEL // FOX_HEADS
ML_HEADS = 8
ML_DV = D_MODEL // ML_HEADS
ML_DQK = ML_DV // 2
GATE_CAP = 15.0
EPS = 1e-6
NEG_INF = -1e30

LANES = 128
SUBLANES = 8
ROW_TILE = 512
FFN_SPLIT = 2
ATTN_TQ = 512
ML_CHUNK_PROMPT = 256
VMEM_LIMIT = 56 * 1024 * 1024


def _params(*sem):
    return pltpu.CompilerParams(dimension_semantics=sem, vmem_limit_bytes=VMEM_LIMIT)


def _rms(x, g):
    return x * lax.rsqrt(jnp.mean(x * x, axis=-1, keepdims=True) + EPS) * g


def _log_sigmoid(x):
    return jnp.minimum(x, 0.0) - jnp.log1p(jnp.exp(-jnp.abs(x)))


def _dot(a, b):
    return jnp.dot(a, b, preferred_element_type=F32)


def _dot_hi(a, b):
    return jnp.dot(a, b, precision=HIGHEST, preferred_element_type=F32)


def _dot_nt(a, b, precision=None):
    return lax.dot_general(a, b, (((1,), (1,)), ((), ())), precision=precision,
                           preferred_element_type=F32)


def _dot_tn(a, b):
    return lax.dot_general(a, b, (((0,), (0,)), ((), ())), preferred_element_type=F32)


def _lane_tile(x, width):
    return jnp.tile(x, (1, width // LANES))


def _post_kernel(x_ref, a_ref, wo_ref, g_ref, wg_ref, wu_ref, wd_ref, gf_ref, o_ref,
                 x1_sc, h_sc, acc_sc, *, final_norm):
    j = pl.program_id(1)

    @pl.when(j == 0)
    def _():
        x1 = x_ref[...] + _dot(a_ref[...], wo_ref[...])
        x1_sc[...] = x1
        h_sc[...] = _rms(x1, g_ref[...]).astype(BF16)

    h = h_sc[...]
    g = _dot(h, wg_ref[...])
    u = _dot(h, wu_ref[...])
    part = _dot(((g * jax.nn.sigmoid(g)) * u).astype(BF16), wd_ref[...])

    @pl.when(j == 0)
    def _():
        acc_sc[...] = part

    @pl.when(j > 0)
    def _():
        acc_sc[...] += part

    @pl.when(j == pl.num_programs(1) - 1)
    def _():
        y = x1_sc[...] + acc_sc[...]
        o_ref[...] = _rms(y, gf_ref[...]) if final_norm else y


def _post(x, a, wo, g, wg, wu, wd, gf, *, final_norm):
    m, d = x.shape
    dff = wg.shape[1]
    tf = dff // FFN_SPLIT
    assert tf * FFN_SPLIT == dff and tf % LANES == 0 and m % ROW_TILE == 0
    return pl.pallas_call(
        functools.partial(_post_kernel, final_norm=final_norm),
        grid=(m // ROW_TILE, FFN_SPLIT),
        in_specs=[
            pl.BlockSpec((ROW_TILE, d), lambda i, j: (i, 0)),
            pl.BlockSpec((ROW_TILE, d), lambda i, j: (i, 0)),
            pl.BlockSpec((d, d), lambda i, j: (0, 0)),
            pl.BlockSpec((1, d), lambda i, j: (0, 0)),
            pl.BlockSpec((d, tf), lambda i, j: (0, j)),
            pl.BlockSpec((d, tf), lambda i, j: (0, j)),
            pl.BlockSpec((tf, d), lambda i, j: (j, 0)),
            pl.BlockSpec((1, d), lambda i, j: (0, 0)),
        ],
        out_specs=pl.BlockSpec((ROW_TILE, d), lambda i, j: (i, 0)),
        out_shape=jax.ShapeDtypeStruct((m, d), F32),
        scratch_shapes=[pltpu.VMEM((ROW_TILE, d), F32), pltpu.VMEM((ROW_TILE, d), BF16),
                        pltpu.VMEM((ROW_TILE, d), F32)],
        compiler_params=_params("parallel", "arbitrary"),
        name="post_ffn",
    )(x, a, wo, g, wg, wu, wd, gf)


def _conv_front(x_ref, g_ref, win_ref):
    d = x_ref.shape[-1]
    h = _rms(x_ref[...], g_ref[...]).astype(BF16)
    p = _dot(h, win_ref[...])
    return p[:, :d], p[:, d:2 * d] * p[:, 2 * d:]


def _conv_taps(u, rig, p0, p1, wc_ref):
    s1 = jnp.where(rig == 0, p1, pltpu.roll(u, 1, axis=0))
    s2 = jnp.where(rig == 0, p0, jnp.where(rig == 1, p1, pltpu.roll(u, 2, axis=0)))
    return wc_ref[0:1, :] * s2 + wc_ref[1:2, :] * s1 + wc_ref[2:3, :] * u


def _conv_prompt_kernel(x_ref, g_ref, win_ref, wc_ref, prev_ref, z_ref, st_ref, carry_sc):
    @pl.when(pl.program_id(1) == 0)
    def _():
        carry_sc[...] = prev_ref[0]

    bg, u = _conv_front(x_ref, g_ref, win_ref)
    r = u.shape[0]
    rig = lax.broadcasted_iota(jnp.int32, u.shape, 0)
    conv = _conv_taps(u, rig, carry_sc[0:1, :], carry_sc[1:2, :], wc_ref)
    z_ref[...] = (bg * conv).astype(BF16)
    tail = u[r - (CONV_W - 1):, :]
    carry_sc[...] = tail
    st_ref[0] = tail


def _conv_sample_kernel(x_ref, g_ref, win_ref, wc_ref, prev_ref, z_ref, st_ref):
    bg, u = _conv_front(x_ref, g_ref, win_ref)
    r, d = u.shape
    n_seq = r // SUBLANES
    prev = prev_ref[...]
    p0 = jnp.broadcast_to(prev[:, 0:1, :], (n_seq, SUBLANES, d)).reshape(r, d)
    p1 = jnp.broadcast_to(prev[:, 1:2, :], (n_seq, SUBLANES, d)).reshape(r, d)
    rig = lax.broadcasted_iota(jnp.int32, u.shape, 0) & (SUBLANES - 1)
    conv = _conv_taps(u, rig, p0, p1, wc_ref)
    z_ref[...] = (bg * conv).astype(BF16)
    st_ref[...] = u.reshape(n_seq, SUBLANES, d)[:, SUBLANES - (CONV_W - 1):, :]


def _conv_mixer(x, g, win, wc, prev, seq_len):
    m, d = x.shape
    n_seq = m // seq_len
    common_in = [pl.BlockSpec((d, 3 * d), lambda *_: (0, 0)), pl.BlockSpec((CONV_W, d), lambda *_: (0, 0))]
    out_shape = (jax.ShapeDtypeStruct((m, d), BF16), jax.ShapeDtypeStruct((n_seq, CONV_W - 1, d), F32))
    if seq_len == SUBLANES:
        per = ROW_TILE // SUBLANES
        return pl.pallas_call(
            _conv_sample_kernel,
            grid=(m // ROW_TILE,),
            in_specs=[pl.BlockSpec((ROW_TILE, d), lambda i: (i, 0)), pl.BlockSpec((1, d), lambda i: (0, 0)),
                      *common_in, pl.BlockSpec((per, CONV_W - 1, d), lambda i: (i, 0, 0))],
            out_specs=(pl.BlockSpec((ROW_TILE, d), lambda i: (i, 0)),
                       pl.BlockSpec((per, CONV_W - 1, d), lambda i: (i, 0, 0))),
            out_shape=out_shape,
            compiler_params=_params("parallel"),
            name="conv_sample",
        )(x, g, win, wc, prev)
    assert seq_len % ROW_TILE == 0
    nl = seq_len // ROW_TILE
    return pl.pallas_call(
        _conv_prompt_kernel,
        grid=(n_seq, nl),
        in_specs=[pl.BlockSpec((ROW_TILE, d), lambda b, l: (b * nl + l, 0)),
                  pl.BlockSpec((1, d), lambda b, l: (0, 0)),
                  *common_in, pl.BlockSpec((1, CONV_W - 1, d), lambda b, l: (b, 0, 0))],
        out_specs=(pl.BlockSpec((ROW_TILE, d), lambda b, l: (b * nl + l, 0)),
                   pl.BlockSpec((1, CONV_W - 1, d), lambda b, l: (b, 0, 0))),
        out_shape=out_shape,
        scratch_shapes=[pltpu.VMEM((CONV_W - 1, d), F32)],
        compiler_params=_params("parallel", "arbitrary"),
        name="conv_prompt",
    )(x, g, win, wc, prev)


def _fox_proj_kernel(x_ref, g_ref, wqkv_ref, wf_ref, bf_ref, *refs, prompt):
    d = x_ref.shape[-1]
    h = _rms(x_ref[...], g_ref[...]).astype(BF16)
    p = _dot(h, wqkv_ref[...])
    q = p[:, :d] * (FOX_HEAD_DIM ** -0.5)
    k = p[:, d:2 * d]
    v = p[:, 2 * d:]
    logf = _log_sigmoid(_dot(h, wf_ref[...]) + bf_ref[...])
    if prompt:
        q_ref, k_ref, v_ref, kb_ref, vb_ref, lf_ref, c_ref, carry_sc = refs
        q_ref[...] = q.astype(BF16)
        kb_ref[...] = k.astype(BF16)
        vb_ref[...] = v.astype(BF16)

        @pl.when(pl.program_id(1) == 0)
        def _():
            carry_sc[...] = jnp.zeros_like(carry_sc)

        r = logf.shape[0]
        tril = (lax.broadcasted_iota(jnp.int32, (r, r), 1)
                <= lax.broadcasted_iota(jnp.int32, (r, r), 0)).astype(F32)
        c = carry_sc[...] + _dot_hi(tril, logf)
        carry_sc[...] = c[r - 1:r, :]
        c_ref[...] = c[:, :FOX_HEADS]
    else:
        q_ref, k_ref, v_ref, lf_ref = refs
        q_ref[...] = q
    k_ref[...] = k
    v_ref[...] = v
    lf_ref[...] = logf[:, :FOX_HEADS]


def _fox_proj(x, g, wqkv, wf, bf, seq_len, *, prompt):
    m, d = x.shape
    row = lambda b, l: (b * nl + l, 0)
    const = lambda b, l: (0, 0)
    if prompt:
        nl = seq_len // ROW_TILE
        grid = (m // seq_len, nl)
    else:
        nl = 1
        grid = (m // ROW_TILE, 1)
    big = pl.BlockSpec((ROW_TILE, d), row)
    small = pl.BlockSpec((ROW_TILE, FOX_HEADS), row)
    f32_out = jax.ShapeDtypeStruct((m, d), F32)
    bf_out = jax.ShapeDtypeStruct((m, d), BF16)
    h_out = jax.ShapeDtypeStruct((m, FOX_HEADS), F32)
    if prompt:
        out_specs = (big, big, big, big, big, small, small)
        out_shape = (bf_out, f32_out, f32_out, bf_out, bf_out, h_out, h_out)
        scratch = [pltpu.VMEM((1, LANES), F32)]
    else:
        out_specs = (big, big, big, small)
        out_shape = (f32_out, f32_out, f32_out, h_out)
        scratch = []
    return pl.pallas_call(
        functools.partial(_fox_proj_kernel, prompt=prompt),
        grid=grid,
        in_specs=[big, pl.BlockSpec((1, d), const), pl.BlockSpec((d, 3 * d), const),
                  pl.BlockSpec((d, LANES), const), pl.BlockSpec((1, LANES), const)],
        out_specs=out_specs,
        out_shape=out_shape,
        scratch_shapes=scratch,
        compiler_params=_params("parallel", "arbitrary"),
        name="fox_proj_prompt" if prompt else "fox_proj_sample",
    )(x, g, wqkv, wf, bf)


def _fox_attn_kernel(q_ref, k_ref, v_ref, c_ref, ct_ref, o_ref, cq_sc, m_sc, l_sc, acc_sc, *, tq):
    hp = pl.program_id(1)
    qi = pl.program_id(2)
    lane = lax.broadcasted_iota(jnp.int32, (tq, LANES), 1)
    half = (lane < FOX_HEAD_DIM, lane >= FOX_HEAD_DIM)
    q2 = q_ref[...]
    cq_all = c_ref[...]
    sel = lax.broadcasted_iota(jnp.int32, (FOX_HEADS, LANES), 0)
    for hh in range(2):
        cq_sc[hh] = _dot_hi(cq_all, (sel == 2 * hp + hh).astype(F32))
        m_sc[hh] = jnp.full((tq, LANES), NEG_INF, F32)
        l_sc[hh] = jnp.zeros((tq, LANES), F32)
        acc_sc[hh] = jnp.zeros((tq, LANES), F32)

    causal = (lax.broadcasted_iota(jnp.int32, (tq, tq), 1)
              <= lax.broadcasted_iota(jnp.int32, (tq, tq), 0))

    def block(ki, diagonal):
        start = pl.multiple_of(ki * tq, tq)
        k2 = k_ref[pl.ds(start, tq), :]
        v2 = v_ref[pl.ds(start, tq), :]
        for hh in range(2):
            qm = jnp.where(half[hh], q2, jnp.zeros_like(q2))
            ck = ct_ref[0, pl.ds(2 * hp + hh, 1), pl.ds(start, tq)]
            s = _dot_nt(qm, k2) + _lane_tile(cq_sc[hh], tq) - ck
            if diagonal:
                s = jnp.where(causal, s, NEG_INF)
            m_prev = m_sc[hh]
            m_new = jnp.maximum(m_prev, jnp.max(s, axis=1, keepdims=True))
            alpha = jnp.exp(m_prev - m_new)
            p = jnp.exp(s - _lane_tile(m_new, tq))
            l_sc[hh] = alpha * l_sc[hh] + jnp.sum(p, axis=1, keepdims=True)
            acc_sc[hh] = alpha * acc_sc[hh] + _dot(p.astype(BF16), v2)
            m_sc[hh] = m_new

    def body(ki, carry):
        block(ki, False)
        return carry

    lax.fori_loop(0, qi, body, 0)
    block(qi, True)
    o_ref[...] = jnp.where(half[0], acc_sc[0] / l_sc[0], acc_sc[1] / l_sc[1]).astype(BF16)


def _fox_attn_prompt(q, kb, vb, c, ct, seq_len):
    m, d = q.shape
    n_seq = m // seq_len
    tq = ATTN_TQ
    nq = seq_len // tq
    return pl.pallas_call(
        functools.partial(_fox_attn_kernel, tq=tq),
        grid=(n_seq, d // LANES, nq),
        in_specs=[pl.BlockSpec((tq, LANES), lambda b, hp, qi: (b * nq + qi, hp)),
                  pl.BlockSpec((seq_len, LANES), lambda b, hp, qi: (b, hp)),
                  pl.BlockSpec((seq_len, LANES), lambda b, hp, qi: (b, hp)),
                  pl.BlockSpec((tq, FOX_HEADS), lambda b, hp, qi: (b * nq + qi, 0)),
                  pl.BlockSpec((1, FOX_HEADS, seq_len), lambda b, hp, qi: (b, 0, 0))],
        out_specs=pl.BlockSpec((tq, LANES), lambda b, hp, qi: (b * nq + qi, hp)),
        out_shape=jax.ShapeDtypeStruct((m, d), BF16),
        scratch_shapes=[pltpu.VMEM((2, tq, LANES), F32)] * 4,
        compiler_params=_params("parallel", "parallel", "parallel"),
        name="fox_attn_prompt",
    )(q, kb, vb, c, ct)


def _fox_decode_kernel(pt_ref, q_ref, kn_ref, vn_ref, lfn_ref, kc_ref, vc_ref, lfc_ref, o_ref,
                       qbd_sc, a_sc, sfx_sc, m_sc, l_sc, acc_sc):
    del pt_ref
    p = pl.program_id(1)
    page = kc_ref.shape[0]
    n_new, d = q_ref.shape[1], q_ref.shape[2]
    rows = FOX_HEADS * n_new
    assert rows == page == LANES and n_new == SUBLANES
    ri = lax.broadcasted_iota(jnp.int32, (rows, page), 0)
    ki = lax.broadcasted_iota(jnp.int32, (rows, page), 1)
    tok_bits = n_new.bit_length() - 1
    dim_bits = FOX_HEAD_DIM.bit_length() - 1
    head_of_row = ((lax.broadcasted_iota(jnp.int32, (rows, FOX_HEADS), 0) >> tok_bits)
                   == lax.broadcasted_iota(jnp.int32, (rows, FOX_HEADS), 1)).astype(F32)
    own_cols = ((lax.broadcasted_iota(jnp.int32, (rows, d), 0) >> tok_bits)
                == (lax.broadcasted_iota(jnp.int32, (rows, d), 1) >> dim_bits))

    def update(s, vb):
        m_prev = m_sc[...]
        m_new = jnp.maximum(m_prev, jnp.max(s, axis=1, keepdims=True))
        alpha = jnp.exp(m_prev - m_new)
        pr = jnp.exp(s - m_new)
        l_sc[...] = alpha * l_sc[...] + jnp.sum(pr, axis=1, keepdims=True)
        acc_sc[...] = _lane_tile(alpha, d) * acc_sc[...] + _dot(pr.astype(BF16), vb)
        m_sc[...] = m_new

    @pl.when(p == 0)
    def _():
        qbd_sc[...] = jnp.where(own_cols, jnp.tile(q_ref[0], (FOX_HEADS, 1)), 0.0).astype(BF16)
        pad = page - n_new
        kn = jnp.concatenate([kn_ref[0], jnp.zeros((pad, d), F32)], axis=0).astype(BF16)
        vn = jnp.concatenate([vn_ref[0], jnp.zeros((pad, d), F32)], axis=0).astype(BF16)
        lfn = jnp.concatenate([lfn_ref[0], jnp.zeros((pad, FOX_HEADS), F32)], axis=0)
        lf_rows = _dot_nt(head_of_row, lfn, HIGHEST)
        cum = _dot_hi(lf_rows, (ri <= ki).astype(F32))
        t_of_row = ri & (n_new - 1)
        a = jnp.sum(jnp.where(ki == t_of_row, cum, 0.0), axis=1, keepdims=True)
        a_sc[...] = jnp.broadcast_to(a, (rows, page))
        s = _dot_nt(qbd_sc[...], kn) + a - cum
        s = jnp.where(ki <= t_of_row, s, NEG_INF)
        m_sc[...] = jnp.full((rows, page), NEG_INF, F32)
        l_sc[...] = jnp.zeros((rows, page), F32)
        acc_sc[...] = jnp.zeros((rows, d), F32)
        sfx_sc[...] = jnp.zeros((rows, page), F32)
        update(s, vn)

    @pl.when(p > 0)
    def _():
        lf_rows = _dot_nt(head_of_row, lfc_ref[...], HIGHEST)
        sfx_in = _dot_hi(lf_rows, (ri >= ki).astype(F32))
        s = (_dot_nt(qbd_sc[...], kc_ref[...].astype(BF16))
             + a_sc[...] + sfx_sc[...] + (sfx_in - lf_rows))
        update(s, vc_ref[...].astype(BF16))
        sfx_sc[...] += jnp.broadcast_to(sfx_in[:, 0:1], (rows, page))

    @pl.when(p == pl.num_programs(1) - 1)
    def _():
        o = acc_sc[...] / _lane_tile(l_sc[...], d)
        o = jnp.where(own_cols, o, 0.0).reshape(FOX_HEADS, n_new, d)
        o_ref[0] = jnp.sum(o, axis=0).astype(BF16)


def _fox_attn_sample(q, k_new, v_new, lf_new, cache_k, cache_v, cache_lf, page_table):
    nb, n_new, d = q.shape
    n_pages = page_table.shape[1]
    page = cache_k.shape[1]
    rows = FOX_HEADS * n_new

    def new_map(b, p, pt):
        return (b, 0, 0)

    def page_map(b, p, pt):
        return (pt[b * n_pages + n_pages - jnp.maximum(p, 1)], 0, 0)

    grid_spec = pltpu.PrefetchScalarGridSpec(
        num_scalar_prefetch=1,
        grid=(nb, n_pages + 1),
        in_specs=[pl.BlockSpec((1, n_new, d), new_map), pl.BlockSpec((1, n_new, d), new_map),
                  pl.BlockSpec((1, n_new, d), new_map), pl.BlockSpec((1, n_new, FOX_HEADS), new_map),
                  pl.BlockSpec((None, page, d), page_map), pl.BlockSpec((None, page, d), page_map),
                  pl.BlockSpec((None, page, FOX_HEADS), page_map)],
        out_specs=pl.BlockSpec((1, n_new, d), new_map),
        scratch_shapes=[pltpu.VMEM((rows, d), BF16), pltpu.VMEM((rows, page), F32),
                        pltpu.VMEM((rows, page), F32), pltpu.VMEM((rows, page), F32),
                        pltpu.VMEM((rows, page), F32), pltpu.VMEM((rows, d), F32)],
    )
    return pl.pallas_call(
        _fox_decode_kernel,
        grid_spec=grid_spec,
        out_shape=jax.ShapeDtypeStruct((nb, n_new, d), BF16),
        compiler_params=_params("parallel", "arbitrary"),
        name="fox_attn_sample",
    )(page_table.reshape(-1), q, k_new, v_new, lf_new, cache_k, cache_v, cache_lf)


def _ml_proj_kernel(x_ref, g_ref, w_ref, wg_ref, bg_ref, q_ref, k_ref, v_ref, o_ref, gt_ref):
    hq = ML_HEADS * ML_DQK
    d = x_ref.shape[-1]
    h = _rms(x_ref[...], g_ref[...]).astype(BF16)
    p = _dot(h, w_ref[...])
    q_ref[...] = p[:, :hq].astype(BF16)
    k_ref[...] = (p[:, hq:2 * hq] * (ML_DQK ** -0.5)).astype(BF16)
    v_ref[...] = p[:, 2 * hq:2 * hq + d].astype(BF16)
    o_ref[...] = jax.nn.sigmoid(p[:, 2 * hq + d:])
    gates = _dot(h, wg_ref[...]) + bg_ref[...]
    gates = GATE_CAP * jnp.tanh(gates / GATE_CAP)
    lane = lax.broadcasted_iota(jnp.int32, gates.shape, 1)
    gates = jnp.where(lane < ML_HEADS, gates, _log_sigmoid(gates))
    gt_ref[...] = gates[:, :2 * ML_HEADS]


def _ml_proj(x, g, w, wg, bg):
    m, d = x.shape
    hq = ML_HEADS * ML_DQK
    row = lambda i: (i, 0)
    const = lambda i: (0, 0)
    return pl.pallas_call(
        _ml_proj_kernel,
        grid=(m // ROW_TILE,),
        in_specs=[pl.BlockSpec((ROW_TILE, d), row), pl.BlockSpec((1, d), const),
                  pl.BlockSpec((d, 2 * hq + 2 * d), const), pl.BlockSpec((d, LANES), const),
                  pl.BlockSpec((1, LANES), const)],
        out_specs=(pl.BlockSpec((ROW_TILE, hq), row), pl.BlockSpec((ROW_TILE, hq), row),
                   pl.BlockSpec((ROW_TILE, d), row), pl.BlockSpec((ROW_TILE, d), row),
                   pl.BlockSpec((ROW_TILE, 2 * ML_HEADS), row)),
        out_shape=(jax.ShapeDtypeStruct((m, hq), BF16), jax.ShapeDtypeStruct((m, hq), BF16),
                   jax.ShapeDtypeStruct((m, d), BF16), jax.ShapeDtypeStruct((m, d), F32),
                   jax.ShapeDtypeStruct((m, 2 * ML_HEADS), F32)),
        compiler_params=_params("parallel"),
        name="ml_proj",
    )(x, g, w, wg, bg)


def _ml_chunk_kernel(q_ref, k_ref, v_ref, og_ref, gt_ref, gtt_ref, gn_ref, c0_ref, n0_ref, m0_ref,
                     a_ref, c_ref, n_ref, m_ref):
    t = q_ref.shape[1]

    @pl.when(pl.program_id(1) == 0)
    def _():
        c_ref[...] = c0_ref[...]
        n_ref[...] = n0_ref[...]
        m_ref[...] = m0_ref[...]

    gates = gt_ref[0]
    gates_t = gtt_ref[0]
    ri = lax.broadcasted_iota(jnp.int32, (t, t), 0)
    ci = lax.broadcasted_iota(jnp.int32, (t, t), 1)
    causal = ci <= ri
    b_cols = _dot_hi(causal.astype(F32), gates)
    b_rows = _dot_hi(gates_t, (ri <= ci).astype(F32))
    lane = lax.broadcasted_iota(jnp.int32, (t, LANES), 1)
    half = (lane < ML_DQK, lane >= ML_DQK)
    lane1 = lax.broadcasted_iota(jnp.int32, (1, LANES), 1)
    row_sq = lax.broadcasted_iota(jnp.int32, (LANES, LANES), 0)
    head_lane = lax.broadcasted_iota(jnp.int32, (1, ML_HEADS), 1)
    m_all = m_ref[0]
    m_out = jnp.zeros((1, ML_HEADS), F32)

    for j in range(ML_HEADS // 2):
        q2 = q_ref[0, :, j * LANES:(j + 1) * LANES]
        k2 = k_ref[0, :, j * LANES:(j + 1) * LANES]
        c2 = c_ref[0, j]
        c2b = c2.astype(BF16)
        n2 = n_ref[0, j:j + 1, :]
        upd = jnp.zeros((LANES, LANES), F32)
        n_add = jnp.zeros((1, LANES), F32)
        decays = []
        for hh in range(2):
            h = 2 * j + hh
            sl = slice(h * ML_DV, (h + 1) * ML_DV)
            qm = jnp.where(half[hh], q2, jnp.zeros_like(q2))
            km = jnp.where(half[hh], k2, jnp.zeros_like(k2))
            v_h = v_ref[0, :, sl]
            i_c = gates[:, h:h + 1]
            i_r = gates_t[h:h + 1, :]
            b_c = b_cols[:, ML_HEADS + h:ML_HEADS + h + 1]
            b_r = b_rows[ML_HEADS + h:ML_HEADS + h + 1, :]
            m_prev = m_all[:, h:h + 1]
            logw = jnp.where(causal, b_c - b_r + i_r, NEG_INF)
            inter = b_c + m_prev
            m_t = jnp.maximum(inter, jnp.max(logw, axis=1, keepdims=True))
            w_inter = jnp.exp(inter - m_t)
            a = jnp.exp(logw - m_t) * _dot_nt(qm, k2)
            num = w_inter * _dot(qm, c2b) + _dot(a.astype(BF16), v_h)
            qn = jnp.sum(qm.astype(F32) * n2, axis=1, keepdims=True)
            den = w_inter * qn + jnp.sum(a, axis=1, keepdims=True)
            hs = num / jnp.maximum(jnp.abs(den), jnp.exp(-m_t))
            hn = hs * lax.rsqrt(jnp.mean(hs * hs, axis=1, keepdims=True) + EPS) * gn_ref[:, sl]
            a_ref[0, :, sl] = (og_ref[0, :, sl] * hn).astype(BF16)
            b_last = b_c[t - 1:t, :]
            m_new = m_t[t - 1:t, :]
            decays.append(jnp.exp(b_last + m_prev - m_new))
            kw = jnp.exp(b_last - b_c + i_c - m_new) * km.astype(F32)
            upd = upd + _dot_tn(kw.astype(BF16), v_h)
            n_add = n_add + jnp.sum(kw, axis=0, keepdims=True)
            m_out = jnp.where(head_lane == h, m_new, m_out)
        c_ref[0, j] = jnp.where(row_sq < ML_DQK, decays[0], decays[1]) * c2 + upd
        n_ref[0, j:j + 1, :] = jnp.where(lane1 < ML_DQK, decays[0], decays[1]) * n2 + n_add
    m_ref[0] = m_out


def _ml_chunk(q, k, v, og, gates, gates_t, gn, c0, n0, m0, seq_len, chunk):
    nb, _, d = v.shape
    hq = q.shape[2]
    nc = seq_len // chunk
    pairs = ML_HEADS // 2
    tok = lambda width: pl.BlockSpec((1, chunk, width), lambda b, c: (b, c, 0))
    state4 = pl.BlockSpec((1, pairs, LANES, LANES), lambda b, c: (b, 0, 0, 0))
    state3 = pl.BlockSpec((1, pairs, LANES), lambda b, c: (b, 0, 0))
    state_m = pl.BlockSpec((1, 1, ML_HEADS), lambda b, c: (b, 0, 0))
    return pl.pallas_call(
        _ml_chunk_kernel,
        grid=(nb, nc),
        in_specs=[tok(hq), tok(hq), tok(d), tok(d), tok(2 * ML_HEADS),
                  pl.BlockSpec((1, 2 * ML_HEADS, chunk), lambda b, c: (b * nc + c, 0, 0)),
                  pl.BlockSpec((1, d), lambda b, c: (0, 0)), state4, state3, state_m],
        out_specs=(tok(d), state4, state3, state_m),
        out_shape=(jax.ShapeDtypeStruct((nb, seq_len, d), BF16),
                   jax.ShapeDtypeStruct((nb, pairs, LANES, LANES), F32),
                   jax.ShapeDtypeStruct((nb, pairs, LANES), F32),
                   jax.ShapeDtypeStruct((nb, 1, ML_HEADS), F32)),
        compiler_params=_params("parallel", "arbitrary"),
        name="ml_chunk",
    )(q, k, v, og, gates, gates_t, gn, c0, n0, m0)


def _pad_cols(w, width):
    return jnp.pad(w, ((0, 0), (0, width - w.shape[1])))


def kernel(x_prompt, x_sample, state_conv, cache_k, cache_v, cache_logf, page_table, state_C, state_n, state_m, norm_mix, norm_ffn, norm_final, w_conv_in, w_conv, w_conv_out, w_fox_in, b_fox_f, w_fox_out, w_ml_in, b_ml_gates, g_ml_norm, w_ml_out, w_ffn_gu, w_ffn_down):
    bp, lp, d = x_prompt.shape
    bs, ls, _ = x_sample.shape
    depth = norm_mix.shape[0]
    dff = w_ffn_down.shape[1]
    assert d == D_MODEL and ls == SUBLANES
    xp = x_prompt.reshape(bp * lp, d)
    xs = x_sample.reshape(bs * ls, d)
    gf = norm_final.reshape(1, d)
    hq = ML_HEADS * ML_DQK
    pairs = ML_HEADS // 2
    out = {name: [] for name in ("conv_p", "conv_s", "kp", "vp", "lfp", "ks", "vs", "lfs",
                                 "cp", "np", "mp", "cs", "ns", "ms")}

    for i in range(depth):
        kind, j = i % 3, i // 3
        gm = norm_mix[i].reshape(1, d)
        if kind == 0:
            win = w_conv_in[j].astype(BF16)
            wo = w_conv_out[j].astype(BF16)
            ap, sp = _conv_mixer(xp, gm, win, w_conv[j], jnp.zeros((bp, CONV_W - 1, d), F32), lp)
            as_, ss = _conv_mixer(xs, gm, win, w_conv[j], state_conv[j], ls)
            out["conv_p"].append(sp)
            out["conv_s"].append(ss)
        elif kind == 1:
            wqkv = w_fox_in[j][:, :3 * d].astype(BF16)
            wf = _pad_cols(w_fox_in[j][:, 3 * d:], LANES).astype(BF16)
            bf = _pad_cols(b_fox_f[j].reshape(1, FOX_HEADS), LANES)
            wo = w_fox_out[j].astype(BF16)
            q, k, v, kb, vb, lf, c = _fox_proj(xp, gm, wqkv, wf, bf, lp, prompt=True)
            ct = jnp.swapaxes(c.reshape(bp, lp, FOX_HEADS), 1, 2)
            ap = _fox_attn_prompt(q, kb, vb, c, ct, lp)
            out["kp"].append(k.reshape(bp, lp, FOX_HEADS, FOX_HEAD_DIM))
            out["vp"].append(v.reshape(bp, lp, FOX_HEADS, FOX_HEAD_DIM))
            out["lfp"].append(lf.reshape(bp, lp, FOX_HEADS))
            q, k, v, lf = _fox_proj(xs, gm, wqkv, wf, bf, ls, prompt=False)
            n_pool, page = cache_k.shape[1], cache_k.shape[2]
            as_ = _fox_attn_sample(
                q.reshape(bs, ls, d), k.reshape(bs, ls, d), v.reshape(bs, ls, d),
                lf.reshape(bs, ls, FOX_HEADS), cache_k[j].reshape(n_pool, page, d),
                cache_v[j].reshape(n_pool, page, d), cache_logf[j], page_table).reshape(bs * ls, d)
            out["ks"].append(k.reshape(bs, ls, FOX_HEADS, FOX_HEAD_DIM))
            out["vs"].append(v.reshape(bs, ls, FOX_HEADS, FOX_HEAD_DIM))
            out["lfs"].append(lf.reshape(bs, ls, FOX_HEADS))
        else:
            w = w_ml_in[j][:, :2 * hq + 2 * d].astype(BF16)
            wg = _pad_cols(w_ml_in[j][:, 2 * hq + 2 * d:], LANES).astype(BF16)
            bg = _pad_cols(b_ml_gates[j].reshape(1, 2 * ML_HEADS), LANES)
            gn = g_ml_norm[j].reshape(1, d)
            wo = w_ml_out[j].astype(BF16)

            def run(x, nb, seq_len, chunk, c0, n0, m0):
                q, k, v, og, gates = _ml_proj(x, gm, w, wg, bg)
                nc = seq_len // chunk
                gates_t = jnp.swapaxes(gates.reshape(nb * nc, chunk, 2 * ML_HEADS), 1, 2)
                tok = lambda t: t.reshape(nb, seq_len, t.shape[-1])
                a, c1, n1, m1 = _ml_chunk(tok(q), tok(k), tok(v), tok(og), tok(gates), gates_t, gn,
                                          c0.reshape(nb, pairs, LANES, LANES),
                                          n0.reshape(nb, pairs, LANES),
                                          m0.reshape(nb, 1, ML_HEADS), seq_len, chunk)
                return (a.reshape(nb * seq_len, d), c1.reshape(nb, ML_HEADS, ML_DQK, ML_DV), n1.reshape(nb, ML_HEADS, ML_DQK),
                        m1.reshape(nb, ML_HEADS))

            ap, c1, n1, m1 = run(xp, bp, lp, ML_CHUNK_PROMPT,
                                 jnp.zeros((bp, ML_HEADS, ML_DQK, ML_DV), F32),
                                 jnp.zeros((bp, ML_HEADS, ML_DQK), F32), jnp.zeros((bp, ML_HEADS), F32))
            out["cp"].append(c1); out["np"].append(n1); out["mp"].append(m1)
            as_, c1, n1, m1 = run(xs, bs, ls, ls, state_C[j], state_n[j], state_m[j])
            out["cs"].append(c1); out["ns"].append(n1); out["ms"].append(m1)

        gffn = norm_ffn[i].reshape(1, d)
        wg_ffn = w_ffn_gu[i][:, :dff].astype(BF16)
        wu_ffn = w_ffn_gu[i][:, dff:].astype(BF16)
        wd_ffn = w_ffn_down[i].astype(BF16)
        last = i == depth - 1
        xp = _post(xp, ap, wo, gffn, wg_ffn, wu_ffn, wd_ffn, gf, final_norm=last)
        xs = _post(xs, as_, wo, gffn, wg_ffn, wu_ffn, wd_ffn, gf, final_norm=last)

    st = lambda name: jnp.stack(out[name])
    return (xp.reshape(bp, lp, d), xs.reshape(bs, ls, d), st("conv_p"), st("conv_s"),
            st("kp"), st("vp"), st("lfp"), st("ks"), st("vs"), st("lfs"),
            st("cp"), st("np"), st("mp"), st("cs"), st("ns"), st("ms"))
```

```python
import functools

import jax
import jax.numpy as jnp
from jax import lax
from jax.experimental import pallas as pl
from jax.experimental.pallas import tpu as pltpu

F32 = jnp.float32
BF16 = jnp.bfloat16
HIGHEST = lax.Precision.HIGHEST

D_MODEL = 1024
CONV_W = 3
FOX_HEADS = 16
FOX_HEAD_DIM = D_MODEL // FOX_HEADS
ML_HEADS = 8
ML_DV = D_MODEL // ML_HEADS
ML_DQK = ML_DV // 2
GATE_CAP = 15.0
EPS = 1e-6
NEG_INF = -1e30

LANES = 128
SUBLANES = 8
ROW_TILE = 512
FFN_SPLIT = 2
ATTN_TQ = 512
ML_CHUNK_PROMPT = 256
VMEM_LIMIT = 56 * 1024 * 1024


def _params(*sem):
    return pltpu.CompilerParams(dimension_semantics=sem, vmem_limit_bytes=VMEM_LIMIT)


def _rms(x, g):
    return x * lax.rsqrt(jnp.mean(x * x, axis=-1, keepdims=True) + EPS) * g


def _log_sigmoid(x):
    return jnp.minimum(x, 0.0) - jnp.log1p(jnp.exp(-jnp.abs(x)))


def _dot(a, b):
    return jnp.dot(a, b, preferred_element_type=F32)


def _dot_hi(a, b):
    return jnp.dot(a, b, precision=HIGHEST, preferred_element_type=F32)


def _dot_nt(a, b, precision=None):
    return lax.dot_general(a, b, (((1,), (1,)), ((), ())), precision=precision,
                           preferred_element_type=F32)


def _dot_tn(a, b):
    return lax.dot_general(a, b, (((0,), (0,)), ((), ())), preferred_element_type=F32)


def _lane_tile(x, width):
    return jnp.tile(x, (1, width // LANES))


def _post_kernel(x_ref, a_ref, wo_ref, g_ref, wg_ref, wu_ref, wd_ref, gf_ref, o_ref,
                 x1_sc, h_sc, acc_sc, *, final_norm):
    j = pl.program_id(1)

    @pl.when(j == 0)
    def _():
        x1 = x_ref[...] + _dot(a_ref[...], wo_ref[...])
        x1_sc[...] = x1
        h_sc[...] = _rms(x1, g_ref[...]).astype(BF16)

    h = h_sc[...]
    g = _dot(h, wg_ref[...])
    u = _dot(h, wu_ref[...])
    part = _dot(((g * jax.nn.sigmoid(g)) * u).astype(BF16), wd_ref[...])

    @pl.when(j == 0)
    def _():
        acc_sc[...] = part

    @pl.when(j > 0)
    def _():
        acc_sc[...] += part

    @pl.when(j == pl.num_programs(1) - 1)
    def _():
        y = x1_sc[...] + acc_sc[...]
        o_ref[...] = _rms(y, gf_ref[...]) if final_norm else y


def _post(x, a, wo, g, wg, wu, wd, gf, *, final_norm):
    m, d = x.shape
    dff = wg.shape[1]
    tf = dff // FFN_SPLIT
    assert tf * FFN_SPLIT == dff and tf % LANES == 0 and m % ROW_TILE == 0
    return pl.pallas_call(
        functools.partial(_post_kernel, final_norm=final_norm),
        grid=(m // ROW_TILE, FFN_SPLIT),
        in_specs=[
            pl.BlockSpec((ROW_TILE, d), lambda i, j: (i, 0)),
            pl.BlockSpec((ROW_TILE, d), lambda i, j: (i, 0)),
            pl.BlockSpec((d, d), lambda i, j: (0, 0)),
            pl.BlockSpec((1, d), lambda i, j: (0, 0)),
            pl.BlockSpec((d, tf), lambda i, j: (0, j)),
            pl.BlockSpec((d, tf), lambda i, j: (0, j)),
            pl.BlockSpec((tf, d), lambda i, j: (j, 0)),
            pl.BlockSpec((1, d), lambda i, j: (0, 0)),
        ],
        out_specs=pl.BlockSpec((ROW_TILE, d), lambda i, j: (i, 0)),
        out_shape=jax.ShapeDtypeStruct((m, d), F32),
        scratch_shapes=[pltpu.VMEM((ROW_TILE, d), F32), pltpu.VMEM((ROW_TILE, d), BF16),
                        pltpu.VMEM((ROW_TILE, d), F32)],
        compiler_params=_params("parallel", "arbitrary"),
        name="post_ffn",
    )(x, a, wo, g, wg, wu, wd, gf)


def _conv_front(x_ref, g_ref, win_ref):
    d = x_ref.shape[-1]
    h = _rms(x_ref[...], g_ref[...]).astype(BF16)
    p = _dot(h, win_ref[...])
    return p[:, :d], p[:, d:2 * d] * p[:, 2 * d:]


def _conv_taps(u, rig, p0, p1, wc_ref):
    s1 = jnp.where(rig == 0, p1, pltpu.roll(u, 1, axis=0))
    s2 = jnp.where(rig == 0, p0, jnp.where(rig == 1, p1, pltpu.roll(u, 2, axis=0)))
    return wc_ref[0:1, :] * s2 + wc_ref[1:2, :] * s1 + wc_ref[2:3, :] * u


def _conv_prompt_kernel(x_ref, g_ref, win_ref, wc_ref, prev_ref, z_ref, st_ref, carry_sc):
    @pl.when(pl.program_id(1) == 0)
    def _():
        carry_sc[...] = prev_ref[0]

    bg, u = _conv_front(x_ref, g_ref, win_ref)
    r = u.shape[0]
    rig = lax.broadcasted_iota(jnp.int32, u.shape, 0)
    conv = _conv_taps(u, rig, carry_sc[0:1, :], carry_sc[1:2, :], wc_ref)
    z_ref[...] = (bg * conv).astype(BF16)
    tail = u[r - (CONV_W - 1):, :]
    carry_sc[...] = tail
    st_ref[0] = tail


def _conv_sample_kernel(x_ref, g_ref, win_ref, wc_ref, prev_ref, z_ref, st_ref):
    bg, u = _conv_front(x_ref, g_ref, win_ref)
    r, d = u.shape
    n_seq = r // SUBLANES
    prev = prev_ref[...]
    p0 = jnp.broadcast_to(prev[:, 0:1, :], (n_seq, SUBLANES, d)).reshape(r, d)
    p1 = jnp.broadcast_to(prev[:, 1:2, :], (n_seq, SUBLANES, d)).reshape(r, d)
    rig = lax.broadcasted_iota(jnp.int32, u.shape, 0) & (SUBLANES - 1)
    conv = _conv_taps(u, rig, p0, p1, wc_ref)
    z_ref[...] = (bg * conv).astype(BF16)
    st_ref[...] = u.reshape(n_seq, SUBLANES, d)[:, SUBLANES - (CONV_W - 1):, :]


def _conv_mixer(x, g, win, wc, prev, seq_len):
    m, d = x.shape
    n_seq = m // seq_len
    common_in = [pl.BlockSpec((d, 3 * d), lambda *_: (0, 0)), pl.BlockSpec((CONV_W, d), lambda *_: (0, 0))]
    out_shape = (jax.ShapeDtypeStruct((m, d), BF16), jax.ShapeDtypeStruct((n_seq, CONV_W - 1, d), F32))
    if seq_len == SUBLANES:
        per = ROW_TILE // SUBLANES
        return pl.pallas_call(
            _conv_sample_kernel,
            grid=(m // ROW_TILE,),
            in_specs=[pl.BlockSpec((ROW_TILE, d), lambda i: (i, 0)), pl.BlockSpec((1, d), lambda i: (0, 0)),
                      *common_in, pl.BlockSpec((per, CONV_W - 1, d), lambda i: (i, 0, 0))],
            out_specs=(pl.BlockSpec((ROW_TILE, d), lambda i: (i, 0)),
                       pl.BlockSpec((per, CONV_W - 1, d), lambda i: (i, 0, 0))),
            out_shape=out_shape,
            compiler_params=_params("parallel"),
            name="conv_sample",
        )(x, g, win, wc, prev)
    assert seq_len % ROW_TILE == 0
    nl = seq_len // ROW_TILE
    return pl.pallas_call(
        _conv_prompt_kernel,
        grid=(n_seq, nl),
        in_specs=[pl.BlockSpec((ROW_TILE, d), lambda b, l: (b * nl + l, 0)),
                  pl.BlockSpec((1, d), lambda b, l: (0, 0)),
                  *common_in, pl.BlockSpec((1, CONV_W - 1, d), lambda b, l: (b, 0, 0))],
        out_specs=(pl.BlockSpec((ROW_TILE, d), lambda b, l: (b * nl + l, 0)),
                   pl.BlockSpec((1, CONV_W - 1, d), lambda b, l: (b, 0, 0))),
        out_shape=out_shape,
        scratch_shapes=[pltpu.VMEM((CONV_W - 1, d), F32)],
        compiler_params=_params("parallel", "arbitrary"),
        name="conv_prompt",
    )(x, g, win, wc, prev)


def _fox_proj_kernel(x_ref, g_ref, wqkv_ref, wf_ref, bf_ref, *refs, prompt):
    d = x_ref.shape[-1]
    h = _rms(x_ref[...], g_ref[...]).astype(BF16)
    p = _dot(h, wqkv_ref[...])
    q = p[:, :d] * (FOX_HEAD_DIM ** -0.5)
    k = p[:, d:2 * d]
    v = p[:, 2 * d:]
    logf = _log_sigmoid(_dot(h, wf_ref[...]) + bf_ref[...])
    if prompt:
        q_ref, k_ref, v_ref, kb_ref, vb_ref, lf_ref, c_ref, carry_sc = refs
        q_ref[...] = q.astype(BF16)
        kb_ref[...] = k.astype(BF16)
        vb_ref[...] = v.astype(BF16)

        @pl.when(pl.program_id(1) == 0)
        def _():
            carry_sc[...] = jnp.zeros_like(carry_sc)

        r = logf.shape[0]
        tril = (lax.broadcasted_iota(jnp.int32, (r, r), 1)
                <= lax.broadcasted_iota(jnp.int32, (r, r), 0)).astype(F32)
        c = carry_sc[...] + _dot_hi(tril, logf)
        carry_sc[...] = c[r - 1:r, :]
        c_ref[...] = c[:, :FOX_HEADS]
    else:
        q_ref, k_ref, v_ref, lf_ref = refs
        q_ref[...] = q
    k_ref[...] = k
    v_ref[...] = v
    lf_ref[...] = logf[:, :FOX_HEADS]


def _fox_proj(x, g, wqkv, wf, bf, seq_len, *, prompt):
    m, d = x.shape
    row = lambda b, l: (b * nl + l, 0)
    const = lambda b, l: (0, 0)
    if prompt:
        nl = seq_len // ROW_TILE
        grid = (m // seq_len, nl)
    else:
        nl = 1
        grid = (m // ROW_TILE, 1)
    big = pl.BlockSpec((ROW_TILE, d), row)
    small = pl.BlockSpec((ROW_TILE, FOX_HEADS), row)
    f32_out = jax.ShapeDtypeStruct((m, d), F32)
    bf_out = jax.ShapeDtypeStruct((m, d), BF16)
    h_out = jax.ShapeDtypeStruct((m, FOX_HEADS), F32)
    if prompt:
        out_specs = (big, big, big, big, big, small, small)
        out_shape = (bf_out, f32_out, f32_out, bf_out, bf_out, h_out, h_out)
        scratch = [pltpu.VMEM((1, LANES), F32)]
    else:
        out_specs = (big, big, big, small)
        out_shape = (f32_out, f32_out, f32_out, h_out)
        scratch = []
    return pl.pallas_call(
        functools.partial(_fox_proj_kernel, prompt=prompt),
        grid=grid,
        in_specs=[big, pl.BlockSpec((1, d), const), pl.BlockSpec((d, 3 * d), const),
                  pl.BlockSpec((d, LANES), const), pl.BlockSpec((1, LANES), const)],
        out_specs=out_specs,
        out_shape=out_shape,
        scratch_shapes=scratch,
        compiler_params=_params("parallel", "arbitrary"),
        name="fox_proj_prompt" if prompt else "fox_proj_sample",
    )(x, g, wqkv, wf, bf)


def _fox_attn_kernel(q_ref, k_ref, v_ref, c_ref, ct_ref, o_ref, cq_sc, m_sc, l_sc, acc_sc, *, tq):
    hp = pl.program_id(1)
    qi = pl.program_id(2)
    lane = lax.broadcasted_iota(jnp.int32, (tq, LANES), 1)
    half = (lane < FOX_HEAD_DIM, lane >= FOX_HEAD_DIM)
    q2 = q_ref[...]
    cq_all = c_ref[...]
    sel = lax.broadcasted_iota(jnp.int32, (FOX_HEADS, LANES), 0)
    for hh in range(2):
        cq_sc[hh] = _dot_hi(cq_all, (sel == 2 * hp + hh).astype(F32))
        m_sc[hh] = jnp.full((tq, LANES), NEG_INF, F32)
        l_sc[hh] = jnp.zeros((tq, LANES), F32)
        acc_sc[hh] = jnp.zeros((tq, LANES), F32)

    causal = (lax.broadcasted_iota(jnp.int32, (tq, tq), 1)
              <= lax.broadcasted_iota(jnp.int32, (tq, tq), 0))

    def block(ki, diagonal):
        start = pl.multiple_of(ki * tq, tq)
        k2 = k_ref[pl.ds(start, tq), :]
        v2 = v_ref[pl.ds(start, tq), :]
        for hh in range(2):
            qm = jnp.where(half[hh], q2, jnp.zeros_like(q2))
            ck = ct_ref[0, pl.ds(2 * hp + hh, 1), pl.ds(start, tq)]
            s = _dot_nt(qm, k2) + _lane_tile(cq_sc[hh], tq) - ck
            if diagonal:
                s = jnp.where(causal, s, NEG_INF)
            m_prev = m_sc[hh]
            m_new = jnp.maximum(m_prev, jnp.max(s, axis=1, keepdims=True))
            alpha = jnp.exp(m_prev - m_new)
            p = jnp.exp(s - _lane_tile(m_new, tq))
            l_sc[hh] = alpha * l_sc[hh] + jnp.sum(p, axis=1, keepdims=True)
            acc_sc[hh] = alpha * acc_sc[hh] + _dot(p.astype(BF16), v2)
            m_sc[hh] = m_new

    def body(ki, carry):
        block(ki, False)
        return carry

    lax.fori_loop(0, qi, body, 0)
    block(qi, True)
    o_ref[...] = jnp.where(half[0], acc_sc[0] / l_sc[0], acc_sc[1] / l_sc[1]).astype(BF16)


def _fox_attn_prompt(q, kb, vb, c, ct, seq_len):
    m, d = q.shape
    n_seq = m // seq_len
    tq = ATTN_TQ
    nq = seq_len // tq
    return pl.pallas_call(
        functools.partial(_fox_attn_kernel, tq=tq),
        grid=(n_seq, d // LANES, nq),
        in_specs=[pl.BlockSpec((tq, LANES), lambda b, hp, qi: (b * nq + qi, hp)),
                  pl.BlockSpec((seq_len, LANES), lambda b, hp, qi: (b, hp)),
                  pl.BlockSpec((seq_len, LANES), lambda b, hp, qi: (b, hp)),
                  pl.BlockSpec((tq, FOX_HEADS), lambda b, hp, qi: (b * nq + qi, 0)),
                  pl.BlockSpec((1, FOX_HEADS, seq_len), lambda b, hp, qi: (b, 0, 0))],
        out_specs=pl.BlockSpec((tq, LANES), lambda b, hp, qi: (b * nq + qi, hp)),
        out_shape=jax.ShapeDtypeStruct((m, d), BF16),
        scratch_shapes=[pltpu.VMEM((2, tq, LANES), F32)] * 4,
        compiler_params=_params("parallel", "parallel", "parallel"),
        name="fox_attn_prompt",
    )(q, kb, vb, c, ct)


def _split3(x):
    hi = x.astype(BF16)
    r1 = x - hi.astype(F32)
    mid = r1.astype(BF16)
    return hi, mid, (r1 - mid.astype(F32)).astype(BF16)


def _mask_dot(x, mask):
    mask = mask.astype(BF16)
    hi, mid, lo = _split3(x)
    return _dot(hi, mask) + _dot(mid, mask) + _dot(lo, mask)


def _dot_mask(mask, x, nt=False):
    mask = mask.astype(BF16)
    dot = _dot_nt if nt else _dot
    hi, mid, lo = _split3(x)
    return dot(mask, hi) + dot(mask, mid) + dot(mask, lo)


def _fox_decode_kernel(pt_ref, q_ref, kn_ref, vn_ref, lfn_ref, *refs, n_pages):
    del pt_ref
    kt_refs, vt_refs, lft_refs = refs[:n_pages], refs[n_pages:2 * n_pages], refs[2 * n_pages:3 * n_pages]
    o_ref = refs[3 * n_pages]
    page = kt_refs[0].shape[-1]
    n_new, d = q_ref.shape[1], q_ref.shape[2]
    rows = FOX_HEADS * n_new
    assert rows == page == LANES and n_new == SUBLANES
    ri = lax.broadcasted_iota(jnp.int32, (rows, page), 0)
    ki = lax.broadcasted_iota(jnp.int32, (rows, page), 1)
    tok_bits = n_new.bit_length() - 1
    dim_bits = FOX_HEAD_DIM.bit_length() - 1
    head_of_row = ((lax.broadcasted_iota(jnp.int32, (rows, FOX_HEADS), 0) >> tok_bits)
                   == lax.broadcasted_iota(jnp.int32, (rows, FOX_HEADS), 1))
    own_cols = ((lax.broadcasted_iota(jnp.int32, (rows, d), 0) >> tok_bits)
                == (lax.broadcasted_iota(jnp.int32, (rows, d), 1) >> dim_bits))
    qbd = jnp.where(own_cols, jnp.tile(q_ref[0], (FOX_HEADS, 1)), 0.0).astype(BF16)

    pad = page - n_new
    kn = jnp.concatenate([kn_ref[0], jnp.zeros((pad, d), F32)], axis=0).astype(BF16)
    vn = jnp.concatenate([vn_ref[0], jnp.zeros((pad, d), F32)], axis=0).astype(BF16)
    lfn = jnp.concatenate([lfn_ref[0], jnp.zeros((pad, FOX_HEADS), F32)], axis=0)
    lf_rows = _dot_mask(head_of_row, lfn, nt=True)
    cum = _mask_dot(lf_rows, ri <= ki)
    t_of_row = ri & (n_new - 1)
    a = jnp.sum(jnp.where(ki == t_of_row, cum, 0.0), axis=1, keepdims=True)
    s_new = jnp.where(ki <= t_of_row, _dot_nt(qbd, kn) + a - cum, NEG_INF)

    lft = jnp.concatenate([r[...] for r in lft_refs], axis=0)
    sfx_in = _mask_dot(lft, ri >= ki)
    sfx_ex = sfx_in - lft
    later = jnp.zeros((FOX_HEADS, 1), F32)
    bias = [None] * n_pages
    for i in reversed(range(n_pages)):
        bias[i] = sfx_ex[i * FOX_HEADS:(i + 1) * FOX_HEADS, :] + later
        later = later + sfx_in[i * FOX_HEADS:(i + 1) * FOX_HEADS, 0:1]
    bias = _dot_mask(head_of_row, jnp.concatenate(bias, axis=1))
    s_old = jnp.concatenate(
        [_dot(qbd, r[...].reshape(d, page).astype(BF16)) for r in kt_refs], axis=1) + bias + a

    m = jnp.maximum(jnp.max(s_new, axis=1, keepdims=True), jnp.max(s_old, axis=1, keepdims=True))
    p_new = jnp.exp(s_new - m)
    p_old = jnp.exp(s_old - m)
    denom = jnp.sum(p_new, axis=1, keepdims=True) + jnp.sum(p_old, axis=1, keepdims=True)
    p_old = p_old.astype(BF16)
    acc = _dot(p_new.astype(BF16), vn)
    for i, r in enumerate(vt_refs):
        acc = acc + _dot_nt(p_old[:, i * page:(i + 1) * page], r[...].reshape(d, page).astype(BF16))
    o = jnp.where(own_cols, acc / denom, 0.0).reshape(FOX_HEADS, n_new, d)
    o_ref[0] = jnp.sum(o, axis=0).astype(BF16)


def _fox_attn_sample(q, k_new, v_new, lf_new, cache_kt, cache_vt, cache_lft, page_table):
    nb, n_new, d = q.shape
    n_pages = page_table.shape[1]
    page = cache_kt.shape[-1]

    def new_map(b, pt):
        return (b, 0, 0)

    def kv_spec(i):
        return pl.BlockSpec((None, FOX_HEADS, FOX_HEAD_DIM, page), lambda b, pt: (pt[b * n_pages + i], 0, 0, 0))

    def lf_spec(i):
        return pl.BlockSpec((None, FOX_HEADS, page), lambda b, pt: (pt[b * n_pages + i], 0, 0))

    pages = range(n_pages)
    grid_spec = pltpu.PrefetchScalarGridSpec(
        num_scalar_prefetch=1,
        grid=(nb,),
        in_specs=[pl.BlockSpec((1, n_new, d), new_map), pl.BlockSpec((1, n_new, d), new_map),
                  pl.BlockSpec((1, n_new, d), new_map), pl.BlockSpec((1, n_new, FOX_HEADS), new_map),
                  *[kv_spec(i) for i in pages], *[kv_spec(i) for i in pages], *[lf_spec(i) for i in pages]],
        out_specs=pl.BlockSpec((1, n_new, d), new_map),
    )
    return pl.pallas_call(
        functools.partial(_fox_decode_kernel, n_pages=n_pages),
        grid_spec=grid_spec,
        out_shape=jax.ShapeDtypeStruct((nb, n_new, d), BF16),
        compiler_params=_params("parallel"),
        name="fox_attn_sample",
    )(page_table.reshape(-1), q, k_new, v_new, lf_new,
      *([cache_kt] * n_pages), *([cache_vt] * n_pages), *([cache_lft] * n_pages))


def _ml_proj_kernel(x_ref, g_ref, w_ref, wg_ref, bg_ref, q_ref, k_ref, v_ref, o_ref, gt_ref):
    hq = ML_HEADS * ML_DQK
    d = x_ref.shape[-1]
    h = _rms(x_ref[...], g_ref[...]).astype(BF16)
    p = _dot(h, w_ref[...])
    q_ref[...] = p[:, :hq].astype(BF16)
    k_ref[...] = (p[:, hq:2 * hq] * (ML_DQK ** -0.5)).astype(BF16)
    v_ref[...] = p[:, 2 * hq:2 * hq + d].astype(BF16)
    o_ref[...] = jax.nn.sigmoid(p[:, 2 * hq + d:])
    gates = _dot(h, wg_ref[...]) + bg_ref[...]
    gates = GATE_CAP * jnp.tanh(gates / GATE_CAP)
    lane = lax.broadcasted_iota(jnp.int32, gates.shape, 1)
    gates = jnp.where(lane < ML_HEADS, gates, _log_sigmoid(gates))
    gt_ref[...] = gates[:, :2 * ML_HEADS]


def _ml_proj(x, g, w, wg, bg):
    m, d = x.shape
    hq = ML_HEADS * ML_DQK
    row = lambda i: (i, 0)
    const = lambda i: (0, 0)
    return pl.pallas_call(
        _ml_proj_kernel,
        grid=(m // ROW_TILE,),
        in_specs=[pl.BlockSpec((ROW_TILE, d), row), pl.BlockSpec((1, d), const),
                  pl.BlockSpec((d, 2 * hq + 2 * d), const), pl.BlockSpec((d, LANES), const),
                  pl.BlockSpec((1, LANES), const)],
        out_specs=(pl.BlockSpec((ROW_TILE, hq), row), pl.BlockSpec((ROW_TILE, hq), row),
                   pl.BlockSpec((ROW_TILE, d), row), pl.BlockSpec((ROW_TILE, d), row),
                   pl.BlockSpec((ROW_TILE, 2 * ML_HEADS), row)),
        out_shape=(jax.ShapeDtypeStruct((m, hq), BF16), jax.ShapeDtypeStruct((m, hq), BF16),
                   jax.ShapeDtypeStruct((m, d), BF16), jax.ShapeDtypeStruct((m, d), F32),
                   jax.ShapeDtypeStruct((m, 2 * ML_HEADS), F32)),
        compiler_params=_params("parallel"),
        name="ml_proj",
    )(x, g, w, wg, bg)


def _ml_chunk_kernel(q_ref, k_ref, v_ref, og_ref, gt_ref, gtt_ref, gn_ref, c0_ref, n0_ref, m0_ref,
                     a_ref, c_ref, n_ref, m_ref):
    t = q_ref.shape[1]

    @pl.when(pl.program_id(1) == 0)
    def _():
        c_ref[...] = c0_ref[...]
        n_ref[...] = n0_ref[...]
        m_ref[...] = m0_ref[...]

    gates = gt_ref[0]
    gates_t = gtt_ref[0]
    ri = lax.broadcasted_iota(jnp.int32, (t, t), 0)
    ci = lax.broadcasted_iota(jnp.int32, (t, t), 1)
    causal = ci <= ri
    b_cols = _dot_hi(causal.astype(F32), gates)
    b_rows = _dot_hi(gates_t, (ri <= ci).astype(F32))
    lane = lax.broadcasted_iota(jnp.int32, (t, LANES), 1)
    half = (lane < ML_DQK, lane >= ML_DQK)
    lane1 = lax.broadcasted_iota(jnp.int32, (1, LANES), 1)
    row_sq = lax.broadcasted_iota(jnp.int32, (LANES, LANES), 0)
    head_lane = lax.broadcasted_iota(jnp.int32, (1, ML_HEADS), 1)
    m_all = m_ref[0]
    m_out = jnp.zeros((1, ML_HEADS), F32)

    for j in range(ML_HEADS // 2):
        q2 = q_ref[0, :, j * LANES:(j + 1) * LANES]
        k2 = k_ref[0, :, j * LANES:(j + 1) * LANES]
        c2 = c_ref[0, j]
        c2b = c2.astype(BF16)
        n2 = n_ref[0, j:j + 1, :]
        upd = jnp.zeros((LANES, LANES), F32)
        n_add = jnp.zeros((1, LANES), F32)
        decays = []
        for hh in range(2):
            h = 2 * j + hh
            sl = slice(h * ML_DV, (h + 1) * ML_DV)
            qm = jnp.where(half[hh], q2, jnp.zeros_like(q2))
            km = jnp.where(half[hh], k2, jnp.zeros_like(k2))
            v_h = v_ref[0, :, sl]
            i_c = gates[:, h:h + 1]
            i_r = gates_t[h:h + 1, :]
            b_c = b_cols[:, ML_HEADS + h:ML_HEADS + h + 1]
            b_r = b_rows[ML_HEADS + h:ML_HEADS + h + 1, :]
            m_prev = m_all[:, h:h + 1]
            logw = jnp.where(causal, b_c - b_r + i_r, NEG_INF)
            inter = b_c + m_prev
            m_t = jnp.maximum(inter, jnp.max(logw, axis=1, keepdims=True))
            w_inter = jnp.exp(inter - m_t)
            a = jnp.exp(logw - m_t) * _dot_nt(qm, k2)
            num = w_inter * _dot(qm, c2b) + _dot(a.astype(BF16), v_h)
            qn = jnp.sum(qm.astype(F32) * n2, axis=1, keepdims=True)
            den = w_inter * qn + jnp.sum(a, axis=1, keepdims=True)
            hs = num / jnp.maximum(jnp.abs(den), jnp.exp(-m_t))
            hn = hs * lax.rsqrt(jnp.mean(hs * hs, axis=1, keepdims=True) + EPS) * gn_ref[:, sl]
            a_ref[0, :, sl] = (og_ref[0, :, sl] * hn).astype(BF16)
            b_last = b_c[t - 1:t, :]
            m_new = m_t[t - 1:t, :]
            decays.append(jnp.exp(b_last + m_prev - m_new))
            kw = jnp.exp(b_last - b_c + i_c - m_new) * km.astype(F32)
            upd = upd + _dot_tn(kw.astype(BF16), v_h)
            n_add = n_add + jnp.sum(kw, axis=0, keepdims=True)
            m_out = jnp.where(head_lane == h, m_new, m_out)
        c_ref[0, j] = jnp.where(row_sq < ML_DQK, decays[0], decays[1]) * c2 + upd
        n_ref[0, j:j + 1, :] = jnp.where(lane1 < ML_DQK, decays[0], decays[1]) * n2 + n_add
    m_ref[0] = m_out


def _ml_chunk(q, k, v, og, gates, gates_t, gn, c0, n0, m0, seq_len, chunk):
    nb, _, d = v.shape
    hq = q.shape[2]
    nc = seq_len // chunk
    pairs = ML_HEADS // 2
    tok = lambda width: pl.BlockSpec((1, chunk, width), lambda b, c: (b, c, 0))
    state4 = pl.BlockSpec((1, pairs, LANES, LANES), lambda b, c: (b, 0, 0, 0))
    state3 = pl.BlockSpec((1, pairs, LANES), lambda b, c: (b, 0, 0))
    state_m = pl.BlockSpec((1, 1, ML_HEADS), lambda b, c: (b, 0, 0))
    return pl.pallas_call(
        _ml_chunk_kernel,
        grid=(nb, nc),
        in_specs=[tok(hq), tok(hq), tok(d), tok(d), tok(2 * ML_HEADS),
                  pl.BlockSpec((1, 2 * ML_HEADS, chunk), lambda b, c: (b * nc + c, 0, 0)),
                  pl.BlockSpec((1, d), lambda b, c: (0, 0)), state4, state3, state_m],
        out_specs=(tok(d), state4, state3, state_m),
        out_shape=(jax.ShapeDtypeStruct((nb, seq_len, d), BF16),
                   jax.ShapeDtypeStruct((nb, pairs, LANES, LANES), F32),
                   jax.ShapeDtypeStruct((nb, pairs, LANES), F32),
                   jax.ShapeDtypeStruct((nb, 1, ML_HEADS), F32)),
        compiler_params=_params("parallel", "arbitrary"),
        name="ml_chunk",
    )(q, k, v, og, gates, gates_t, gn, c0, n0, m0)


def _pad_cols(w, width):
    return jnp.pad(w, ((0, 0), (0, width - w.shape[1])))


def kernel(x_prompt, x_sample, state_conv, cache_k, cache_v, cache_logf, page_table, state_C, state_n, state_m, norm_mix, norm_ffn, norm_final, w_conv_in, w_conv, w_conv_out, w_fox_in, b_fox_f, w_fox_out, w_ml_in, b_ml_gates, g_ml_norm, w_ml_out, w_ffn_gu, w_ffn_down):
    bp, lp, d = x_prompt.shape
    bs, ls, _ = x_sample.shape
    depth = norm_mix.shape[0]
    dff = w_ffn_down.shape[1]
    assert d == D_MODEL and ls == SUBLANES
    xp = x_prompt.reshape(bp * lp, d)
    xs = x_sample.reshape(bs * ls, d)
    gf = norm_final.reshape(1, d)
    hq = ML_HEADS * ML_DQK
    pairs = ML_HEADS // 2
    out = {name: [] for name in ("conv_p", "conv_s", "kp", "vp", "lfp", "ks", "vs", "lfs",
                                 "cp", "np", "mp", "cs", "ns", "ms")}

    for i in range(depth):
        kind, j = i % 3, i // 3
        gm = norm_mix[i].reshape(1, d)
        if kind == 0:
            win = w_conv_in[j].astype(BF16)
            wo = w_conv_out[j].astype(BF16)
            ap, sp = _conv_mixer(xp, gm, win, w_conv[j], jnp.zeros((bp, CONV_W - 1, d), F32), lp)
            as_, ss = _conv_mixer(xs, gm, win, w_conv[j], state_conv[j], ls)
            out["conv_p"].append(sp)
            out["conv_s"].append(ss)
        elif kind == 1:
            wqkv = w_fox_in[j][:, :3 * d].astype(BF16)
            wf = _pad_cols(w_fox_in[j][:, 3 * d:], LANES).astype(BF16)
            bf = _pad_cols(b_fox_f[j].reshape(1, FOX_HEADS), LANES)
            wo = w_fox_out[j].astype(BF16)
            q, k, v, kb, vb, lf, c = _fox_proj(xp, gm, wqkv, wf, bf, lp, prompt=True)
            ct = jnp.swapaxes(c.reshape(bp, lp, FOX_HEADS), 1, 2)
            ap = _fox_attn_prompt(q, kb, vb, c, ct, lp)
            out["kp"].append(k.reshape(bp, lp, FOX_HEADS, FOX_HEAD_DIM))
            out["vp"].append(v.reshape(bp, lp, FOX_HEADS, FOX_HEAD_DIM))
            out["lfp"].append(lf.reshape(bp, lp, FOX_HEADS))
            q, k, v, lf = _fox_proj(xs, gm, wqkv, wf, bf, ls, prompt=False)
            as_ = _fox_attn_sample(
                q.reshape(bs, ls, d), k.reshape(bs, ls, d), v.reshape(bs, ls, d),
                lf.reshape(bs, ls, FOX_HEADS), jnp.transpose(cache_k[j], (0, 2, 3, 1)),
                jnp.transpose(cache_v[j], (0, 2, 3, 1)), jnp.transpose(cache_logf[j], (0, 2, 1)),
                page_table).reshape(bs * ls, d)
            out["ks"].append(k.reshape(bs, ls, FOX_HEADS, FOX_HEAD_DIM))
            out["vs"].append(v.reshape(bs, ls, FOX_HEADS, FOX_HEAD_DIM))
            out["lfs"].append(lf.reshape(bs, ls, FOX_HEADS))
        else:
            w = w_ml_in[j][:, :2 * hq + 2 * d].astype(BF16)
            wg = _pad_cols(w_ml_in[j][:, 2 * hq + 2 * d:], LANES).astype(BF16)
            bg = _pad_cols(b_ml_gates[j].reshape(1, 2 * ML_HEADS), LANES)
            gn = g_ml_norm[j].reshape(1, d)
            wo = w_ml_out[j].astype(BF16)

            def run(x, nb, seq_len, chunk, c0, n0, m0):
                q, k, v, og, gates = _ml_proj(x, gm, w, wg, bg)
                nc = seq_len // chunk
                gates_t = jnp.swapaxes(gates.reshape(nb * nc, chunk, 2 * ML_HEADS), 1, 2)
                tok = lambda t: t.reshape(nb, seq_len, t.shape[-1])
                a, c1, n1, m1 = _ml_chunk(tok(q), tok(k), tok(v), tok(og), tok(gates), gates_t, gn,
                                          c0.reshape(nb, pairs, LANES, LANES),
                                          n0.reshape(nb, pairs, LANES),
                                          m0.reshape(nb, 1, ML_HEADS), seq_len, chunk)
                return (a.reshape(nb * seq_len, d), c1.reshape(nb, ML_HEADS, ML_DQK, ML_DV), n1.reshape(nb, ML_HEADS, ML_DQK),
                        m1.reshape(nb, ML_HEADS))

            ap, c1, n1, m1 = run(xp, bp, lp, ML_CHUNK_PROMPT,
                                 jnp.zeros((bp, ML_HEADS, ML_DQK, ML_DV), F32),
                                 jnp.zeros((bp, ML_HEADS, ML_DQK), F32), jnp.zeros((bp, ML_HEADS), F32))
            out["cp"].append(c1); out["np"].append(n1); out["mp"].append(m1)
            as_, c1, n1, m1 = run(xs, bs, ls, ls, state_C[j], state_n[j], state_m[j])
            out["cs"].append(c1); out["ns"].append(n1); out["ms"].append(m1)

        gffn = norm_ffn[i].reshape(1, d)
        wg_ffn = w_ffn_gu[i][:, :dff].astype(BF16)
        wu_ffn = w_ffn_gu[i][:, dff:].astype(BF16)
        wd_ffn = w_ffn_down[i].astype(BF16)
        last = i == depth - 1
        xp = _post(xp, ap, wo, gffn, wg_ffn, wu_ffn, wd_ffn, gf, final_norm=last)
        xs = _post(xs, as_, wo, gffn, wg_ffn, wu_ffn, wd_ffn, gf, final_norm=last)

    st = lambda name: jnp.stack(out[name])
    return (xp.reshape(bp, lp, d), xs.reshape(bs, ls, d), st("conv_p"), st("conv_s"),
            st("kp"), st("vp"), st("lfp"), st("ks"), st("vs"), st("lfs"),
            st("cp"), st("np"), st("mp"), st("cs"), st("ns"), st("ms"))
```

```python
import functools

import jax
import jax.numpy as jnp
from jax import lax
from jax.experimental import pallas as pl
from jax.experimental.pallas import tpu as pltpu

F32 = jnp.float32
BF16 = jnp.bfloat16
HIGHEST = lax.Precision.HIGHEST

D_MODEL = 1024
CONV_W = 3
FOX_HEADS = 16
FOX_HEAD_DIM = D_MODEL // FOX_HEADS
ML_HEADS = 8
ML_DV = D_MODEL // ML_HEADS
ML_DQK = ML_DV // 2
GATE_CAP = 15.0
EPS = 1e-6
NEG_INF = -1e30

LANES = 128
SUBLANES = 8
ROW_TILE = 512
FFN_SPLIT = 2
ATTN_TQ = 512
ML_CHUNK_PROMPT = 128
ML_STEP_SEQS = 16
VMEM_LIMIT = 56 * 1024 * 1024


def _params(*sem):
    return pltpu.CompilerParams(dimension_semantics=sem, vmem_limit_bytes=VMEM_LIMIT)


def _rms(x, g):
    return x * lax.rsqrt(jnp.mean(x * x, axis=-1, keepdims=True) + EPS) * g


def _log_sigmoid(x):
    return jnp.minimum(x, 0.0) - jnp.log1p(jnp.exp(-jnp.abs(x)))


def _dot(a, b):
    return jnp.dot(a, b, preferred_element_type=F32)


def _dot_hi(a, b):
    return jnp.dot(a, b, precision=HIGHEST, preferred_element_type=F32)


def _dot_nt(a, b, precision=None):
    return lax.dot_general(a, b, (((1,), (1,)), ((), ())), precision=precision,
                           preferred_element_type=F32)


def _dot_tn(a, b):
    return lax.dot_general(a, b, (((0,), (0,)), ((), ())), preferred_element_type=F32)


def _lane_tile(x, width):
    return jnp.tile(x, (1, width // LANES))


def _post_kernel(x_ref, a_ref, wo_ref, g_ref, wg_ref, wu_ref, wd_ref, gf_ref, o_ref,
                 x1_sc, h_sc, acc_sc, *, final_norm):
    j = pl.program_id(1)

    @pl.when(j == 0)
    def _():
        x1 = x_ref[...] + _dot(a_ref[...], wo_ref[...])
        x1_sc[...] = x1
        h_sc[...] = _rms(x1, g_ref[...]).astype(BF16)

    h = h_sc[...]
    g = _dot(h, wg_ref[...])
    u = _dot(h, wu_ref[...])
    part = _dot(((g * jax.nn.sigmoid(g)) * u).astype(BF16), wd_ref[...])

    @pl.when(j == 0)
    def _():
        acc_sc[...] = part

    @pl.when(j > 0)
    def _():
        acc_sc[...] += part

    @pl.when(j == pl.num_programs(1) - 1)
    def _():
        y = x1_sc[...] + acc_sc[...]
        o_ref[...] = _rms(y, gf_ref[...]) if final_norm else y


def _post(x, a, wo, g, wg, wu, wd, gf, *, final_norm):
    m, d = x.shape
    dff = wg.shape[1]
    tf = dff // FFN_SPLIT
    assert tf * FFN_SPLIT == dff and tf % LANES == 0 and m % ROW_TILE == 0
    return pl.pallas_call(
        functools.partial(_post_kernel, final_norm=final_norm),
        grid=(m // ROW_TILE, FFN_SPLIT),
        in_specs=[
            pl.BlockSpec((ROW_TILE, d), lambda i, j: (i, 0)),
            pl.BlockSpec((ROW_TILE, d), lambda i, j: (i, 0)),
            pl.BlockSpec((d, d), lambda i, j: (0, 0)),
            pl.BlockSpec((1, d), lambda i, j: (0, 0)),
            pl.BlockSpec((d, tf), lambda i, j: (0, j)),
            pl.BlockSpec((d, tf), lambda i, j: (0, j)),
            pl.BlockSpec((tf, d), lambda i, j: (j, 0)),
            pl.BlockSpec((1, d), lambda i, j: (0, 0)),
        ],
        out_specs=pl.BlockSpec((ROW_TILE, d), lambda i, j: (i, 0)),
        out_shape=jax.ShapeDtypeStruct((m, d), F32),
        scratch_shapes=[pltpu.VMEM((ROW_TILE, d), F32), pltpu.VMEM((ROW_TILE, d), BF16),
                        pltpu.VMEM((ROW_TILE, d), F32)],
        compiler_params=_params("parallel", "arbitrary"),
        name="post_ffn",
    )(x, a, wo, g, wg, wu, wd, gf)


def _conv_front(x_ref, g_ref, win_ref):
    d = x_ref.shape[-1]
    h = _rms(x_ref[...], g_ref[...]).astype(BF16)
    p = _dot(h, win_ref[...])
    return p[:, :d], p[:, d:2 * d] * p[:, 2 * d:]


def _conv_taps(u, rig, p0, p1, wc_ref):
    s1 = jnp.where(rig == 0, p1, pltpu.roll(u, 1, axis=0))
    s2 = jnp.where(rig == 0, p0, jnp.where(rig == 1, p1, pltpu.roll(u, 2, axis=0)))
    return wc_ref[0:1, :] * s2 + wc_ref[1:2, :] * s1 + wc_ref[2:3, :] * u


def _conv_prompt_kernel(x_ref, g_ref, win_ref, wc_ref, prev_ref, z_ref, st_ref, carry_sc):
    @pl.when(pl.program_id(1) == 0)
    def _():
        carry_sc[...] = prev_ref[0]

    bg, u = _conv_front(x_ref, g_ref, win_ref)
    r = u.shape[0]
    rig = lax.broadcasted_iota(jnp.int32, u.shape, 0)
    conv = _conv_taps(u, rig, carry_sc[0:1, :], carry_sc[1:2, :], wc_ref)
    z_ref[...] = (bg * conv).astype(BF16)
    tail = u[r - (CONV_W - 1):, :]
    carry_sc[...] = tail
    st_ref[0] = tail


def _conv_sample_kernel(x_ref, g_ref, win_ref, wc_ref, prev_ref, z_ref, st_ref):
    bg, u = _conv_front(x_ref, g_ref, win_ref)
    r, d = u.shape
    n_seq = r // SUBLANES
    prev = prev_ref[...]
    p0 = jnp.broadcast_to(prev[:, 0:1, :], (n_seq, SUBLANES, d)).reshape(r, d)
    p1 = jnp.broadcast_to(prev[:, 1:2, :], (n_seq, SUBLANES, d)).reshape(r, d)
    rig = lax.broadcasted_iota(jnp.int32, u.shape, 0) & (SUBLANES - 1)
    conv = _conv_taps(u, rig, p0, p1, wc_ref)
    z_ref[...] = (bg * conv).astype(BF16)
    st_ref[...] = u.reshape(n_seq, SUBLANES, d)[:, SUBLANES - (CONV_W - 1):, :]


def _conv_mixer(x, g, win, wc, prev, seq_len):
    m, d = x.shape
    n_seq = m // seq_len
    common_in = [pl.BlockSpec((d, 3 * d), lambda *_: (0, 0)), pl.BlockSpec((CONV_W, d), lambda *_: (0, 0))]
    out_shape = (jax.ShapeDtypeStruct((m, d), BF16), jax.ShapeDtypeStruct((n_seq, CONV_W - 1, d), F32))
    if seq_len == SUBLANES:
        per = ROW_TILE // SUBLANES
        return pl.pallas_call(
            _conv_sample_kernel,
            grid=(m // ROW_TILE,),
            in_specs=[pl.BlockSpec((ROW_TILE, d), lambda i: (i, 0)), pl.BlockSpec((1, d), lambda i: (0, 0)),
                      *common_in, pl.BlockSpec((per, CONV_W - 1, d), lambda i: (i, 0, 0))],
            out_specs=(pl.BlockSpec((ROW_TILE, d), lambda i: (i, 0)),
                       pl.BlockSpec((per, CONV_W - 1, d), lambda i: (i, 0, 0))),
            out_shape=out_shape,
            compiler_params=_params("parallel"),
            name="conv_sample",
        )(x, g, win, wc, prev)
    assert seq_len % ROW_TILE == 0
    nl = seq_len // ROW_TILE
    return pl.pallas_call(
        _conv_prompt_kernel,
        grid=(n_seq, nl),
        in_specs=[pl.BlockSpec((ROW_TILE, d), lambda b, l: (b * nl + l, 0)),
                  pl.BlockSpec((1, d), lambda b, l: (0, 0)),
                  *common_in, pl.BlockSpec((1, CONV_W - 1, d), lambda b, l: (b, 0, 0))],
        out_specs=(pl.BlockSpec((ROW_TILE, d), lambda b, l: (b * nl + l, 0)),
                   pl.BlockSpec((1, CONV_W - 1, d), lambda b, l: (b, 0, 0))),
        out_shape=out_shape,
        scratch_shapes=[pltpu.VMEM((CONV_W - 1, d), F32)],
        compiler_params=_params("parallel", "arbitrary"),
        name="conv_prompt",
    )(x, g, win, wc, prev)


def _fox_proj_kernel(x_ref, g_ref, wqkv_ref, wf_ref, bf_ref, q_ref, k_ref, v_ref, lf_ref):
    d = x_ref.shape[-1]
    h = _rms(x_ref[...], g_ref[...]).astype(BF16)
    p = _dot(h, wqkv_ref[...])
    logf = _log_sigmoid(_dot(h, wf_ref[...]) + bf_ref[...])
    q_ref[...] = p[:, :d] * (FOX_HEAD_DIM ** -0.5)
    k_ref[...] = p[:, d:2 * d]
    v_ref[...] = p[:, 2 * d:]
    lf_ref[...] = logf[:, :FOX_HEADS]


def _fox_proj_sample(x, g, wqkv, wf, bf):
    m, d = x.shape
    row = lambda i: (i, 0)
    const = lambda i: (0, 0)
    big = pl.BlockSpec((ROW_TILE, d), row)
    f32_out = jax.ShapeDtypeStruct((m, d), F32)
    return pl.pallas_call(
        _fox_proj_kernel,
        grid=(m // ROW_TILE,),
        in_specs=[big, pl.BlockSpec((1, d), const), pl.BlockSpec((d, 3 * d), const),
                  pl.BlockSpec((d, LANES), const), pl.BlockSpec((1, LANES), const)],
        out_specs=(big, big, big, pl.BlockSpec((ROW_TILE, FOX_HEADS), row)),
        out_shape=(f32_out, f32_out, f32_out, jax.ShapeDtypeStruct((m, FOX_HEADS), F32)),
        compiler_params=_params("parallel"),
        name="fox_proj_sample",
    )(x, g, wqkv, wf, bf)


def _fox_proj_prompt_kernel(x_ref, g_ref, wq_ref, wkvt_ref, wf_ref, wft_ref, bf_ref, bft_ref,
                            q_ref, kt_ref, vt_ref, ktb_ref, vtb_ref, lft_ref, c_ref, ct_ref,
                            carry_sc, carryt_sc):
    d = x_ref.shape[-1]
    r = x_ref.shape[0]
    h = _rms(x_ref[...], g_ref[...]).astype(BF16)
    q_ref[...] = (_dot(h, wq_ref[...]) * (FOX_HEAD_DIM ** -0.5)).astype(BF16)
    kv = _dot_nt(wkvt_ref[...], h)
    kt_ref[0] = kv[:d]
    vt_ref[0] = kv[d:]
    ktb_ref[0] = kv[:d].astype(BF16)
    vtb_ref[0] = kv[d:].astype(BF16)
    logf = _log_sigmoid(_dot(h, wf_ref[...]) + bf_ref[...])
    logf_t = _log_sigmoid(_dot_nt(wft_ref[...], h) + bft_ref[...])
    lft_ref[0] = logf_t

    @pl.when(pl.program_id(1) == 0)
    def _():
        carry_sc[...] = jnp.zeros_like(carry_sc)
        carryt_sc[...] = jnp.zeros_like(carryt_sc)

    ri = lax.broadcasted_iota(jnp.int32, (r, r), 0)
    ci = lax.broadcasted_iota(jnp.int32, (r, r), 1)
    c = carry_sc[...] + _dot_mask(ci <= ri, logf)
    carry_sc[...] = c[r - 1:r, :]
    c_ref[...] = c[:, :FOX_HEADS]
    ct = _lane_tile(carryt_sc[...], r) + _mask_dot(logf_t, ri <= ci)
    carryt_sc[...] = jnp.broadcast_to(ct[:, r - 1:r], carryt_sc.shape)
    ct_ref[0] = ct


def _fox_proj_prompt(x, g, wq, wkvt, wf, wft, bf, bft, seq_len):
    m, d = x.shape
    n_seq = m // seq_len
    nl = seq_len // ROW_TILE
    row = lambda b, l: (b * nl + l, 0)
    const = lambda b, l: (0, 0)
    feat = lambda width: pl.BlockSpec((1, width, ROW_TILE), lambda b, l: (b, 0, l))
    feat_shape = lambda width, dtype: jax.ShapeDtypeStruct((n_seq, width, seq_len), dtype)
    return pl.pallas_call(
        _fox_proj_prompt_kernel,
        grid=(n_seq, nl),
        in_specs=[pl.BlockSpec((ROW_TILE, d), row), pl.BlockSpec((1, d), const),
                  pl.BlockSpec((d, d), const), pl.BlockSpec((2 * d, d), const),
                  pl.BlockSpec((d, LANES), const), pl.BlockSpec((FOX_HEADS, d), const),
                  pl.BlockSpec((1, LANES), const), pl.BlockSpec((FOX_HEADS, 1), const)],
        out_specs=(pl.BlockSpec((ROW_TILE, d), row), feat(d), feat(d), feat(d), feat(d), feat(FOX_HEADS),
                   pl.BlockSpec((ROW_TILE, FOX_HEADS), row), feat(FOX_HEADS)),
        out_shape=(jax.ShapeDtypeStruct((m, d), BF16), feat_shape(d, F32), feat_shape(d, F32),
                   feat_shape(d, BF16), feat_shape(d, BF16), feat_shape(FOX_HEADS, F32),
                   jax.ShapeDtypeStruct((m, FOX_HEADS), F32), feat_shape(FOX_HEADS, F32)),
        scratch_shapes=[pltpu.VMEM((1, LANES), F32), pltpu.VMEM((FOX_HEADS, LANES), F32)],
        compiler_params=_params("parallel", "arbitrary"),
        name="fox_proj_prompt",
    )(x, g, wq, wkvt, wf, wft, bf, bft)


def _fox_attn_kernel(q_ref, kt_ref, vt_ref, c_ref, ct_ref, o_ref, cq_sc, m_sc, l_sc, acc_sc, *, tq):
    hp = pl.program_id(1)
    qi = pl.program_id(2)
    lane = lax.broadcasted_iota(jnp.int32, (tq, LANES), 1)
    half = (lane < FOX_HEAD_DIM, lane >= FOX_HEAD_DIM)
    q2 = q_ref[...]
    cq_all = c_ref[...]
    sel = lax.broadcasted_iota(jnp.int32, (FOX_HEADS, LANES), 0)
    for hh in range(2):
        cq_sc[hh] = _dot_hi(cq_all, (sel == 2 * hp + hh).astype(F32))
        m_sc[hh] = jnp.full((tq, LANES), NEG_INF, F32)
        l_sc[hh] = jnp.zeros((tq, LANES), F32)
        acc_sc[hh] = jnp.zeros((tq, LANES), F32)

    causal = (lax.broadcasted_iota(jnp.int32, (tq, tq), 1)
              <= lax.broadcasted_iota(jnp.int32, (tq, tq), 0))

    def block(ki, diagonal):
        start = pl.multiple_of(ki * tq, tq)
        k2t = kt_ref[0, :, pl.ds(start, tq)]
        v2t = vt_ref[0, :, pl.ds(start, tq)]
        s2 = [_dot(jnp.where(half[hh], q2, jnp.zeros_like(q2)), k2t) for hh in range(2)]
        p2, alpha2 = [], []
        for hh in range(2):
            ck = ct_ref[0, pl.ds(2 * hp + hh, 1), pl.ds(start, tq)]
            s = s2[hh] + _lane_tile(cq_sc[hh], tq) - ck
            if diagonal:
                s = jnp.where(causal, s, NEG_INF)
            m_prev = m_sc[hh]
            m_new = jnp.maximum(m_prev, jnp.max(s, axis=1, keepdims=True))
            alpha = jnp.exp(m_prev - m_new)
            p = jnp.exp(s - _lane_tile(m_new, tq))
            l_sc[hh] = alpha * l_sc[hh] + jnp.sum(p, axis=1, keepdims=True)
            m_sc[hh] = m_new
            p2.append(p.astype(BF16))
            alpha2.append(alpha)
        pv2 = [_dot_nt(p2[hh], v2t) for hh in range(2)]
        for hh in range(2):
            acc_sc[hh] = alpha2[hh] * acc_sc[hh] + pv2[hh]

    def body(ki, carry):
        block(ki, False)
        return carry

    lax.fori_loop(0, qi, body, 0)
    block(qi, True)
    o_ref[...] = jnp.where(half[0], acc_sc[0] / l_sc[0], acc_sc[1] / l_sc[1]).astype(BF16)


def _fox_attn_prompt(q, ktb, vtb, c, ct, seq_len):
    m, d = q.shape
    n_seq = m // seq_len
    tq = ATTN_TQ
    nq = seq_len // tq
    return pl.pallas_call(
        functools.partial(_fox_attn_kernel, tq=tq),
        grid=(n_seq, d // LANES, nq),
        in_specs=[pl.BlockSpec((tq, LANES), lambda b, hp, qi: (b * nq + qi, hp)),
                  pl.BlockSpec((1, LANES, seq_len), lambda b, hp, qi: (b, hp, 0)),
                  pl.BlockSpec((1, LANES, seq_len), lambda b, hp, qi: (b, hp, 0)),
                  pl.BlockSpec((tq, FOX_HEADS), lambda b, hp, qi: (b * nq + qi, 0)),
                  pl.BlockSpec((1, FOX_HEADS, seq_len), lambda b, hp, qi: (b, 0, 0))],
        out_specs=pl.BlockSpec((tq, LANES), lambda b, hp, qi: (b * nq + qi, hp)),
        out_shape=jax.ShapeDtypeStruct((m, d), BF16),
        scratch_shapes=[pltpu.VMEM((2, tq, LANES), F32)] * 4,
        compiler_params=_params("parallel", "parallel", "parallel"),
        name="fox_attn_prompt",
    )(q, ktb, vtb, c, ct)


def _split3(x):
    hi = x.astype(BF16)
    r1 = x - hi.astype(F32)
    mid = r1.astype(BF16)
    return hi, mid, (r1 - mid.astype(F32)).astype(BF16)


def _mask_dot(x, mask):
    mask = mask.astype(BF16)
    hi, mid, lo = _split3(x)
    return _dot(hi, mask) + _dot(mid, mask) + _dot(lo, mask)


def _dot_mask(mask, x, nt=False):
    mask = mask.astype(BF16)
    dot = _dot_nt if nt else _dot
    hi, mid, lo = _split3(x)
    return dot(mask, hi) + dot(mask, mid) + dot(mask, lo)


def _fox_decode_kernel(pt_ref, q_ref, kn_ref, vn_ref, lfn_ref, *refs, n_pages):
    del pt_ref
    kt_refs, vt_refs, lft_refs = refs[:n_pages], refs[n_pages:2 * n_pages], refs[2 * n_pages:3 * n_pages]
    o_ref = refs[3 * n_pages]
    page = kt_refs[0].shape[-1]
    n_new, d = q_ref.shape[1], q_ref.shape[2]
    rows = FOX_HEADS * n_new
    assert rows == page == LANES and n_new == SUBLANES
    ri = lax.broadcasted_iota(jnp.int32, (rows, page), 0)
    ki = lax.broadcasted_iota(jnp.int32, (rows, page), 1)
    tok_bits = n_new.bit_length() - 1
    dim_bits = FOX_HEAD_DIM.bit_length() - 1
    head_of_row = ((lax.broadcasted_iota(jnp.int32, (rows, FOX_HEADS), 0) >> tok_bits)
                   == lax.broadcasted_iota(jnp.int32, (rows, FOX_HEADS), 1))
    own_cols = ((lax.broadcasted_iota(jnp.int32, (rows, d), 0) >> tok_bits)
                == (lax.broadcasted_iota(jnp.int32, (rows, d), 1) >> dim_bits))
    qbd = jnp.where(own_cols, jnp.tile(q_ref[0], (FOX_HEADS, 1)), 0.0).astype(BF16)

    pad = page - n_new
    kn = jnp.concatenate([kn_ref[0], jnp.zeros((pad, d), F32)], axis=0).astype(BF16)
    vn = jnp.concatenate([vn_ref[0], jnp.zeros((pad, d), F32)], axis=0).astype(BF16)
    lfn = jnp.concatenate([lfn_ref[0], jnp.zeros((pad, FOX_HEADS), F32)], axis=0)
    lf_rows = _dot_mask(head_of_row, lfn, nt=True)
    cum = _mask_dot(lf_rows, ri <= ki)
    t_of_row = ri & (n_new - 1)
    a = jnp.sum(jnp.where(ki == t_of_row, cum, 0.0), axis=1, keepdims=True)
    s_new = jnp.where(ki <= t_of_row, _dot_nt(qbd, kn) + a - cum, NEG_INF)

    lft = jnp.concatenate([r[...] for r in lft_refs], axis=0)
    sfx_in = _mask_dot(lft, ri >= ki)
    sfx_ex = sfx_in - lft
    later = jnp.zeros((FOX_HEADS, 1), F32)
    bias = [None] * n_pages
    for i in reversed(range(n_pages)):
        bias[i] = sfx_ex[i * FOX_HEADS:(i + 1) * FOX_HEADS, :] + later
        later = later + sfx_in[i * FOX_HEADS:(i + 1) * FOX_HEADS, 0:1]
    bias = _dot_mask(head_of_row, jnp.concatenate(bias, axis=1))
    s_old = jnp.concatenate(
        [_dot(qbd, r[...].reshape(d, page).astype(BF16)) for r in kt_refs], axis=1) + bias + a

    m = jnp.maximum(jnp.max(s_new, axis=1, keepdims=True), jnp.max(s_old, axis=1, keepdims=True))
    p_new = jnp.exp(s_new - m)
    p_old = jnp.exp(s_old - m)
    denom = jnp.sum(p_new, axis=1, keepdims=True) + jnp.sum(p_old, axis=1, keepdims=True)
    p_old = p_old.astype(BF16)
    acc = _dot(p_new.astype(BF16), vn)
    for i, r in enumerate(vt_refs):
        acc = acc + _dot_nt(p_old[:, i * page:(i + 1) * page], r[...].reshape(d, page).astype(BF16))
    o = jnp.where(own_cols, acc / denom, 0.0).reshape(FOX_HEADS, n_new, d)
    o_ref[0] = jnp.sum(o, axis=0).astype(BF16)


def _fox_attn_sample(q, k_new, v_new, lf_new, cache_kt, cache_vt, cache_lft, page_table):
    nb, n_new, d = q.shape
    n_pages = page_table.shape[1]
    page = cache_kt.shape[-1]

    def new_map(b, pt):
        return (b, 0, 0)

    def kv_spec(i):
        return pl.BlockSpec((None, FOX_HEADS, FOX_HEAD_DIM, page), lambda b, pt: (pt[b * n_pages + i], 0, 0, 0))

    def lf_spec(i):
        return pl.BlockSpec((None, FOX_HEADS, page), lambda b, pt: (pt[b * n_pages + i], 0, 0))

    pages = range(n_pages)
    grid_spec = pltpu.PrefetchScalarGridSpec(
        num_scalar_prefetch=1,
        grid=(nb,),
        in_specs=[pl.BlockSpec((1, n_new, d), new_map), pl.BlockSpec((1, n_new, d), new_map),
                  pl.BlockSpec((1, n_new, d), new_map), pl.BlockSpec((1, n_new, FOX_HEADS), new_map),
                  *[kv_spec(i) for i in pages], *[kv_spec(i) for i in pages], *[lf_spec(i) for i in pages]],
        out_specs=pl.BlockSpec((1, n_new, d), new_map),
    )
    return pl.pallas_call(
        functools.partial(_fox_decode_kernel, n_pages=n_pages),
        grid_spec=grid_spec,
        out_shape=jax.ShapeDtypeStruct((nb, n_new, d), BF16),
        compiler_params=_params("parallel"),
        name="fox_attn_sample",
    )(page_table.reshape(-1), q, k_new, v_new, lf_new,
      *([cache_kt] * n_pages), *([cache_vt] * n_pages), *([cache_lft] * n_pages))


def _ml_proj_kernel(x_ref, g_ref, wk_ref, wqvo_ref, wg_ref, bg_ref,
                    q_ref, k_ref, v_ref, o_ref, gt_ref, *, feature_major):
    d = x_ref.shape[-1]
    hq = ML_HEADS * ML_DQK
    h = _rms(x_ref[...], g_ref[...]).astype(BF16)
    k_ref[...] = (_dot(h, wk_ref[...]) * (ML_DQK ** -0.5)).astype(BF16)
    if feature_major:
        p = _dot_nt(wqvo_ref[...], h)
        q_ref[0] = p[:hq].astype(BF16)
        v_ref[0] = p[hq:hq + d].astype(BF16)
        o_ref[0] = jax.nn.sigmoid(p[hq + d:])
    else:
        p = _dot(h, wqvo_ref[...])
        q_ref[...] = p[:, :hq].astype(BF16)
        v_ref[...] = p[:, hq:hq + d].astype(BF16)
        o_ref[...] = jax.nn.sigmoid(p[:, hq + d:])
    gates = _dot(h, wg_ref[...]) + bg_ref[...]
    gates = GATE_CAP * jnp.tanh(gates / GATE_CAP)
    lane = lax.broadcasted_iota(jnp.int32, gates.shape, 1)
    gates = jnp.where(lane < ML_HEADS, gates, _log_sigmoid(gates))
    gt_ref[...] = gates[:, :2 * ML_HEADS]


def _ml_proj(x, g, wk, wqvo, wg, bg, seq_len, *, feature_major):
    m, d = x.shape
    hq = ML_HEADS * ML_DQK
    n_seq = m // seq_len
    nl = seq_len // ROW_TILE
    row = lambda b, l: (b * nl + l, 0)
    const = lambda b, l: (0, 0)

    def out(width, dtype):
        if feature_major:
            return (pl.BlockSpec((1, width, ROW_TILE), lambda b, l: (b, 0, l)),
                    jax.ShapeDtypeStruct((n_seq, width, seq_len), dtype))
        return pl.BlockSpec((ROW_TILE, width), row), jax.ShapeDtypeStruct((m, width), dtype)

    (q_spec, q_shape), (v_spec, v_shape), (o_spec, o_shape) = out(hq, BF16), out(d, BF16), out(d, F32)
    return pl.pallas_call(
        functools.partial(_ml_proj_kernel, feature_major=feature_major),
        grid=(n_seq, nl),
        in_specs=[pl.BlockSpec((ROW_TILE, d), row), pl.BlockSpec((1, d), const),
                  pl.BlockSpec(wk.shape, const), pl.BlockSpec(wqvo.shape, const),
                  pl.BlockSpec((d, LANES), const), pl.BlockSpec((1, LANES), const)],
        out_specs=(q_spec, pl.BlockSpec((ROW_TILE, hq), row), v_spec, o_spec,
                   pl.BlockSpec((ROW_TILE, 2 * ML_HEADS), row)),
        out_shape=(q_shape, jax.ShapeDtypeStruct((m, hq), BF16), v_shape, o_shape,
                   jax.ShapeDtypeStruct((m, 2 * ML_HEADS), F32)),
        compiler_params=_params("parallel", "parallel"),
        name="ml_proj_prompt" if feature_major else "ml_proj_sample",
    )(x, g, wk, wqvo, wg, bg)


def _ml_chunk_kernel(qt_ref, k_ref, vt_ref, ogt_ref, gt_ref, gtt_ref, gnb_ref,
                     a_ref, c_ref, n_ref, m_ref, ct_sc, n_sc, m_sc):
    t = k_ref.shape[1]

    @pl.when(pl.program_id(1) == 0)
    def _():
        ct_sc[...] = jnp.zeros_like(ct_sc)
        n_sc[...] = jnp.zeros_like(n_sc)
        m_sc[...] = jnp.zeros_like(m_sc)

    gates = gt_ref[0]
    gates_t = gtt_ref[0]
    n_g = 2 * ML_HEADS
    ri = lax.broadcasted_iota(jnp.int32, (t, t), 0)
    ci = lax.broadcasted_iota(jnp.int32, (t, t), 1)
    causal_t = ri <= ci
    b_cols = _dot_mask(ci <= ri, gates)
    b_rows = _mask_dot(gates_t, causal_t)
    w_cols = jnp.concatenate([gates[:, :ML_HEADS] - b_cols[:, ML_HEADS:], jnp.zeros((t, ML_HEADS), F32)],
                             axis=1)
    left = jnp.concatenate([p.astype(F32) for p in _split3(w_cols)]
                           + [jnp.ones((t, n_g), F32), jnp.zeros((t, LANES - 4 * n_g), F32)],
                           axis=1).astype(BF16)
    right_pad = jnp.zeros((LANES - 4 * n_g, t), F32)
    sub = lax.broadcasted_iota(jnp.int32, (n_g, t), 0)
    sub8 = lax.broadcasted_iota(jnp.int32, (SUBLANES, t), 0)
    lane = lax.broadcasted_iota(jnp.int32, (t, LANES), 1)
    half = (lane < ML_DQK, lane >= ML_DQK)
    row_h = lax.broadcasted_iota(jnp.int32, (LANES, t), 0)
    half_rows = (row_h < ML_DQK, row_h >= ML_DQK)
    lane1 = lax.broadcasted_iota(jnp.int32, (1, LANES), 1)
    head_lane = lax.broadcasted_iota(jnp.int32, (1, ML_HEADS), 1)
    m_all = m_sc[...]
    m_out = jnp.zeros((1, ML_HEADS), F32)

    heads = range(ML_HEADS)
    pairs = range(ML_HEADS // 2)
    pj = lambda h: slice((h // 2) * LANES, (h // 2 + 1) * LANES)
    dv = lambda h: slice(h * ML_DV, (h + 1) * ML_DV)
    i_r = [gates_t[h:h + 1, :] for h in heads]
    b_r = [b_rows[ML_HEADS + h:ML_HEADS + h + 1, :] for h in heads]
    m_prev = [m_all[:, h:h + 1] for h in heads]
    ct = [ct_sc[j] for j in pairs]
    n2 = [n_sc[j] for j in pairs]
    km = [jnp.where(half[h % 2], k_ref[0, :, pj(h)], jnp.zeros((t, LANES), BF16)) for h in heads]

    logw, kq, cq, qn = [], [], [], []
    for h in heads:
        qt2 = qt_ref[0, pj(h), :]
        qtm = jnp.where(half_rows[h % 2], qt2, jnp.zeros_like(qt2))
        u_hi, u_mid, u_lo = (p.astype(F32) for p in _split3(b_r[h]))
        tail = jnp.where(sub == 0, u_hi, jnp.where(sub == 1, u_mid, jnp.where(sub == 2, u_lo, 0.0)))
        pick = (sub == h).astype(F32)
        right = jnp.concatenate([pick, pick, pick, tail, right_pad], axis=0).astype(BF16)
        logw.append(_dot(left, right))
        kq.append(_dot(km[h], qt2))
        cq.append(_dot(ct[h // 2].astype(BF16), qtm))
        qn.append(_dot(n2[h // 2].astype(BF16), qtm)[0:1, :])

    m_t, w_inter, a, a_sum, vtw, wk8, decay = [], [], [], [], [], [], []
    for h in heads:
        lw = jnp.where(causal_t, logw[h], NEG_INF)
        inter = b_r[h] + m_prev[h]
        m_t.append(jnp.maximum(inter, jnp.max(lw, axis=0, keepdims=True)))
        w_inter.append(jnp.exp(inter - m_t[h]))
        a_h = jnp.exp(lw - m_t[h]) * kq[h]
        a_sum.append(jnp.sum(a_h, axis=0, keepdims=True))
        a.append(a_h.astype(BF16))
        b_last = b_r[h][:, t - 1:t]
        m_new = m_t[h][:, t - 1:t]
        decay.append(jnp.exp(b_last + m_prev[h] - m_new))
        wk = jnp.exp(b_last - b_r[h] + i_r[h] - m_new)
        vtw.append((vt_ref[0, dv(h), :].astype(F32) * wk).astype(BF16))
        wk8.append(jnp.where(sub8 == 0, wk, 0.0).astype(BF16))
        m_out = jnp.where(head_lane == h, m_new, m_out)

    va = [_dot(vt_ref[0, dv(h), :], a[h]) for h in heads]
    c_add = [_dot(vtw[h], km[h]) for h in heads]
    n_add = [_dot(wk8[h], km[h]) for h in heads]

    for h in heads:
        hu = w_inter[h] * cq[h] + va[h]
        den = w_inter[h] * qn[h] + a_sum[h]
        r1 = 1.0 / jnp.maximum(jnp.abs(den), jnp.exp(-m_t[h]))
        ms = jnp.mean(hu * hu, axis=0, keepdims=True)
        hn = hu * (r1 * lax.rsqrt(r1 * r1 * ms + EPS)) * _lane_tile(gnb_ref[dv(h), :], t)
        a_ref[0, :, dv(h)] = (ogt_ref[0, dv(h), :] * hn).T.astype(BF16)
    for j in pairs:
        d2 = jnp.where(lane1 < ML_DQK, decay[2 * j], decay[2 * j + 1])
        ct_sc[j] = d2 * ct[j] + c_add[2 * j] + c_add[2 * j + 1]
        n_sc[j] = d2 * n2[j] + n_add[2 * j] + n_add[2 * j + 1]
    m_sc[...] = m_out

    @pl.when(pl.program_id(1) == pl.num_programs(1) - 1)
    def _():
        for j in range(ML_HEADS // 2):
            c_ref[0, j] = ct_sc[j].T
            n_ref[0, j:j + 1, :] = n_sc[j][0:1, :]
        m_ref[0] = m_out


def _ml_step_kernel(q_ref, k_ref, v_ref, og_ref, gt_ref, gtt_ref, gn_ref, c0_ref, n0_ref, m0_ref,
                    a_ref, c_ref, n_ref, m_ref):
    g, t = q_ref.shape[0], q_ref.shape[1]
    gates = gt_ref[...]
    gates_t = gtt_ref[...]
    ri = lax.broadcasted_iota(jnp.int32, (g, t, t), 1)
    ci = lax.broadcasted_iota(jnp.int32, (g, t, t), 2)
    causal = ci <= ri
    eye = jnp.broadcast_to(
        (lax.broadcasted_iota(jnp.int32, (ML_DQK, ML_DQK), 0)
         == lax.broadcasted_iota(jnp.int32, (ML_DQK, ML_DQK), 1)).astype(BF16)[None], (g, ML_DQK, ML_DQK))
    head_lane = lax.broadcasted_iota(jnp.int32, (g, 1, ML_HEADS), 2)
    m_all = m0_ref[...]
    m_out = jnp.zeros((g, 1, ML_HEADS), F32)

    for h in range(ML_HEADS):
        sl = slice(h * ML_DV, (h + 1) * ML_DV)
        sq = slice(h * ML_DQK, (h + 1) * ML_DQK)
        fh = ML_HEADS + h
        q_h = q_ref[:, :, sq]
        k_h = k_ref[:, :, sq]
        v_h = v_ref[:, :, sl]
        c_h = c0_ref[:, h]
        n_h = n0_ref[:, h:h + 1, :]
        i_c, f_c = gates[:, :, h:h + 1], gates[:, :, fh:fh + 1]
        i_r, f_r = gates_t[:, h:h + 1, :], gates_t[:, fh:fh + 1, :]
        m_prev = m_all[:, :, h:h + 1]
        b_c = jnp.sum(jnp.where(causal, f_r, 0.0), axis=2, keepdims=True)
        b_r = jnp.sum(jnp.where(ri <= ci, f_c, 0.0), axis=1, keepdims=True)
        logw = jnp.where(causal, b_c - b_r + i_r, NEG_INF)
        inter = b_c + m_prev
        m_t = jnp.maximum(inter, jnp.max(logw, axis=2, keepdims=True))
        w_inter = jnp.exp(inter - m_t)
        a = jnp.exp(logw - m_t) * jnp.einsum("gtd,gsd->gts", q_h, k_h, preferred_element_type=F32)
        num = (w_inter * jnp.einsum("gtd,gde->gte", q_h, c_h.astype(BF16), preferred_element_type=F32)
               + jnp.einsum("gts,gse->gte", a.astype(BF16), v_h, preferred_element_type=F32))
        qn = jnp.sum(q_h.astype(F32) * n_h, axis=2, keepdims=True)
        den = w_inter * qn + jnp.sum(a, axis=2, keepdims=True)
        hs = num / jnp.maximum(jnp.abs(den), jnp.exp(-m_t))
        hn = hs * lax.rsqrt(jnp.mean(hs * hs, axis=2, keepdims=True) + EPS) * gn_ref[:, sl]
        a_ref[:, :, sl] = (og_ref[:, :, sl] * hn).astype(BF16)
        b_last = b_c[:, t - 1:t, :]
        m_new = m_t[:, t - 1:t, :]
        decay = jnp.exp(b_last + m_prev - m_new)
        kw = jnp.exp(b_last - b_c + i_c - m_new) * k_h.astype(F32)
        kt_h = jnp.einsum("gdk,gsk->gds", eye, k_h, preferred_element_type=F32)
        kwt = (kt_h * jnp.exp(b_last - b_r + i_r - m_new)).astype(BF16)
        c_ref[:, h] = decay * c_h + jnp.einsum("gds,gse->gde", kwt, v_h, preferred_element_type=F32)
        n_ref[:, h:h + 1, :] = decay * n_h + jnp.sum(kw, axis=1, keepdims=True)
        m_out = jnp.where(head_lane == h, m_new, m_out)
    m_ref[...] = m_out


def _ml_chunk(qt, k, vt, ogt, gates, gates_t, gnb, chunk):
    nb, d, seq_len = vt.shape
    hq = k.shape[2]
    pairs = ML_HEADS // 2
    tok = lambda width: pl.BlockSpec((1, chunk, width), lambda b, c: (b, c, 0))
    tok_t = lambda width: pl.BlockSpec((1, width, chunk), lambda b, c: (b, 0, c))
    return pl.pallas_call(
        _ml_chunk_kernel,
        grid=(nb, seq_len // chunk),
        in_specs=[tok_t(hq), tok(hq), tok_t(d), tok_t(d), tok(2 * ML_HEADS), tok_t(2 * ML_HEADS),
                  pl.BlockSpec((d, LANES), lambda b, c: (0, 0))],
        out_specs=(tok(d), pl.BlockSpec((1, pairs, LANES, LANES), lambda b, c: (b, 0, 0, 0)),
                   pl.BlockSpec((1, pairs, LANES), lambda b, c: (b, 0, 0)),
                   pl.BlockSpec((1, 1, ML_HEADS), lambda b, c: (b, 0, 0))),
        out_shape=(jax.ShapeDtypeStruct((nb, seq_len, d), BF16),
                   jax.ShapeDtypeStruct((nb, pairs, LANES, LANES), F32),
                   jax.ShapeDtypeStruct((nb, pairs, LANES), F32),
                   jax.ShapeDtypeStruct((nb, 1, ML_HEADS), F32)),
        scratch_shapes=[pltpu.VMEM((pairs, ML_DV, LANES), F32), pltpu.VMEM((pairs, SUBLANES, LANES), F32),
                        pltpu.VMEM((1, ML_HEADS), F32)],
        compiler_params=_params("parallel", "arbitrary"),
        name="ml_chunk",
    )(qt, k, vt, ogt, gates, gates_t, gnb)


def _ml_step(q, k, v, og, gates, gates_t, gn, c0, n0, m0):
    nb, t, d = v.shape
    hq = q.shape[2]
    g = ML_STEP_SEQS
    lead = lambda *rest: pl.BlockSpec((g, *rest), lambda i: (i,) + (0,) * len(rest))
    state = (lead(ML_HEADS, ML_DQK, ML_DV), lead(ML_HEADS, ML_DQK), lead(1, ML_HEADS))
    return pl.pallas_call(
        _ml_step_kernel,
        grid=(nb // g,),
        in_specs=[lead(t, hq), lead(t, hq), lead(t, d), lead(t, d), lead(t, 2 * ML_HEADS),
                  lead(2 * ML_HEADS, t), pl.BlockSpec((1, d), lambda i: (0, 0)), *state],
        out_specs=(lead(t, d), *state),
        out_shape=(jax.ShapeDtypeStruct((nb, t, d), BF16), jax.ShapeDtypeStruct(c0.shape, F32),
                   jax.ShapeDtypeStruct(n0.shape, F32), jax.ShapeDtypeStruct(m0.shape, F32)),
        compiler_params=_params("parallel"),
        name="ml_step",
    )(q, k, v, og, gates, gates_t, gn, c0, n0, m0)


def _pad_cols(w, width):
    return jnp.pad(w, ((0, 0), (0, width - w.shape[1])))


def kernel(x_prompt, x_sample, state_conv, cache_k, cache_v, cache_logf, page_table, state_C, state_n, state_m, norm_mix, norm_ffn, norm_final, w_conv_in, w_conv, w_conv_out, w_fox_in, b_fox_f, w_fox_out, w_ml_in, b_ml_gates, g_ml_norm, w_ml_out, w_ffn_gu, w_ffn_down):
    bp, lp, d = x_prompt.shape
    bs, ls, _ = x_sample.shape
    depth = norm_mix.shape[0]
    dff = w_ffn_down.shape[1]
    assert d == D_MODEL and ls == SUBLANES
    xp = x_prompt.reshape(bp * lp, d)
    xs = x_sample.reshape(bs * ls, d)
    gf = norm_final.reshape(1, d)
    hq = ML_HEADS * ML_DQK
    pairs = ML_HEADS // 2
    out = {name: [] for name in ("conv_p", "conv_s", "kp", "vp", "lfp", "ks", "vs", "lfs",
                                 "cp", "np", "mp", "cs", "ns", "ms")}

    for i in range(depth):
        kind, j = i % 3, i // 3
        gm = norm_mix[i].reshape(1, d)
        if kind == 0:
            win = w_conv_in[j].astype(BF16)
            wo = w_conv_out[j].astype(BF16)
            ap, sp = _conv_mixer(xp, gm, win, w_conv[j], jnp.zeros((bp, CONV_W - 1, d), F32), lp)
            as_, ss = _conv_mixer(xs, gm, win, w_conv[j], state_conv[j], ls)
            out["conv_p"].append(sp)
            out["conv_s"].append(ss)
        elif kind == 1:
            wqkv = w_fox_in[j][:, :3 * d].astype(BF16)
            wf = _pad_cols(w_fox_in[j][:, 3 * d:], LANES).astype(BF16)
            bf = _pad_cols(b_fox_f[j].reshape(1, FOX_HEADS), LANES)
            wo = w_fox_out[j].astype(BF16)
            q, kt, vt, ktb, vtb, lft, c, ct = _fox_proj_prompt(
                xp, gm, wqkv[:, :d], wqkv[:, d:].T, wf, wf[:, :FOX_HEADS].T, bf,
                b_fox_f[j].reshape(FOX_HEADS, 1), lp)
            ap = _fox_attn_prompt(q, ktb, vtb, c, ct, lp)
            heads_last = lambda t: jnp.transpose(t.reshape(bp, FOX_HEADS, FOX_HEAD_DIM, lp), (0, 3, 1, 2))
            out["kp"].append(heads_last(kt))
            out["vp"].append(heads_last(vt))
            out["lfp"].append(jnp.swapaxes(lft, 1, 2))
            q, k, v, lf = _fox_proj_sample(xs, gm, wqkv, wf, bf)
            as_ = _fox_attn_sample(
                q.reshape(bs, ls, d), k.reshape(bs, ls, d), v.reshape(bs, ls, d),
                lf.reshape(bs, ls, FOX_HEADS), jnp.transpose(cache_k[j], (0, 2, 3, 1)),
                jnp.transpose(cache_v[j], (0, 2, 3, 1)), jnp.transpose(cache_logf[j], (0, 2, 1)),
                page_table).reshape(bs * ls, d)
            out["ks"].append(k.reshape(bs, ls, FOX_HEADS, FOX_HEAD_DIM))
            out["vs"].append(v.reshape(bs, ls, FOX_HEADS, FOX_HEAD_DIM))
            out["lfs"].append(lf.reshape(bs, ls, FOX_HEADS))
        else:
            w = w_ml_in[j]
            wk = w[:, hq:2 * hq].astype(BF16)
            wqvo = jnp.concatenate([w[:, :hq], w[:, 2 * hq:2 * hq + 2 * d]], axis=1).astype(BF16)
            wg = _pad_cols(w[:, 2 * hq + 2 * d:], LANES).astype(BF16)
            bg = _pad_cols(b_ml_gates[j].reshape(1, 2 * ML_HEADS), LANES)
            gn = g_ml_norm[j].reshape(1, d)
            wo = w_ml_out[j].astype(BF16)

            def tokens(nb, seq_len, *arrays):
                return [t.reshape(nb, seq_len, t.shape[-1]) for t in arrays]

            qt, k, vt, ogt, gates = _ml_proj(xp, gm, wk, wqvo.T, wg, bg, lp, feature_major=True)
            k, gates = tokens(bp, lp, k, gates)
            a, c1, n1, m1 = _ml_chunk(qt, k, vt, ogt, gates, jnp.swapaxes(gates, 1, 2),
                                      jnp.broadcast_to(gn.reshape(d, 1), (d, LANES)), ML_CHUNK_PROMPT)
            ap = a.reshape(bp * lp, d)
            out["cp"].append(c1.reshape(bp, ML_HEADS, ML_DQK, ML_DV))
            out["np"].append(n1.reshape(bp, ML_HEADS, ML_DQK))
            out["mp"].append(m1.reshape(bp, ML_HEADS))

            q, k, v, og, gates = _ml_proj(xs, gm, wk, wqvo, wg, bg, ROW_TILE, feature_major=False)
            q, k, v, og, gates = tokens(bs, ls, q, k, v, og, gates)
            a, c1, n1, m1 = _ml_step(q, k, v, og, gates, jnp.swapaxes(gates, 1, 2), gn,
                                     state_C[j], state_n[j], state_m[j].reshape(bs, 1, ML_HEADS))
            as_ = a.reshape(bs * ls, d)
            out["cs"].append(c1); out["ns"].append(n1); out["ms"].append(m1.reshape(bs, ML_HEADS))

        gffn = norm_ffn[i].reshape(1, d)
        wg_ffn = w_ffn_gu[i][:, :dff].astype(BF16)
        wu_ffn = w_ffn_gu[i][:, dff:].astype(BF16)
        wd_ffn = w_ffn_down[i].astype(BF16)
        last = i == depth - 1
        xp = _post(xp, ap, wo, gffn, wg_ffn, wu_ffn, wd_ffn, gf, final_norm=last)
        xs = _post(xs, as_, wo, gffn, wg_ffn, wu_ffn, wd_ffn, gf, final_norm=last)

    st = lambda name: jnp.stack(out[name])
    return (xp.reshape(bp, lp, d), xs.reshape(bs, ls, d), st("conv_p"), st("conv_s"),
            st("kp"), st("vp"), st("lfp"), st("ks"), st("vs"), st("lfs"),
            st("cp"), st("np"), st("mp"), st("cs"), st("ns"), st("ms"))
```

```python
import functools

import jax
import jax.numpy as jnp
from jax import lax
from jax.experimental import pallas as pl
from jax.experimental.pallas import tpu as pltpu

F32 = jnp.float32
BF16 = jnp.bfloat16
HIGHEST = lax.Precision.HIGHEST

D_MODEL = 1024
CONV_W = 3
FOX_HEADS = 16
FOX_HEAD_DIM = D_MODEL // FOX_HEADS
ML_HEADS = 8
ML_DV = D_MODEL // ML_HEADS
ML_DQK = ML_DV // 2
GATE_CAP = 15.0
EPS = 1e-6
NEG_INF = -1e30

LANES = 128
SUBLANES = 8
ROW_TILE = 512
POST_ROW_TILE = 1024
FFN_CHUNK = 256
ATTN_TQ = 512
ML_CHUNK_PROMPT = 128
ML_STEP_SEQS = 16
VMEM_LIMIT = 56 * 1024 * 1024


def _params(*sem):
    return pltpu.CompilerParams(dimension_semantics=sem, vmem_limit_bytes=VMEM_LIMIT)


def _rms(x, g):
    return x * lax.rsqrt(jnp.mean(x * x, axis=-1, keepdims=True) + EPS) * g


def _log_sigmoid(x):
    return jnp.minimum(x, 0.0) - jnp.log1p(jnp.exp(-jnp.abs(x)))


def _dot(a, b):
    return jnp.dot(a, b, preferred_element_type=F32)


def _dot_hi(a, b):
    return jnp.dot(a, b, precision=HIGHEST, preferred_element_type=F32)


def _dot_nt(a, b, precision=None):
    return lax.dot_general(a, b, (((1,), (1,)), ((), ())), precision=precision,
                           preferred_element_type=F32)


def _dot_tn(a, b):
    return lax.dot_general(a, b, (((0,), (0,)), ((), ())), preferred_element_type=F32)


def _lane_tile(x, width):
    return jnp.tile(x, (1, width // LANES))


def _post_kernel(x_ref, a_ref, wo_ref, g_ref, wg_ref, wu_ref, wd_ref, gf_ref, o_ref, *, final_norm):
    x1 = x_ref[...] + _dot(a_ref[...], wo_ref[...])
    h = _rms(x1, g_ref[...]).astype(BF16)
    y = x1
    for c in range(0, wg_ref.shape[1], FFN_CHUNK):
        g = _dot(h, wg_ref[:, c:c + FFN_CHUNK])
        u = _dot(h, wu_ref[:, c:c + FFN_CHUNK])
        y = y + _dot(((g * jax.nn.sigmoid(g)) * u).astype(BF16), wd_ref[c:c + FFN_CHUNK, :])
    o_ref[...] = _rms(y, gf_ref[...]) if final_norm else y


def _post(x, a, wo, g, wg, wu, wd, gf, *, final_norm):
    m, d = x.shape
    dff = wg.shape[1]
    tm = min(POST_ROW_TILE, m)
    assert dff % FFN_CHUNK == 0 and m % tm == 0
    row = pl.BlockSpec((tm, d), lambda i: (i, 0))
    resident = lambda shape: pl.BlockSpec(shape, lambda i: (0, 0), pipeline_mode=pl.Buffered(1))
    return pl.pallas_call(
        functools.partial(_post_kernel, final_norm=final_norm),
        grid=(m // tm,),
        in_specs=[row, row, resident((d, d)), resident((1, d)), resident((d, dff)), resident((d, dff)),
                  resident((dff, d)), resident((1, d))],
        out_specs=row,
        out_shape=jax.ShapeDtypeStruct((m, d), F32),
        compiler_params=_params("parallel"),
        name="post_ffn",
    )(x, a, wo, g, wg, wu, wd, gf)


def _conv_front(x_ref, g_ref, win_ref):
    d = x_ref.shape[-1]
    h = _rms(x_ref[...], g_ref[...]).astype(BF16)
    p = _dot(h, win_ref[...])
    return p[:, :d], p[:, d:2 * d] * p[:, 2 * d:]


def _conv_taps(u, rig, p0, p1, wc_ref):
    s1 = jnp.where(rig == 0, p1, pltpu.roll(u, 1, axis=0))
    s2 = jnp.where(rig == 0, p0, jnp.where(rig == 1, p1, pltpu.roll(u, 2, axis=0)))
    return wc_ref[0:1, :] * s2 + wc_ref[1:2, :] * s1 + wc_ref[2:3, :] * u


def _conv_prompt_kernel(x_ref, g_ref, win_ref, wc_ref, prev_ref, z_ref, st_ref, carry_sc):
    @pl.when(pl.program_id(1) == 0)
    def _():
        carry_sc[...] = prev_ref[0]

    bg, u = _conv_front(x_ref, g_ref, win_ref)
    r = u.shape[0]
    rig = lax.broadcasted_iota(jnp.int32, u.shape, 0)
    conv = _conv_taps(u, rig, carry_sc[0:1, :], carry_sc[1:2, :], wc_ref)
    z_ref[...] = (bg * conv).astype(BF16)
    tail = u[r - (CONV_W - 1):, :]
    carry_sc[...] = tail
    st_ref[0] = tail


def _conv_sample_kernel(x_ref, g_ref, win_ref, wc_ref, prev_ref, z_ref, st_ref):
    bg, u = _conv_front(x_ref, g_ref, win_ref)
    r, d = u.shape
    n_seq = r // SUBLANES
    prev = prev_ref[...]
    p0 = jnp.broadcast_to(prev[:, 0:1, :], (n_seq, SUBLANES, d)).reshape(r, d)
    p1 = jnp.broadcast_to(prev[:, 1:2, :], (n_seq, SUBLANES, d)).reshape(r, d)
    rig = lax.broadcasted_iota(jnp.int32, u.shape, 0) & (SUBLANES - 1)
    conv = _conv_taps(u, rig, p0, p1, wc_ref)
    z_ref[...] = (bg * conv).astype(BF16)
    st_ref[...] = u.reshape(n_seq, SUBLANES, d)[:, SUBLANES - (CONV_W - 1):, :]


def _conv_mixer(x, g, win, wc, prev, seq_len):
    m, d = x.shape
    n_seq = m // seq_len
    common_in = [pl.BlockSpec((d, 3 * d), lambda *_: (0, 0)), pl.BlockSpec((CONV_W, d), lambda *_: (0, 0))]
    out_shape = (jax.ShapeDtypeStruct((m, d), BF16), jax.ShapeDtypeStruct((n_seq, CONV_W - 1, d), F32))
    if seq_len == SUBLANES:
        per = ROW_TILE // SUBLANES
        return pl.pallas_call(
            _conv_sample_kernel,
            grid=(m // ROW_TILE,),
            in_specs=[pl.BlockSpec((ROW_TILE, d), lambda i: (i, 0)), pl.BlockSpec((1, d), lambda i: (0, 0)),
                      *common_in, pl.BlockSpec((per, CONV_W - 1, d), lambda i: (i, 0, 0))],
            out_specs=(pl.BlockSpec((ROW_TILE, d), lambda i: (i, 0)),
                       pl.BlockSpec((per, CONV_W - 1, d), lambda i: (i, 0, 0))),
            out_shape=out_shape,
            compiler_params=_params("parallel"),
            name="conv_sample",
        )(x, g, win, wc, prev)
    assert seq_len % ROW_TILE == 0
    nl = seq_len // ROW_TILE
    return pl.pallas_call(
        _conv_prompt_kernel,
        grid=(n_seq, nl),
        in_specs=[pl.BlockSpec((ROW_TILE, d), lambda b, l: (b * nl + l, 0)),
                  pl.BlockSpec((1, d), lambda b, l: (0, 0)),
                  *common_in, pl.BlockSpec((1, CONV_W - 1, d), lambda b, l: (b, 0, 0))],
        out_specs=(pl.BlockSpec((ROW_TILE, d), lambda b, l: (b * nl + l, 0)),
                   pl.BlockSpec((1, CONV_W - 1, d), lambda b, l: (b, 0, 0))),
        out_shape=out_shape,
        scratch_shapes=[pltpu.VMEM((CONV_W - 1, d), F32)],
        compiler_params=_params("parallel", "arbitrary"),
        name="conv_prompt",
    )(x, g, win, wc, prev)


def _fox_proj_kernel(x_ref, g_ref, wqkv_ref, wf_ref, bf_ref, q_ref, k_ref, v_ref, lf_ref):
    d = x_ref.shape[-1]
    h = _rms(x_ref[...], g_ref[...]).astype(BF16)
    p = _dot(h, wqkv_ref[...])
    logf = _log_sigmoid(_dot(h, wf_ref[...]) + bf_ref[...])
    q_ref[...] = p[:, :d] * (FOX_HEAD_DIM ** -0.5)
    k_ref[...] = p[:, d:2 * d]
    v_ref[...] = p[:, 2 * d:]
    lf_ref[...] = logf[:, :FOX_HEADS]


def _fox_proj_sample(x, g, wqkv, wf, bf):
    m, d = x.shape
    row = lambda i: (i, 0)
    const = lambda i: (0, 0)
    big = pl.BlockSpec((ROW_TILE, d), row)
    f32_out = jax.ShapeDtypeStruct((m, d), F32)
    return pl.pallas_call(
        _fox_proj_kernel,
        grid=(m // ROW_TILE,),
        in_specs=[big, pl.BlockSpec((1, d), const), pl.BlockSpec((d, 3 * d), const),
                  pl.BlockSpec((d, LANES), const), pl.BlockSpec((1, LANES), const)],
        out_specs=(big, big, big, pl.BlockSpec((ROW_TILE, FOX_HEADS), row)),
        out_shape=(f32_out, f32_out, f32_out, jax.ShapeDtypeStruct((m, FOX_HEADS), F32)),
        compiler_params=_params("parallel"),
        name="fox_proj_sample",
    )(x, g, wqkv, wf, bf)


def _fox_proj_prompt_kernel(x_ref, g_ref, wk_ref, wqkvt_ref, wf_ref, wft_ref, bf_ref, bft_ref,
                            qtb_ref, kt_ref, vt_ref, kb_ref, vtb_ref, lft_ref, c_ref, ct_ref,
                            carry_sc, carryt_sc):
    d = x_ref.shape[-1]
    r = x_ref.shape[0]
    h = _rms(x_ref[...], g_ref[...]).astype(BF16)
    kb_ref[...] = _dot(h, wk_ref[...]).astype(BF16)
    qkv = _dot_nt(wqkvt_ref[...], h)
    qtb_ref[0] = (qkv[:d] * (FOX_HEAD_DIM ** -0.5)).astype(BF16)
    kt_ref[0] = qkv[d:2 * d]
    vt_ref[0] = qkv[2 * d:]
    vtb_ref[0] = qkv[2 * d:].astype(BF16)
    logf = _log_sigmoid(_dot(h, wf_ref[...]) + bf_ref[...])
    logf_t = _log_sigmoid(_dot_nt(wft_ref[...], h) + bft_ref[...])
    lft_ref[0] = logf_t

    @pl.when(pl.program_id(1) == 0)
    def _():
        carry_sc[...] = jnp.zeros_like(carry_sc)
        carryt_sc[...] = jnp.zeros_like(carryt_sc)

    ri = lax.broadcasted_iota(jnp.int32, (r, r), 0)
    ci = lax.broadcasted_iota(jnp.int32, (r, r), 1)
    c = carry_sc[...] + _dot_mask(ci <= ri, logf)
    carry_sc[...] = c[r - 1:r, :]
    c_ref[...] = c[:, :FOX_HEADS]
    ct = _lane_tile(carryt_sc[...], r) + _mask_dot(logf_t, ri <= ci)
    carryt_sc[...] = jnp.broadcast_to(ct[:, r - 1:r], carryt_sc.shape)
    ct_ref[0] = ct


def _fox_proj_prompt(x, g, wk, wqkvt, wf, wft, bf, bft, seq_len):
    m, d = x.shape
    n_seq = m // seq_len
    nl = seq_len // ROW_TILE
    row = lambda b, l: (b * nl + l, 0)
    const = lambda b, l: (0, 0)
    feat = lambda width: pl.BlockSpec((1, width, ROW_TILE), lambda b, l: (b, 0, l))
    feat_shape = lambda width, dtype: jax.ShapeDtypeStruct((n_seq, width, seq_len), dtype)
    return pl.pallas_call(
        _fox_proj_prompt_kernel,
        grid=(n_seq, nl),
        in_specs=[pl.BlockSpec((ROW_TILE, d), row), pl.BlockSpec((1, d), const),
                  pl.BlockSpec((d, d), const), pl.BlockSpec((3 * d, d), const),
                  pl.BlockSpec((d, LANES), const), pl.BlockSpec((FOX_HEADS, d), const),
                  pl.BlockSpec((1, LANES), const), pl.BlockSpec((FOX_HEADS, 1), const)],
        out_specs=(feat(d), feat(d), feat(d), pl.BlockSpec((ROW_TILE, d), row), feat(d), feat(FOX_HEADS),
                   pl.BlockSpec((ROW_TILE, FOX_HEADS), row), feat(FOX_HEADS)),
        out_shape=(feat_shape(d, BF16), feat_shape(d, F32), feat_shape(d, F32),
                   jax.ShapeDtypeStruct((m, d), BF16), feat_shape(d, BF16), feat_shape(FOX_HEADS, F32),
                   jax.ShapeDtypeStruct((m, FOX_HEADS), F32), feat_shape(FOX_HEADS, F32)),
        scratch_shapes=[pltpu.VMEM((1, LANES), F32), pltpu.VMEM((FOX_HEADS, LANES), F32)],
        compiler_params=_params("parallel", "arbitrary"),
        name="fox_proj_prompt",
    )(x, g, wk, wqkvt, wf, wft, bf, bft)


def _fox_attn_kernel(qt_ref, k_ref, vt_ref, c_ref, ct_ref, o_ref, kaug_sc, m_sc, l_sc, acc_sc, *, tq):
    hp = pl.program_id(1)
    qi = pl.program_id(2)
    seq = k_ref.shape[0]
    hd = FOX_HEAD_DIM
    n_aug = 2 * SUBLANES

    @pl.when(qi == 0)
    def _():
        c3 = jnp.concatenate([p.astype(F32) for p in _split3(c_ref[...])]
                             + [jnp.zeros((seq, LANES - 3 * FOX_HEADS), F32)], axis=1).astype(BF16)
        src = lax.broadcasted_iota(jnp.int32, (LANES, LANES), 0)
        dst = lax.broadcasted_iota(jnp.int32, (LANES, LANES), 1)
        term = src >> (FOX_HEADS.bit_length() - 1)
        lane = lax.broadcasted_iota(jnp.int32, (seq, LANES), 1)
        k2 = k_ref[...]
        for hh in range(2):
            off = hd * (1 - hh)
            place = jnp.where((src & (FOX_HEADS - 1)) == 2 * hp + hh,
                              jnp.where(dst == off + term, -1.0, 0.0), 0.0).astype(BF16)
            ones = jnp.where(lane >= off + 3, jnp.where(lane < off + 6, 1.0, 0.0), 0.0)
            aug = (_dot(c3, place) + ones).astype(BF16)
            own = (lane >= hd) if hh else (lane < hd)
            kaug_sc[hh] = jnp.where(own, k2, aug)

    qt2 = qt_ref[0]
    sub = lax.broadcasted_iota(jnp.int32, (n_aug, tq), 0)
    qa = []
    for hh in range(2):
        c_hi, c_mid, c_lo = (p.astype(F32) for p in _split3(ct_ref[0, pl.ds(2 * hp + hh, 1), :]))
        piece = jnp.where(sub < 3, 1.0, jnp.where(sub == 3, c_hi, jnp.where(sub == 4, c_mid,
                          jnp.where(sub == 5, c_lo, 0.0)))).astype(BF16)
        rest = jnp.zeros((hd - n_aug, tq), BF16)
        qa.append(jnp.concatenate([qt2[:hd], piece, rest] if hh == 0 else [piece, rest, qt2[hd:]], axis=0))
        m_sc[hh] = jnp.full((1, tq), NEG_INF, F32)
        l_sc[hh] = jnp.zeros((1, tq), F32)
    acc_sc[...] = jnp.zeros_like(acc_sc)

    visible = (lax.broadcasted_iota(jnp.int32, (tq, tq), 0)
               <= lax.broadcasted_iota(jnp.int32, (tq, tq), 1))

    def block(ki, diagonal):
        start = pl.multiple_of(ki * tq, tq)
        s2 = [_dot(kaug_sc[hh, pl.ds(start, tq), :], qa[hh]) for hh in range(2)]
        p2, alpha2 = [], []
        for hh in range(2):
            s = jnp.where(visible, s2[hh], NEG_INF) if diagonal else s2[hh]
            m_prev = m_sc[hh]
            m_new = jnp.maximum(m_prev, jnp.max(s, axis=0, keepdims=True))
            alpha = jnp.exp(m_prev - m_new)
            p = jnp.exp(s - m_new)
            l_sc[hh] = alpha * l_sc[hh] + jnp.sum(p, axis=0, keepdims=True)
            m_sc[hh] = m_new
            p2.append(p.astype(BF16))
            alpha2.append(alpha)
        pv2 = [_dot(vt_ref[0, hh * hd:(hh + 1) * hd, pl.ds(start, tq)], p2[hh]) for hh in range(2)]
        for hh in range(2):
            rows = slice(hh * hd, (hh + 1) * hd)
            acc_sc[rows, :] = alpha2[hh] * acc_sc[rows, :] + pv2[hh]

    def body(ki, carry):
        block(ki, False)
        return carry

    lax.fori_loop(0, qi, body, 0)
    block(qi, True)
    o_t = jnp.concatenate([acc_sc[:hd, :] / l_sc[0], acc_sc[hd:, :] / l_sc[1]], axis=0)
    o_ref[...] = o_t.T.astype(BF16)


def _fox_attn_prompt(qtb, kb, vtb, c, ct, seq_len):
    n_seq, d, _ = qtb.shape
    tq = ATTN_TQ
    nq = seq_len // tq
    return pl.pallas_call(
        functools.partial(_fox_attn_kernel, tq=tq),
        grid=(n_seq, d // LANES, nq),
        in_specs=[pl.BlockSpec((1, LANES, tq), lambda b, hp, qi: (b, hp, qi)),
                  pl.BlockSpec((seq_len, LANES), lambda b, hp, qi: (b, hp)),
                  pl.BlockSpec((1, LANES, seq_len), lambda b, hp, qi: (b, hp, 0)),
                  pl.BlockSpec((seq_len, FOX_HEADS), lambda b, hp, qi: (b, 0)),
                  pl.BlockSpec((1, FOX_HEADS, tq), lambda b, hp, qi: (b, 0, qi))],
        out_specs=pl.BlockSpec((tq, LANES), lambda b, hp, qi: (b * nq + qi, hp)),
        out_shape=jax.ShapeDtypeStruct((n_seq * seq_len, d), BF16),
        scratch_shapes=[pltpu.VMEM((2, seq_len, LANES), BF16), pltpu.VMEM((2, 1, tq), F32),
                        pltpu.VMEM((2, 1, tq), F32), pltpu.VMEM((LANES, tq), F32)],
        compiler_params=_params("parallel", "parallel", "arbitrary"),
        name="fox_attn_prompt",
    )(qtb, kb, vtb, c, ct)


def _split3(x):
    hi = x.astype(BF16)
    r1 = x - hi.astype(F32)
    mid = r1.astype(BF16)
    return hi, mid, (r1 - mid.astype(F32)).astype(BF16)


def _mask_dot(x, mask):
    mask = mask.astype(BF16)
    hi, mid, lo = _split3(x)
    return _dot(hi, mask) + _dot(mid, mask) + _dot(lo, mask)


def _dot_mask(mask, x, nt=False):
    mask = mask.astype(BF16)
    dot = _dot_nt if nt else _dot
    hi, mid, lo = _split3(x)
    return dot(mask, hi) + dot(mask, mid) + dot(mask, lo)


def _fox_decode_kernel(pt_ref, q_ref, kn_ref, vn_ref, lfn_ref, *refs, n_pages):
    del pt_ref
    kt_refs, vt_refs, lft_refs = refs[:n_pages], refs[n_pages:2 * n_pages], refs[2 * n_pages:3 * n_pages]
    o_ref = refs[3 * n_pages]
    page = kt_refs[0].shape[-1]
    n_new, d = q_ref.shape[1], q_ref.shape[2]
    rows = FOX_HEADS * n_new
    assert rows == page == LANES and n_new == SUBLANES
    ri = lax.broadcasted_iota(jnp.int32, (rows, page), 0)
    ki = lax.broadcasted_iota(jnp.int32, (rows, page), 1)
    tok_bits = n_new.bit_length() - 1
    dim_bits = FOX_HEAD_DIM.bit_length() - 1
    head_of_row = ((lax.broadcasted_iota(jnp.int32, (rows, FOX_HEADS), 0) >> tok_bits)
                   == lax.broadcasted_iota(jnp.int32, (rows, FOX_HEADS), 1))
    own_cols = ((lax.broadcasted_iota(jnp.int32, (rows, d), 0) >> tok_bits)
                == (lax.broadcasted_iota(jnp.int32, (rows, d), 1) >> dim_bits))
    qbd = jnp.where(own_cols, jnp.tile(q_ref[0], (FOX_HEADS, 1)), 0.0).astype(BF16)

    pad = page - n_new
    kn = jnp.concatenate([kn_ref[0], jnp.zeros((pad, d), F32)], axis=0).astype(BF16)
    vn = jnp.concatenate([vn_ref[0], jnp.zeros((pad, d), F32)], axis=0).astype(BF16)
    lfn = jnp.concatenate([lfn_ref[0], jnp.zeros((pad, FOX_HEADS), F32)], axis=0)
    lf_rows = _dot_mask(head_of_row, lfn, nt=True)
    cum = _mask_dot(lf_rows, ri <= ki)
    t_of_row = ri & (n_new - 1)
    a = jnp.sum(jnp.where(ki == t_of_row, cum, 0.0), axis=1, keepdims=True)
    s_new = jnp.where(ki <= t_of_row, _dot_nt(qbd, kn) + a - cum, NEG_INF)

    lft = jnp.concatenate([r[...] for r in lft_refs], axis=0)
    sfx_in = _mask_dot(lft, ri >= ki)
    sfx_ex = sfx_in - lft
    later = jnp.zeros((FOX_HEADS, 1), F32)
    bias = [None] * n_pages
    for i in reversed(range(n_pages)):
        bias[i] = sfx_ex[i * FOX_HEADS:(i + 1) * FOX_HEADS, :] + later
        later = later + sfx_in[i * FOX_HEADS:(i + 1) * FOX_HEADS, 0:1]
    bias = _dot_mask(head_of_row, jnp.concatenate(bias, axis=1))
    s_old = jnp.concatenate(
        [_dot(qbd, r[...].reshape(d, page).astype(BF16)) for r in kt_refs], axis=1) + bias + a

    m = jnp.maximum(jnp.max(s_new, axis=1, keepdims=True), jnp.max(s_old, axis=1, keepdims=True))
    p_new = jnp.exp(s_new - m)
    p_old = jnp.exp(s_old - m)
    denom = jnp.sum(p_new, axis=1, keepdims=True) + jnp.sum(p_old, axis=1, keepdims=True)
    p_old = p_old.astype(BF16)
    acc = _dot(p_new.astype(BF16), vn)
    for i, r in enumerate(vt_refs):
        acc = acc + _dot_nt(p_old[:, i * page:(i + 1) * page], r[...].reshape(d, page).astype(BF16))
    o = jnp.where(own_cols, acc / denom, 0.0).reshape(FOX_HEADS, n_new, d)
    o_ref[0] = jnp.sum(o, axis=0).astype(BF16)


def _fox_attn_sample(q, k_new, v_new, lf_new, cache_kt, cache_vt, cache_lft, page_table):
    nb, n_new, d = q.shape
    n_pages = page_table.shape[1]
    page = cache_kt.shape[-1]

    def new_map(b, pt):
        return (b, 0, 0)

    def kv_spec(i):
        return pl.BlockSpec((None, FOX_HEADS, FOX_HEAD_DIM, page), lambda b, pt: (pt[b * n_pages + i], 0, 0, 0))

    def lf_spec(i):
        return pl.BlockSpec((None, FOX_HEADS, page), lambda b, pt: (pt[b * n_pages + i], 0, 0))

    pages = range(n_pages)
    grid_spec = pltpu.PrefetchScalarGridSpec(
        num_scalar_prefetch=1,
        grid=(nb,),
        in_specs=[pl.BlockSpec((1, n_new, d), new_map), pl.BlockSpec((1, n_new, d), new_map),
                  pl.BlockSpec((1, n_new, d), new_map), pl.BlockSpec((1, n_new, FOX_HEADS), new_map),
                  *[kv_spec(i) for i in pages], *[kv_spec(i) for i in pages], *[lf_spec(i) for i in pages]],
        out_specs=pl.BlockSpec((1, n_new, d), new_map),
    )
    return pl.pallas_call(
        functools.partial(_fox_decode_kernel, n_pages=n_pages),
        grid_spec=grid_spec,
        out_shape=jax.ShapeDtypeStruct((nb, n_new, d), BF16),
        compiler_params=_params("parallel"),
        name="fox_attn_sample",
    )(page_table.reshape(-1), q, k_new, v_new, lf_new,
      *([cache_kt] * n_pages), *([cache_vt] * n_pages), *([cache_lft] * n_pages))


def _ml_proj_kernel(x_ref, g_ref, wk_ref, wqvo_ref, wg_ref, bg_ref,
                    q_ref, k_ref, v_ref, o_ref, gt_ref, *, feature_major):
    d = x_ref.shape[-1]
    hq = ML_HEADS * ML_DQK
    h = _rms(x_ref[...], g_ref[...]).astype(BF16)
    k_ref[...] = (_dot(h, wk_ref[...]) * (ML_DQK ** -0.5)).astype(BF16)
    if feature_major:
        p = _dot_nt(wqvo_ref[...], h)
        q_ref[0] = p[:hq].astype(BF16)
        v_ref[0] = p[hq:hq + d].astype(BF16)
        o_ref[0] = jax.nn.sigmoid(p[hq + d:])
    else:
        p = _dot(h, wqvo_ref[...])
        q_ref[...] = p[:, :hq].astype(BF16)
        v_ref[...] = p[:, hq:hq + d].astype(BF16)
        o_ref[...] = jax.nn.sigmoid(p[:, hq + d:])
    gates = _dot(h, wg_ref[...]) + bg_ref[...]
    gates = GATE_CAP * jnp.tanh(gates / GATE_CAP)
    lane = lax.broadcasted_iota(jnp.int32, gates.shape, 1)
    gates = jnp.where(lane < ML_HEADS, gates, _log_sigmoid(gates))
    gt_ref[...] = gates[:, :2 * ML_HEADS]


def _ml_proj(x, g, wk, wqvo, wg, bg, seq_len, *, feature_major):
    m, d = x.shape
    hq = ML_HEADS * ML_DQK
    n_seq = m // seq_len
    nl = seq_len // ROW_TILE
    row = lambda b, l: (b * nl + l, 0)
    const = lambda b, l: (0, 0)

    def out(width, dtype):
        if feature_major:
            return (pl.BlockSpec((1, width, ROW_TILE), lambda b, l: (b, 0, l)),
                    jax.ShapeDtypeStruct((n_seq, width, seq_len), dtype))
        return pl.BlockSpec((ROW_TILE, width), row), jax.ShapeDtypeStruct((m, width), dtype)

    (q_spec, q_shape), (v_spec, v_shape), (o_spec, o_shape) = out(hq, BF16), out(d, BF16), out(d, F32)
    return pl.pallas_call(
        functools.partial(_ml_proj_kernel, feature_major=feature_major),
        grid=(n_seq, nl),
        in_specs=[pl.BlockSpec((ROW_TILE, d), row), pl.BlockSpec((1, d), const),
                  pl.BlockSpec(wk.shape, const), pl.BlockSpec(wqvo.shape, const),
                  pl.BlockSpec((d, LANES), const), pl.BlockSpec((1, LANES), const)],
        out_specs=(q_spec, pl.BlockSpec((ROW_TILE, hq), row), v_spec, o_spec,
                   pl.BlockSpec((ROW_TILE, 2 * ML_HEADS), row)),
        out_shape=(q_shape, jax.ShapeDtypeStruct((m, hq), BF16), v_shape, o_shape,
                   jax.ShapeDtypeStruct((m, 2 * ML_HEADS), F32)),
        compiler_params=_params("parallel", "parallel"),
        name="ml_proj_prompt" if feature_major else "ml_proj_sample",
    )(x, g, wk, wqvo, wg, bg)


def _ml_chunk_kernel(qt_ref, k_ref, vt_ref, ogt_ref, gt_ref, gtt_ref, gnb_ref,
                     a_ref, c_ref, n_ref, m_ref, ct_sc, n_sc, m_sc):
    t = k_ref.shape[1]

    @pl.when(pl.program_id(1) == 0)
    def _():
        ct_sc[...] = jnp.zeros_like(ct_sc)
        n_sc[...] = jnp.zeros_like(n_sc)
        m_sc[...] = jnp.zeros_like(m_sc)

    gates = gt_ref[0]
    gates_t = gtt_ref[0]
    n_g = 2 * ML_HEADS
    ri = lax.broadcasted_iota(jnp.int32, (t, t), 0)
    ci = lax.broadcasted_iota(jnp.int32, (t, t), 1)
    causal_t = ri <= ci
    b_cols = _dot_mask(ci <= ri, gates)
    b_rows = _mask_dot(gates_t, causal_t)
    w_cols = jnp.concatenate([gates[:, :ML_HEADS] - b_cols[:, ML_HEADS:], jnp.zeros((t, ML_HEADS), F32)],
                             axis=1)
    left = jnp.concatenate([p.astype(F32) for p in _split3(w_cols)]
                           + [jnp.ones((t, n_g), F32), jnp.zeros((t, LANES - 4 * n_g), F32)],
                           axis=1).astype(BF16)
    right_pad = jnp.zeros((LANES - 4 * n_g, t), F32)
    sub = lax.broadcasted_iota(jnp.int32, (n_g, t), 0)
    sub8 = lax.broadcasted_iota(jnp.int32, (SUBLANES, t), 0)
    lane = lax.broadcasted_iota(jnp.int32, (t, LANES), 1)
    half = (lane < ML_DQK, lane >= ML_DQK)
    row_h = lax.broadcasted_iota(jnp.int32, (LANES, t), 0)
    half_rows = (row_h < ML_DQK, row_h >= ML_DQK)
    lane1 = lax.broadcasted_iota(jnp.int32, (1, LANES), 1)
    head_lane = lax.broadcasted_iota(jnp.int32, (1, ML_HEADS), 1)
    m_all = m_sc[...]
    m_out = jnp.zeros((1, ML_HEADS), F32)

    heads = range(ML_HEADS)
    pairs = range(ML_HEADS // 2)
    pj = lambda h: slice((h // 2) * LANES, (h // 2 + 1) * LANES)
    dv = lambda h: slice(h * ML_DV, (h + 1) * ML_DV)
    i_r = [gates_t[h:h + 1, :] for h in heads]
    b_r = [b_rows[ML_HEADS + h:ML_HEADS + h + 1, :] for h in heads]
    m_prev = [m_all[:, h:h + 1] for h in heads]
    ct = [ct_sc[j] for j in pairs]
    n2 = [n_sc[j] for j in pairs]
    km = [jnp.where(half[h % 2], k_ref[0, :, pj(h)], jnp.zeros((t, LANES), BF16)) for h in heads]

    logw, kq, cq, qn = [], [], [], []
    for h in heads:
        qt2 = qt_ref[0, pj(h), :]
        qtm = jnp.where(half_rows[h % 2], qt2, jnp.zeros_like(qt2))
        u_hi, u_mid, u_lo = (p.astype(F32) for p in _split3(b_r[h]))
        tail = jnp.where(sub == 0, u_hi, jnp.where(sub == 1, u_mid, jnp.where(sub == 2, u_lo, 0.0)))
        pick = (sub == h).astype(F32)
        right = jnp.concatenate([pick, pick, pick, tail, right_pad], axis=0).astype(BF16)
        logw.append(_dot(left, right))
        kq.append(_dot(km[h], qt2))
        cq.append(_dot(ct[h // 2].astype(BF16), qtm))
        qn.append(_dot(n2[h // 2].astype(BF16), qtm)[0:1, :])

    m_t, w_inter, a, a_sum, vtw, wk8, decay = [], [], [], [], [], [], []
    for h in heads:
        lw = jnp.where(causal_t, logw[h], NEG_INF)
        inter = b_r[h] + m_prev[h]
        m_t.append(jnp.maximum(inter, jnp.max(lw, axis=0, keepdims=True)))
        w_inter.append(jnp.exp(inter - m_t[h]))
        a_h = jnp.exp(lw - m_t[h]) * kq[h]
        a_sum.append(jnp.sum(a_h, axis=0, keepdims=True))
        a.append(a_h.astype(BF16))
        b_last = b_r[h][:, t - 1:t]
        m_new = m_t[h][:, t - 1:t]
        decay.append(jnp.exp(b_last + m_prev[h] - m_new))
        wk = jnp.exp(b_last - b_r[h] + i_r[h] - m_new)
        vtw.append((vt_ref[0, dv(h), :].astype(F32) * wk).astype(BF16))
        wk8.append(jnp.where(sub8 == 0, wk, 0.0).astype(BF16))
        m_out = jnp.where(head_lane == h, m_new, m_out)

    va = [_dot(vt_ref[0, dv(h), :], a[h]) for h in heads]
    c_add = [_dot(vtw[h], km[h]) for h in heads]
    n_add = [_dot(wk8[h], km[h]) for h in heads]

    for h in heads:
        hu = w_inter[h] * cq[h] + va[h]
        den = w_inter[h] * qn[h] + a_sum[h]
        r1 = 1.0 / jnp.maximum(jnp.abs(den), jnp.exp(-m_t[h]))
        ms = jnp.mean(hu * hu, axis=0, keepdims=True)
        hn = hu * (r1 * lax.rsqrt(r1 * r1 * ms + EPS)) * _lane_tile(gnb_ref[dv(h), :], t)
        a_ref[0, :, dv(h)] = (ogt_ref[0, dv(h), :] * hn).T.astype(BF16)
    for j in pairs:
        d2 = jnp.where(lane1 < ML_DQK, decay[2 * j], decay[2 * j + 1])
        ct_sc[j] = d2 * ct[j] + c_add[2 * j] + c_add[2 * j + 1]
        n_sc[j] = d2 * n2[j] + n_add[2 * j] + n_add[2 * j + 1]
    m_sc[...] = m_out

    @pl.when(pl.program_id(1) == pl.num_programs(1) - 1)
    def _():
        for j in range(ML_HEADS // 2):
            c_ref[0, j] = ct_sc[j].T
            n_ref[0, j:j + 1, :] = n_sc[j][0:1, :]
        m_ref[0] = m_out


def _ml_step_kernel(q_ref, k_ref, v_ref, og_ref, gt_ref, gtt_ref, gn_ref, c0_ref, n0_ref, m0_ref,
                    a_ref, c_ref, n_ref, m_ref):
    g, t = q_ref.shape[0], q_ref.shape[1]
    gates = gt_ref[...]
    gates_t = gtt_ref[...]
    ri = lax.broadcasted_iota(jnp.int32, (g, t, t), 1)
    ci = lax.broadcasted_iota(jnp.int32, (g, t, t), 2)
    causal = ci <= ri
    eye = jnp.broadcast_to(
        (lax.broadcasted_iota(jnp.int32, (ML_DQK, ML_DQK), 0)
         == lax.broadcasted_iota(jnp.int32, (ML_DQK, ML_DQK), 1)).astype(BF16)[None], (g, ML_DQK, ML_DQK))
    head_lane = lax.broadcasted_iota(jnp.int32, (g, 1, ML_HEADS), 2)
    m_all = m0_ref[...]
    m_out = jnp.zeros((g, 1, ML_HEADS), F32)

    for h in range(ML_HEADS):
        sl = slice(h * ML_DV, (h + 1) * ML_DV)
        sq = slice(h * ML_DQK, (h + 1) * ML_DQK)
        fh = ML_HEADS + h
        q_h = q_ref[:, :, sq]
        k_h = k_ref[:, :, sq]
        v_h = v_ref[:, :, sl]
        c_h = c0_ref[:, h]
        n_h = n0_ref[:, h:h + 1, :]
        i_c, f_c = gates[:, :, h:h + 1], gates[:, :, fh:fh + 1]
        i_r, f_r = gates_t[:, h:h + 1, :], gates_t[:, fh:fh + 1, :]
        m_prev = m_all[:, :, h:h + 1]
        b_c = jnp.sum(jnp.where(causal, f_r, 0.0), axis=2, keepdims=True)
        b_r = jnp.sum(jnp.where(ri <= ci, f_c, 0.0), axis=1, keepdims=True)
        logw = jnp.where(causal, b_c - b_r + i_r, NEG_INF)
        inter = b_c + m_prev
        m_t = jnp.maximum(inter, jnp.max(logw, axis=2, keepdims=True))
        w_inter = jnp.exp(inter - m_t)
        a = jnp.exp(logw - m_t) * jnp.einsum("gtd,gsd->gts", q_h, k_h, preferred_element_type=F32)
        num = (w_inter * jnp.einsum("gtd,gde->gte", q_h, c_h.astype(BF16), preferred_element_type=F32)
               + jnp.einsum("gts,gse->gte", a.astype(BF16), v_h, preferred_element_type=F32))
        qn = jnp.sum(q_h.astype(F32) * n_h, axis=2, keepdims=True)
        den = w_inter * qn + jnp.sum(a, axis=2, keepdims=True)
        hs = num / jnp.maximum(jnp.abs(den), jnp.exp(-m_t))
        hn = hs * lax.rsqrt(jnp.mean(hs * hs, axis=2, keepdims=True) + EPS) * gn_ref[:, sl]
        a_ref[:, :, sl] = (og_ref[:, :, sl] * hn).astype(BF16)
        b_last = b_c[:, t - 1:t, :]
        m_new = m_t[:, t - 1:t, :]
        decay = jnp.exp(b_last + m_prev - m_new)
        kw = jnp.exp(b_last - b_c + i_c - m_new) * k_h.astype(F32)
        kt_h = jnp.einsum("gdk,gsk->gds", eye, k_h, preferred_element_type=F32)
        kwt = (kt_h * jnp.exp(b_last - b_r + i_r - m_new)).astype(BF16)
        c_ref[:, h] = decay * c_h + jnp.einsum("gds,gse->gde", kwt, v_h, preferred_element_type=F32)
        n_ref[:, h:h + 1, :] = decay * n_h + jnp.sum(kw, axis=1, keepdims=True)
        m_out = jnp.where(head_lane == h, m_new, m_out)
    m_ref[...] = m_out


def _ml_chunk(qt, k, vt, ogt, gates, gates_t, gnb, chunk):
    nb, d, seq_len = vt.shape
    hq = k.shape[2]
    pairs = ML_HEADS // 2
    tok = lambda width: pl.BlockSpec((1, chunk, width), lambda b, c: (b, c, 0))
    tok_t = lambda width: pl.BlockSpec((1, width, chunk), lambda b, c: (b, 0, c))
    return pl.pallas_call(
        _ml_chunk_kernel,
        grid=(nb, seq_len // chunk),
        in_specs=[tok_t(hq), tok(hq), tok_t(d), tok_t(d), tok(2 * ML_HEADS), tok_t(2 * ML_HEADS),
                  pl.BlockSpec((d, LANES), lambda b, c: (0, 0))],
        out_specs=(tok(d), pl.BlockSpec((1, pairs, LANES, LANES), lambda b, c: (b, 0, 0, 0)),
                   pl.BlockSpec((1, pairs, LANES), lambda b, c: (b, 0, 0)),
                   pl.BlockSpec((1, 1, ML_HEADS), lambda b, c: (b, 0, 0))),
        out_shape=(jax.ShapeDtypeStruct((nb, seq_len, d), BF16),
                   jax.ShapeDtypeStruct((nb, pairs, LANES, LANES), F32),
                   jax.ShapeDtypeStruct((nb, pairs, LANES), F32),
                   jax.ShapeDtypeStruct((nb, 1, ML_HEADS), F32)),
        scratch_shapes=[pltpu.VMEM((pairs, ML_DV, LANES), F32), pltpu.VMEM((pairs, SUBLANES, LANES), F32),
                        pltpu.VMEM((1, ML_HEADS), F32)],
        compiler_params=_params("parallel", "arbitrary"),
        name="ml_chunk",
    )(qt, k, vt, ogt, gates, gates_t, gnb)


def _ml_step(q, k, v, og, gates, gates_t, gn, c0, n0, m0):
    nb, t, d = v.shape
    hq = q.shape[2]
    g = ML_STEP_SEQS
    lead = lambda *rest: pl.BlockSpec((g, *rest), lambda i: (i,) + (0,) * len(rest))
    state = (lead(ML_HEADS, ML_DQK, ML_DV), lead(ML_HEADS, ML_DQK), lead(1, ML_HEADS))
    return pl.pallas_call(
        _ml_step_kernel,
        grid=(nb // g,),
        in_specs=[lead(t, hq), lead(t, hq), lead(t, d), lead(t, d), lead(t, 2 * ML_HEADS),
                  lead(2 * ML_HEADS, t), pl.BlockSpec((1, d), lambda i: (0, 0)), *state],
        out_specs=(lead(t, d), *state),
        out_shape=(jax.ShapeDtypeStruct((nb, t, d), BF16), jax.ShapeDtypeStruct(c0.shape, F32),
                   jax.ShapeDtypeStruct(n0.shape, F32), jax.ShapeDtypeStruct(m0.shape, F32)),
        compiler_params=_params("parallel"),
        name="ml_step",
    )(q, k, v, og, gates, gates_t, gn, c0, n0, m0)


def _pad_cols(w, width):
    return jnp.pad(w, ((0, 0), (0, width - w.shape[1])))


def kernel(x_prompt, x_sample, state_conv, cache_k, cache_v, cache_logf, page_table, state_C, state_n, state_m, norm_mix, norm_ffn, norm_final, w_conv_in, w_conv, w_conv_out, w_fox_in, b_fox_f, w_fox_out, w_ml_in, b_ml_gates, g_ml_norm, w_ml_out, w_ffn_gu, w_ffn_down):
    bp, lp, d = x_prompt.shape
    bs, ls, _ = x_sample.shape
    depth = norm_mix.shape[0]
    dff = w_ffn_down.shape[1]
    assert d == D_MODEL and ls == SUBLANES
    xp = x_prompt.reshape(bp * lp, d)
    xs = x_sample.reshape(bs * ls, d)
    gf = norm_final.reshape(1, d)
    hq = ML_HEADS * ML_DQK
    pairs = ML_HEADS // 2
    out = {name: [] for name in ("conv_p", "conv_s", "kp", "vp", "lfp", "ks", "vs", "lfs",
                                 "cp", "np", "mp", "cs", "ns", "ms")}

    for i in range(depth):
        kind, j = i % 3, i // 3
        gm = norm_mix[i].reshape(1, d)
        if kind == 0:
            win = w_conv_in[j].astype(BF16)
            wo = w_conv_out[j].astype(BF16)
            ap, sp = _conv_mixer(xp, gm, win, w_conv[j], jnp.zeros((bp, CONV_W - 1, d), F32), lp)
            as_, ss = _conv_mixer(xs, gm, win, w_conv[j], state_conv[j], ls)
            out["conv_p"].append(sp)
            out["conv_s"].append(ss)
        elif kind == 1:
            wqkv = w_fox_in[j][:, :3 * d].astype(BF16)
            wf = _pad_cols(w_fox_in[j][:, 3 * d:], LANES).astype(BF16)
            bf = _pad_cols(b_fox_f[j].reshape(1, FOX_HEADS), LANES)
            wo = w_fox_out[j].astype(BF16)
            qtb, kt, vt, kb, vtb, lft, c, ct = _fox_proj_prompt(
                xp, gm, wqkv[:, d:2 * d], wqkv.T, wf, wf[:, :FOX_HEADS].T, bf,
                b_fox_f[j].reshape(FOX_HEADS, 1), lp)
            ap = _fox_attn_prompt(qtb, kb, vtb, c, ct, lp)
            heads_last = lambda t: jnp.transpose(t.reshape(bp, FOX_HEADS, FOX_HEAD_DIM, lp), (0, 3, 1, 2))
            out["kp"].append(heads_last(kt))
            out["vp"].append(heads_last(vt))
            out["lfp"].append(jnp.swapaxes(lft, 1, 2))
            q, k, v, lf = _fox_proj_sample(xs, gm, wqkv, wf, bf)
            as_ = _fox_attn_sample(
                q.reshape(bs, ls, d), k.reshape(bs, ls, d), v.reshape(bs, ls, d),
                lf.reshape(bs, ls, FOX_HEADS), jnp.transpose(cache_k[j], (0, 2, 3, 1)),
                jnp.transpose(cache_v[j], (0, 2, 3, 1)), jnp.transpose(cache_logf[j], (0, 2, 1)),
                page_table).reshape(bs * ls, d)
            out["ks"].append(k.reshape(bs, ls, FOX_HEADS, FOX_HEAD_DIM))
            out["vs"].append(v.reshape(bs, ls, FOX_HEADS, FOX_HEAD_DIM))
            out["lfs"].append(lf.reshape(bs, ls, FOX_HEADS))
        else:
            w = w_ml_in[j]
            wk = w[:, hq:2 * hq].astype(BF16)
            wqvo = jnp.concatenate([w[:, :hq], w[:, 2 * hq:2 * hq + 2 * d]], axis=1).astype(BF16)
            wg = _pad_cols(w[:, 2 * hq + 2 * d:], LANES).astype(BF16)
            bg = _pad_cols(b_ml_gates[j].reshape(1, 2 * ML_HEADS), LANES)
            gn = g_ml_norm[j].reshape(1, d)
            wo = w_ml_out[j].astype(BF16)

            def tokens(nb, seq_len, *arrays):
                return [t.reshape(nb, seq_len, t.shape[-1]) for t in arrays]

            qt, k, vt, ogt, gates = _ml_proj(xp, gm, wk, wqvo.T, wg, bg, lp, feature_major=True)
            k, gates = tokens(bp, lp, k, gates)
            a, c1, n1, m1 = _ml_chunk(qt, k, vt, ogt, gates, jnp.swapaxes(gates, 1, 2),
                                      jnp.broadcast_to(gn.reshape(d, 1), (d, LANES)), ML_CHUNK_PROMPT)
            ap = a.reshape(bp * lp, d)
            out["cp"].append(c1.reshape(bp, ML_HEADS, ML_DQK, ML_DV))
            out["np"].append(n1.reshape(bp, ML_HEADS, ML_DQK))
            out["mp"].append(m1.reshape(bp, ML_HEADS))

            q, k, v, og, gates = _ml_proj(xs, gm, wk, wqvo, wg, bg, ROW_TILE, feature_major=False)
            q, k, v, og, gates = tokens(bs, ls, q, k, v, og, gates)
            a, c1, n1, m1 = _ml_step(q, k, v, og, gates, jnp.swapaxes(gates, 1, 2), gn,
                                     state_C[j], state_n[j], state_m[j].reshape(bs, 1, ML_HEADS))
            as_ = a.reshape(bs * ls, d)
            out["cs"].append(c1); out["ns"].append(n1); out["ms"].append(m1.reshape(bs, ML_HEADS))

        gffn = norm_ffn[i].reshape(1, d)
        wg_ffn = w_ffn_gu[i][:, :dff].astype(BF16)
        wu_ffn = w_ffn_gu[i][:, dff:].astype(BF16)
        wd_ffn = w_ffn_down[i].astype(BF16)
        last = i == depth - 1
        xp = _post(xp, ap, wo, gffn, wg_ffn, wu_ffn, wd_ffn, gf, final_norm=last)
        xs = _post(xs, as_, wo, gffn, wg_ffn, wu_ffn, wd_ffn, gf, final_norm=last)

    st = lambda name: jnp.stack(out[name])
    return (xp.reshape(bp, lp, d), xs.reshape(bs, ls, d), st("conv_p"), st("conv_s"),
            st("kp"), st("vp"), st("lfp"), st("ks"), st("vs"), st("lfs"),
            st("cp"), st("np"), st("mp"), st("cs"), st("ns"), st("ms"))
```

```python
import functools

import jax
import jax.numpy as jnp
from jax import lax
from jax.experimental import pallas as pl
from jax.experimental.pallas import tpu as pltpu

F32 = jnp.float32
BF16 = jnp.bfloat16
HIGHEST = lax.Precision.HIGHEST

D_MODEL = 1024
CONV_W = 3
FOX_HEADS = 16
FOX_HEAD_DIM = D_MODEL // FOX_HEADS
ML_HEADS = 8
ML_DV = D_MODEL // ML_HEADS
ML_DQK = ML_DV // 2
GATE_CAP = 15.0
EPS = 1e-6
NEG_INF = -1e30
LOG2E = 1.4426950408889634

LANES = 128
SUBLANES = 8
ROW_TILE = 512
POST_ROW_TILE = 1024
FFN_CHUNK = 256
ATTN_TQ = 512
ML_CHUNK_PROMPT = 128
ML_STEP_SEQS = 16
VMEM_LIMIT = 56 * 1024 * 1024


def _params(*sem):
    return pltpu.CompilerParams(dimension_semantics=sem, vmem_limit_bytes=VMEM_LIMIT)


def _rms(x, g):
    return x * lax.rsqrt(jnp.mean(x * x, axis=-1, keepdims=True) + EPS) * g


def _log_sigmoid(x):
    return jnp.minimum(x, 0.0) - jnp.log1p(jnp.exp(-jnp.abs(x)))


def _dot(a, b):
    return jnp.dot(a, b, preferred_element_type=F32)


def _dot_hi(a, b):
    return jnp.dot(a, b, precision=HIGHEST, preferred_element_type=F32)


def _dot_nt(a, b, precision=None):
    return lax.dot_general(a, b, (((1,), (1,)), ((), ())), precision=precision,
                           preferred_element_type=F32)


def _dot_tn(a, b):
    return lax.dot_general(a, b, (((0,), (0,)), ((), ())), preferred_element_type=F32)


def _lane_tile(x, width):
    return jnp.tile(x, (1, width // LANES))


def _post_kernel(x_ref, a_ref, wo_ref, g_ref, wg_ref, wu_ref, wd_ref, gf_ref, o_ref, *, final_norm):
    x1 = x_ref[...] + _dot(a_ref[...], wo_ref[...])
    h = _rms(x1, g_ref[...]).astype(BF16)
    y = x1
    for c in range(0, wg_ref.shape[1], FFN_CHUNK):
        g = _dot(h, wg_ref[:, c:c + FFN_CHUNK])
        u = _dot(h, wu_ref[:, c:c + FFN_CHUNK])
        y = y + _dot(((g * jax.nn.sigmoid(g)) * u).astype(BF16), wd_ref[c:c + FFN_CHUNK, :])
    o_ref[...] = _rms(y, gf_ref[...]) if final_norm else y


def _post(x, a, wo, g, wg, wu, wd, gf, *, final_norm):
    m, d = x.shape
    dff = wg.shape[1]
    tm = min(POST_ROW_TILE, m)
    assert dff % FFN_CHUNK == 0 and m % tm == 0
    row = pl.BlockSpec((tm, d), lambda i: (i, 0))
    resident = lambda shape: pl.BlockSpec(shape, lambda i: (0, 0), pipeline_mode=pl.Buffered(1))
    return pl.pallas_call(
        functools.partial(_post_kernel, final_norm=final_norm),
        grid=(m // tm,),
        in_specs=[row, row, resident((d, d)), resident((1, d)), resident((d, dff)), resident((d, dff)),
                  resident((dff, d)), resident((1, d))],
        out_specs=row,
        out_shape=jax.ShapeDtypeStruct((m, d), F32),
        compiler_params=_params("parallel"),
        name="post_ffn",
    )(x, a, wo, g, wg, wu, wd, gf)


def _conv_front(x_ref, g_ref, win_ref):
    d = x_ref.shape[-1]
    h = _rms(x_ref[...], g_ref[...]).astype(BF16)
    p = _dot(h, win_ref[...])
    return p[:, :d], p[:, d:2 * d] * p[:, 2 * d:]


def _conv_taps(u, rig, p0, p1, wc_ref):
    s1 = jnp.where(rig == 0, p1, pltpu.roll(u, 1, axis=0))
    s2 = jnp.where(rig == 0, p0, jnp.where(rig == 1, p1, pltpu.roll(u, 2, axis=0)))
    return wc_ref[0:1, :] * s2 + wc_ref[1:2, :] * s1 + wc_ref[2:3, :] * u


def _conv_prompt_kernel(x_ref, g_ref, win_ref, wc_ref, prev_ref, z_ref, st_ref, carry_sc):
    @pl.when(pl.program_id(1) == 0)
    def _():
        carry_sc[...] = prev_ref[0]

    bg, u = _conv_front(x_ref, g_ref, win_ref)
    r = u.shape[0]
    rig = lax.broadcasted_iota(jnp.int32, u.shape, 0)
    conv = _conv_taps(u, rig, carry_sc[0:1, :], carry_sc[1:2, :], wc_ref)
    z_ref[...] = (bg * conv).astype(BF16)
    tail = u[r - (CONV_W - 1):, :]
    carry_sc[...] = tail
    st_ref[0] = tail


def _conv_sample_kernel(x_ref, g_ref, win_ref, wc_ref, prev_ref, z_ref, st_ref):
    bg, u = _conv_front(x_ref, g_ref, win_ref)
    r, d = u.shape
    n_seq = r // SUBLANES
    prev = prev_ref[...]
    p0 = jnp.broadcast_to(prev[:, 0:1, :], (n_seq, SUBLANES, d)).reshape(r, d)
    p1 = jnp.broadcast_to(prev[:, 1:2, :], (n_seq, SUBLANES, d)).reshape(r, d)
    rig = lax.broadcasted_iota(jnp.int32, u.shape, 0) & (SUBLANES - 1)
    conv = _conv_taps(u, rig, p0, p1, wc_ref)
    z_ref[...] = (bg * conv).astype(BF16)
    st_ref[...] = u.reshape(n_seq, SUBLANES, d)[:, SUBLANES - (CONV_W - 1):, :]


def _conv_mixer(x, g, win, wc, prev, seq_len):
    m, d = x.shape
    n_seq = m // seq_len
    common_in = [pl.BlockSpec((d, 3 * d), lambda *_: (0, 0)), pl.BlockSpec((CONV_W, d), lambda *_: (0, 0))]
    out_shape = (jax.ShapeDtypeStruct((m, d), BF16), jax.ShapeDtypeStruct((n_seq, CONV_W - 1, d), F32))
    if seq_len == SUBLANES:
        per = ROW_TILE // SUBLANES
        return pl.pallas_call(
            _conv_sample_kernel,
            grid=(m // ROW_TILE,),
            in_specs=[pl.BlockSpec((ROW_TILE, d), lambda i: (i, 0)), pl.BlockSpec((1, d), lambda i: (0, 0)),
                      *common_in, pl.BlockSpec((per, CONV_W - 1, d), lambda i: (i, 0, 0))],
            out_specs=(pl.BlockSpec((ROW_TILE, d), lambda i: (i, 0)),
                       pl.BlockSpec((per, CONV_W - 1, d), lambda i: (i, 0, 0))),
            out_shape=out_shape,
            compiler_params=_params("parallel"),
            name="conv_sample",
        )(x, g, win, wc, prev)
    assert seq_len % ROW_TILE == 0
    nl = seq_len // ROW_TILE
    return pl.pallas_call(
        _conv_prompt_kernel,
        grid=(n_seq, nl),
        in_specs=[pl.BlockSpec((ROW_TILE, d), lambda b, l: (b * nl + l, 0)),
                  pl.BlockSpec((1, d), lambda b, l: (0, 0)),
                  *common_in, pl.BlockSpec((1, CONV_W - 1, d), lambda b, l: (b, 0, 0))],
        out_specs=(pl.BlockSpec((ROW_TILE, d), lambda b, l: (b * nl + l, 0)),
                   pl.BlockSpec((1, CONV_W - 1, d), lambda b, l: (b, 0, 0))),
        out_shape=out_shape,
        scratch_shapes=[pltpu.VMEM((CONV_W - 1, d), F32)],
        compiler_params=_params("parallel", "arbitrary"),
        name="conv_prompt",
    )(x, g, win, wc, prev)


def _fox_proj_kernel(x_ref, g_ref, wqkv_ref, wf_ref, bf_ref, q_ref, k_ref, v_ref, lf_ref):
    d = x_ref.shape[-1]
    h = _rms(x_ref[...], g_ref[...]).astype(BF16)
    p = _dot(h, wqkv_ref[...])
    logf = _log_sigmoid(_dot(h, wf_ref[...]) + bf_ref[...])
    q_ref[...] = p[:, :d] * (FOX_HEAD_DIM ** -0.5)
    k_ref[...] = p[:, d:2 * d]
    v_ref[...] = p[:, 2 * d:]
    lf_ref[...] = logf[:, :FOX_HEADS]


def _fox_proj_sample(x, g, wqkv, wf, bf):
    m, d = x.shape
    row = lambda i: (i, 0)
    const = lambda i: (0, 0)
    big = pl.BlockSpec((ROW_TILE, d), row)
    f32_out = jax.ShapeDtypeStruct((m, d), F32)
    return pl.pallas_call(
        _fox_proj_kernel,
        grid=(m // ROW_TILE,),
        in_specs=[big, pl.BlockSpec((1, d), const), pl.BlockSpec((d, 3 * d), const),
                  pl.BlockSpec((d, LANES), const), pl.BlockSpec((1, LANES), const)],
        out_specs=(big, big, big, pl.BlockSpec((ROW_TILE, FOX_HEADS), row)),
        out_shape=(f32_out, f32_out, f32_out, jax.ShapeDtypeStruct((m, FOX_HEADS), F32)),
        compiler_params=_params("parallel"),
        name="fox_proj_sample",
    )(x, g, wqkv, wf, bf)


def _fox_proj_prompt_kernel(x_ref, g_ref, wk_ref, wqkvt_ref, wf_ref, wft_ref, bf_ref, bft_ref,
                            qtb_ref, kt_ref, vt_ref, kb_ref, vtb_ref, lft_ref, c_ref, ct_ref,
                            carry_sc, carryt_sc):
    d = x_ref.shape[-1]
    r = x_ref.shape[0]
    h = _rms(x_ref[...], g_ref[...]).astype(BF16)
    kb_ref[...] = _dot(h, wk_ref[...]).astype(BF16)
    qkv = _dot_nt(wqkvt_ref[...], h)
    qtb_ref[0] = (qkv[:d] * (FOX_HEAD_DIM ** -0.5 * LOG2E)).astype(BF16)
    kt_ref[0] = qkv[d:2 * d]
    vt_ref[0] = qkv[2 * d:]
    vtb_ref[0] = qkv[2 * d:].astype(BF16)
    logf = _log_sigmoid(_dot(h, wf_ref[...]) + bf_ref[...])
    logf_t = _log_sigmoid(_dot_nt(wft_ref[...], h) + bft_ref[...])
    lft_ref[0] = logf_t

    @pl.when(pl.program_id(1) == 0)
    def _():
        carry_sc[...] = jnp.zeros_like(carry_sc)
        carryt_sc[...] = jnp.zeros_like(carryt_sc)

    ri = lax.broadcasted_iota(jnp.int32, (r, r), 0)
    ci = lax.broadcasted_iota(jnp.int32, (r, r), 1)
    c = carry_sc[...] + _dot_mask(ci <= ri, logf)
    carry_sc[...] = c[r - 1:r, :]
    c_ref[...] = c[:, :FOX_HEADS]
    ct = _lane_tile(carryt_sc[...], r) + _mask_dot(logf_t, ri <= ci)
    carryt_sc[...] = jnp.broadcast_to(ct[:, r - 1:r], carryt_sc.shape)
    ct_ref[0] = ct


def _fox_proj_prompt(x, g, wk, wqkvt, wf, wft, bf, bft, seq_len):
    m, d = x.shape
    n_seq = m // seq_len
    nl = seq_len // ROW_TILE
    row = lambda b, l: (b * nl + l, 0)
    const = lambda b, l: (0, 0)
    feat = lambda width: pl.BlockSpec((1, width, ROW_TILE), lambda b, l: (b, 0, l))
    feat_shape = lambda width, dtype: jax.ShapeDtypeStruct((n_seq, width, seq_len), dtype)
    return pl.pallas_call(
        _fox_proj_prompt_kernel,
        grid=(n_seq, nl),
        in_specs=[pl.BlockSpec((ROW_TILE, d), row), pl.BlockSpec((1, d), const),
                  pl.BlockSpec((d, d), const), pl.BlockSpec((3 * d, d), const),
                  pl.BlockSpec((d, LANES), const), pl.BlockSpec((FOX_HEADS, d), const),
                  pl.BlockSpec((1, LANES), const), pl.BlockSpec((FOX_HEADS, 1), const)],
        out_specs=(feat(d), feat(d), feat(d), pl.BlockSpec((ROW_TILE, d), row), feat(d), feat(FOX_HEADS),
                   pl.BlockSpec((ROW_TILE, FOX_HEADS), row), feat(FOX_HEADS)),
        out_shape=(feat_shape(d, BF16), feat_shape(d, F32), feat_shape(d, F32),
                   jax.ShapeDtypeStruct((m, d), BF16), feat_shape(d, BF16), feat_shape(FOX_HEADS, F32),
                   jax.ShapeDtypeStruct((m, FOX_HEADS), F32), feat_shape(FOX_HEADS, F32)),
        scratch_shapes=[pltpu.VMEM((1, LANES), F32), pltpu.VMEM((FOX_HEADS, LANES), F32)],
        compiler_params=_params("parallel", "arbitrary"),
        name="fox_proj_prompt",
    )(x, g, wk, wqkvt, wf, wft, bf, bft)


def _fox_attn_kernel(qt_ref, k_ref, vt_ref, c_ref, ct_ref, o_ref, kaug_sc, vaug_sc, m_sc, acc_sc, *, tq):
    hp = pl.program_id(1)
    qi = pl.program_id(2)
    seq = k_ref.shape[0]
    hd = FOX_HEAD_DIM
    n_aug = 2 * SUBLANES
    one_row = jnp.where(lax.broadcasted_iota(jnp.int32, (n_aug, seq), 0) == 0, 1.0, 0.0).astype(BF16)

    @pl.when(qi == 0)
    def _():
        for hh in range(2):
            vaug_sc[hh] = jnp.concatenate([vt_ref[0, hh * hd:(hh + 1) * hd, :], one_row], axis=0)
        c3 = jnp.concatenate([p.astype(F32) for p in _split3(c_ref[...] * LOG2E)]
                             + [jnp.zeros((seq, LANES - 3 * FOX_HEADS), F32)], axis=1).astype(BF16)
        src = lax.broadcasted_iota(jnp.int32, (LANES, LANES), 0)
        dst = lax.broadcasted_iota(jnp.int32, (LANES, LANES), 1)
        term = src >> (FOX_HEADS.bit_length() - 1)
        lane = lax.broadcasted_iota(jnp.int32, (seq, LANES), 1)
        k2 = k_ref[...]
        for hh in range(2):
            off = hd * (1 - hh)
            place = jnp.where((src & (FOX_HEADS - 1)) == 2 * hp + hh,
                              jnp.where(dst == off + term, -1.0, 0.0), 0.0).astype(BF16)
            ones = jnp.where(lane >= off + 3, jnp.where(lane < off + 6, 1.0, 0.0), 0.0)
            aug = (_dot(c3, place) + ones).astype(BF16)
            own = (lane >= hd) if hh else (lane < hd)
            kaug_sc[hh] = jnp.where(own, k2, aug)

    qt2 = qt_ref[0]
    sub = lax.broadcasted_iota(jnp.int32, (n_aug, tq), 0)
    qa = []
    for hh in range(2):
        c_hi, c_mid, c_lo = (p.astype(F32) for p in _split3(ct_ref[0, pl.ds(2 * hp + hh, 1), :] * LOG2E))
        piece = jnp.where(sub < 3, 1.0, jnp.where(sub == 3, c_hi, jnp.where(sub == 4, c_mid,
                          jnp.where(sub == 5, c_lo, 0.0)))).astype(BF16)
        rest = jnp.zeros((hd - n_aug, tq), BF16)
        qa.append(jnp.concatenate([qt2[:hd], piece, rest] if hh == 0 else [piece, rest, qt2[hd:]], axis=0))
        m_sc[hh] = jnp.full((1, tq), NEG_INF, F32)
    acc_sc[...] = jnp.zeros_like(acc_sc)

    visible = (lax.broadcasted_iota(jnp.int32, (tq, tq), 0)
               <= lax.broadcasted_iota(jnp.int32, (tq, tq), 1))

    def scores(ki):
        return [_dot(kaug_sc[hh, ki * tq:(ki + 1) * tq, :], qa[hh]) for hh in range(2)]

    def absorb(ki, s2, diagonal):
        p2, alpha2 = [], []
        for hh in range(2):
            s = jnp.where(visible, s2[hh], NEG_INF) if diagonal else s2[hh]
            m_prev = m_sc[hh]
            m_new = jnp.maximum(m_prev, jnp.max(s, axis=0, keepdims=True))
            alpha2.append(jnp.exp2(m_prev - m_new))
            p2.append(jnp.exp2(s - m_new).astype(BF16))
            m_sc[hh] = m_new
        pv2 = [_dot(vaug_sc[hh, :, ki * tq:(ki + 1) * tq], p2[hh]) for hh in range(2)]
        for hh in range(2):
            acc_sc[hh] = alpha2[hh] * acc_sc[hh] + pv2[hh]

    for n in range(seq // tq):
        @pl.when(qi == n)
        def _():
            s_next = scores(0)
            for ki in range(n + 1):
                s_cur = s_next
                if ki < n:
                    s_next = scores(ki + 1)
                absorb(ki, s_cur, ki == n)

    o_t = jnp.concatenate([acc_sc[hh, :hd, :] / acc_sc[hh, hd:hd + 1, :] for hh in range(2)], axis=0)
    o_ref[...] = o_t.T.astype(BF16)


def _fox_attn_prompt(qtb, kb, vtb, c, ct, seq_len):
    n_seq, d, _ = qtb.shape
    tq = ATTN_TQ
    nq = seq_len // tq
    return pl.pallas_call(
        functools.partial(_fox_attn_kernel, tq=tq),
        grid=(n_seq, d // LANES, nq),
        in_specs=[pl.BlockSpec((1, LANES, tq), lambda b, hp, qi: (b, hp, qi)),
                  pl.BlockSpec((seq_len, LANES), lambda b, hp, qi: (b, hp)),
                  pl.BlockSpec((1, LANES, seq_len), lambda b, hp, qi: (b, hp, 0)),
                  pl.BlockSpec((seq_len, FOX_HEADS), lambda b, hp, qi: (b, 0)),
                  pl.BlockSpec((1, FOX_HEADS, tq), lambda b, hp, qi: (b, 0, qi))],
        out_specs=pl.BlockSpec((tq, LANES), lambda b, hp, qi: (b * nq + qi, hp)),
        out_shape=jax.ShapeDtypeStruct((n_seq * seq_len, d), BF16),
        scratch_shapes=[pltpu.VMEM((2, seq_len, LANES), BF16),
                        pltpu.VMEM((2, FOX_HEAD_DIM + 2 * SUBLANES, seq_len), BF16),
                        pltpu.VMEM((2, 1, tq), F32),
                        pltpu.VMEM((2, FOX_HEAD_DIM + 2 * SUBLANES, tq), F32)],
        compiler_params=_params("parallel", "parallel", "arbitrary"),
        name="fox_attn_prompt",
    )(qtb, kb, vtb, c, ct)


def _split3(x):
    hi = x.astype(BF16)
    r1 = x - hi.astype(F32)
    mid = r1.astype(BF16)
    return hi, mid, (r1 - mid.astype(F32)).astype(BF16)


def _mask_dot(x, mask):
    mask = mask.astype(BF16)
    hi, mid, lo = _split3(x)
    return _dot(hi, mask) + _dot(mid, mask) + _dot(lo, mask)


def _dot_mask(mask, x, nt=False):
    mask = mask.astype(BF16)
    dot = _dot_nt if nt else _dot
    hi, mid, lo = _split3(x)
    return dot(mask, hi) + dot(mask, mid) + dot(mask, lo)


def _fox_decode_kernel(pt_ref, q_ref, kn_ref, vn_ref, lfn_ref, *refs, n_pages):
    del pt_ref
    kt_refs, vt_refs, lft_refs = refs[:n_pages], refs[n_pages:2 * n_pages], refs[2 * n_pages:3 * n_pages]
    o_ref = refs[3 * n_pages]
    page = kt_refs[0].shape[-1]
    n_new, d = q_ref.shape[1], q_ref.shape[2]
    rows = FOX_HEADS * n_new
    assert rows == page == LANES and n_new == SUBLANES
    ri = lax.broadcasted_iota(jnp.int32, (rows, page), 0)
    ki = lax.broadcasted_iota(jnp.int32, (rows, page), 1)
    tok_bits = n_new.bit_length() - 1
    dim_bits = FOX_HEAD_DIM.bit_length() - 1
    head_of_row = ((lax.broadcasted_iota(jnp.int32, (rows, FOX_HEADS), 0) >> tok_bits)
                   == lax.broadcasted_iota(jnp.int32, (rows, FOX_HEADS), 1))
    own_cols = ((lax.broadcasted_iota(jnp.int32, (rows, d), 0) >> tok_bits)
                == (lax.broadcasted_iota(jnp.int32, (rows, d), 1) >> dim_bits))
    qbd = jnp.where(own_cols, jnp.tile(q_ref[0], (FOX_HEADS, 1)), 0.0).astype(BF16)

    pad = page - n_new
    kn = jnp.concatenate([kn_ref[0], jnp.zeros((pad, d), F32)], axis=0).astype(BF16)
    vn = jnp.concatenate([vn_ref[0], jnp.zeros((pad, d), F32)], axis=0).astype(BF16)
    lfn = jnp.concatenate([lfn_ref[0], jnp.zeros((pad, FOX_HEADS), F32)], axis=0)
    lf_rows = _dot_mask(head_of_row, lfn, nt=True)
    cum = _mask_dot(lf_rows, ri <= ki)
    t_of_row = ri & (n_new - 1)
    a = jnp.sum(jnp.where(ki == t_of_row, cum, 0.0), axis=1, keepdims=True)
    s_new = jnp.where(ki <= t_of_row, _dot_nt(qbd, kn) + a - cum, NEG_INF)

    lft = jnp.concatenate([r[...] for r in lft_refs], axis=0)
    sfx_in = _mask_dot(lft, ri >= ki)
    sfx_ex = sfx_in - lft
    later = jnp.zeros((FOX_HEADS, 1), F32)
    bias = [None] * n_pages
    for i in reversed(range(n_pages)):
        bias[i] = sfx_ex[i * FOX_HEADS:(i + 1) * FOX_HEADS, :] + later
        later = later + sfx_in[i * FOX_HEADS:(i + 1) * FOX_HEADS, 0:1]
    bias = _dot_mask(head_of_row, jnp.concatenate(bias, axis=1))
    s_old = jnp.concatenate(
        [_dot(qbd, r[...].reshape(d, page).astype(BF16)) for r in kt_refs], axis=1) + bias + a

    m = jnp.maximum(jnp.max(s_new, axis=1, keepdims=True), jnp.max(s_old, axis=1, keepdims=True))
    p_new = jnp.exp(s_new - m)
    p_old = jnp.exp(s_old - m)
    denom = jnp.sum(p_new, axis=1, keepdims=True) + jnp.sum(p_old, axis=1, keepdims=True)
    p_old = p_old.astype(BF16)
    acc = _dot(p_new.astype(BF16), vn)
    for i, r in enumerate(vt_refs):
        acc = acc + _dot_nt(p_old[:, i * page:(i + 1) * page], r[...].reshape(d, page).astype(BF16))
    o = jnp.where(own_cols, acc / denom, 0.0).reshape(FOX_HEADS, n_new, d)
    o_ref[0] = jnp.sum(o, axis=0).astype(BF16)


def _fox_attn_sample(q, k_new, v_new, lf_new, cache_kt, cache_vt, cache_lft, page_table):
    nb, n_new, d = q.shape
    n_pages = page_table.shape[1]
    page = cache_kt.shape[-1]

    def new_map(b, pt):
        return (b, 0, 0)

    def kv_spec(i):
        return pl.BlockSpec((None, FOX_HEADS, FOX_HEAD_DIM, page), lambda b, pt: (pt[b * n_pages + i], 0, 0, 0))

    def lf_spec(i):
        return pl.BlockSpec((None, FOX_HEADS, page), lambda b, pt: (pt[b * n_pages + i], 0, 0))

    pages = range(n_pages)
    grid_spec = pltpu.PrefetchScalarGridSpec(
        num_scalar_prefetch=1,
        grid=(nb,),
        in_specs=[pl.BlockSpec((1, n_new, d), new_map), pl.BlockSpec((1, n_new, d), new_map),
                  pl.BlockSpec((1, n_new, d), new_map), pl.BlockSpec((1, n_new, FOX_HEADS), new_map),
                  *[kv_spec(i) for i in pages], *[kv_spec(i) for i in pages], *[lf_spec(i) for i in pages]],
        out_specs=pl.BlockSpec((1, n_new, d), new_map),
    )
    return pl.pallas_call(
        functools.partial(_fox_decode_kernel, n_pages=n_pages),
        grid_spec=grid_spec,
        out_shape=jax.ShapeDtypeStruct((nb, n_new, d), BF16),
        compiler_params=_params("parallel"),
        name="fox_attn_sample",
    )(page_table.reshape(-1), q, k_new, v_new, lf_new,
      *([cache_kt] * n_pages), *([cache_vt] * n_pages), *([cache_lft] * n_pages))


def _ml_proj_kernel(x_ref, g_ref, wk_ref, wqvo_ref, wg_ref, bg_ref,
                    q_ref, k_ref, v_ref, o_ref, gt_ref, *, feature_major):
    d = x_ref.shape[-1]
    hq = ML_HEADS * ML_DQK
    h = _rms(x_ref[...], g_ref[...]).astype(BF16)
    k_ref[...] = (_dot(h, wk_ref[...]) * (ML_DQK ** -0.5)).astype(BF16)
    if feature_major:
        p = _dot_nt(wqvo_ref[...], h)
        q_ref[0] = p[:hq].astype(BF16)
        v_ref[0] = p[hq:hq + d].astype(BF16)
        o_ref[0] = jax.nn.sigmoid(p[hq + d:])
    else:
        p = _dot(h, wqvo_ref[...])
        q_ref[...] = p[:, :hq].astype(BF16)
        v_ref[...] = p[:, hq:hq + d].astype(BF16)
        o_ref[...] = jax.nn.sigmoid(p[:, hq + d:])
    gates = _dot(h, wg_ref[...]) + bg_ref[...]
    gates = GATE_CAP * jnp.tanh(gates / GATE_CAP)
    lane = lax.broadcasted_iota(jnp.int32, gates.shape, 1)
    gates = jnp.where(lane < ML_HEADS, gates, _log_sigmoid(gates))
    gt_ref[...] = gates[:, :2 * ML_HEADS]


def _ml_proj(x, g, wk, wqvo, wg, bg, seq_len, *, feature_major):
    m, d = x.shape
    hq = ML_HEADS * ML_DQK
    n_seq = m // seq_len
    nl = seq_len // ROW_TILE
    row = lambda b, l: (b * nl + l, 0)
    const = lambda b, l: (0, 0)

    def out(width, dtype):
        if feature_major:
            return (pl.BlockSpec((1, width, ROW_TILE), lambda b, l: (b, 0, l)),
                    jax.ShapeDtypeStruct((n_seq, width, seq_len), dtype))
        return pl.BlockSpec((ROW_TILE, width), row), jax.ShapeDtypeStruct((m, width), dtype)

    (q_spec, q_shape), (v_spec, v_shape), (o_spec, o_shape) = out(hq, BF16), out(d, BF16), out(d, F32)
    return pl.pallas_call(
        functools.partial(_ml_proj_kernel, feature_major=feature_major),
        grid=(n_seq, nl),
        in_specs=[pl.BlockSpec((ROW_TILE, d), row), pl.BlockSpec((1, d), const),
                  pl.BlockSpec(wk.shape, const), pl.BlockSpec(wqvo.shape, const),
                  pl.BlockSpec((d, LANES), const), pl.BlockSpec((1, LANES), const)],
        out_specs=(q_spec, pl.BlockSpec((ROW_TILE, hq), row), v_spec, o_spec,
                   pl.BlockSpec((ROW_TILE, 2 * ML_HEADS), row)),
        out_shape=(q_shape, jax.ShapeDtypeStruct((m, hq), BF16), v_shape, o_shape,
                   jax.ShapeDtypeStruct((m, 2 * ML_HEADS), F32)),
        compiler_params=_params("parallel", "parallel"),
        name="ml_proj_prompt" if feature_major else "ml_proj_sample",
    )(x, g, wk, wqvo, wg, bg)


def _ml_chunk_kernel(qt_ref, k_ref, vt_ref, ogt_ref, gt_ref, gtt_ref, gnb_ref,
                     a_ref, c_ref, n_ref, m_ref, ct_sc, n_sc, m_sc):
    t = k_ref.shape[1]

    @pl.when(pl.program_id(1) == 0)
    def _():
        ct_sc[...] = jnp.zeros_like(ct_sc)
        n_sc[...] = jnp.zeros_like(n_sc)
        m_sc[...] = jnp.zeros_like(m_sc)

    gates = gt_ref[0]
    gates_t = gtt_ref[0]
    n_g = 2 * ML_HEADS
    ri = lax.broadcasted_iota(jnp.int32, (t, t), 0)
    ci = lax.broadcasted_iota(jnp.int32, (t, t), 1)
    causal_t = ri <= ci
    b_cols = _dot_mask(ci <= ri, gates)
    b_rows = _mask_dot(gates_t, causal_t)
    w_cols = jnp.concatenate([gates[:, :ML_HEADS] - b_cols[:, ML_HEADS:], jnp.zeros((t, ML_HEADS), F32)],
                             axis=1)
    left = jnp.concatenate([p.astype(F32) for p in _split3(w_cols)]
                           + [jnp.ones((t, n_g), F32), jnp.zeros((t, LANES - 4 * n_g), F32)],
                           axis=1).astype(BF16)
    right_pad = jnp.zeros((LANES - 4 * n_g, t), F32)
    sub = lax.broadcasted_iota(jnp.int32, (n_g, t), 0)
    sub8 = lax.broadcasted_iota(jnp.int32, (SUBLANES, t), 0)
    lane = lax.broadcasted_iota(jnp.int32, (t, LANES), 1)
    half = (lane < ML_DQK, lane >= ML_DQK)
    row_h = lax.broadcasted_iota(jnp.int32, (LANES, t), 0)
    half_rows = (row_h < ML_DQK, row_h >= ML_DQK)
    lane1 = lax.broadcasted_iota(jnp.int32, (1, LANES), 1)
    head_lane = lax.broadcasted_iota(jnp.int32, (1, ML_HEADS), 1)
    m_all = m_sc[...]
    m_out = jnp.zeros((1, ML_HEADS), F32)

    heads = range(ML_HEADS)
    pairs = range(ML_HEADS // 2)
    pj = lambda h: slice((h // 2) * LANES, (h // 2 + 1) * LANES)
    dv = lambda h: slice(h * ML_DV, (h + 1) * ML_DV)
    i_r = [gates_t[h:h + 1, :] for h in heads]
    b_r = [b_rows[ML_HEADS + h:ML_HEADS + h + 1, :] for h in heads]
    m_prev = [m_all[:, h:h + 1] for h in heads]
    ct = [ct_sc[j] for j in pairs]
    n2 = [n_sc[j] for j in pairs]
    km = [jnp.where(half[h % 2], k_ref[0, :, pj(h)], jnp.zeros((t, LANES), BF16)) for h in heads]

    logw, kq, cq, qn = [], [], [], []
    for h in heads:
        qt2 = qt_ref[0, pj(h), :]
        qtm = jnp.where(half_rows[h % 2], qt2, jnp.zeros_like(qt2))
        u_hi, u_mid, u_lo = (p.astype(F32) for p in _split3(b_r[h]))
        tail = jnp.where(sub == 0, u_hi, jnp.where(sub == 1, u_mid, jnp.where(sub == 2, u_lo, 0.0)))
        pick = (sub == h).astype(F32)
        right = jnp.concatenate([pick, pick, pick, tail, right_pad], axis=0).astype(BF16)
        logw.append(_dot(left, right))
        kq.append(_dot(km[h], qt2))
        cq.append(_dot(ct[h // 2].astype(BF16), qtm))
        qn.append(_dot(n2[h // 2].astype(BF16), qtm)[0:1, :])

    m_t, w_inter, a, a_sum, vtw, wk8, decay = [], [], [], [], [], [], []
    for h in heads:
        lw = jnp.where(causal_t, logw[h], NEG_INF)
        inter = b_r[h] + m_prev[h]
        m_t.append(jnp.maximum(inter, jnp.max(lw, axis=0, keepdims=True)))
        w_inter.append(jnp.exp(inter - m_t[h]))
        a_h = jnp.exp(lw - m_t[h]) * kq[h]
        a_sum.append(jnp.sum(a_h, axis=0, keepdims=True))
        a.append(a_h.astype(BF16))
        b_last = b_r[h][:, t - 1:t]
        m_new = m_t[h][:, t - 1:t]
        decay.append(jnp.exp(b_last + m_prev[h] - m_new))
        wk = jnp.exp(b_last - b_r[h] + i_r[h] - m_new)
        vtw.append((vt_ref[0, dv(h), :].astype(F32) * wk).astype(BF16))
        wk8.append(jnp.where(sub8 == 0, wk, 0.0).astype(BF16))
        m_out = jnp.where(head_lane == h, m_new, m_out)

    va = [_dot(vt_ref[0, dv(h), :], a[h]) for h in heads]
    c_add = [_dot(vtw[h], km[h]) for h in heads]
    n_add = [_dot(wk8[h], km[h]) for h in heads]

    for h in heads:
        hu = w_inter[h] * cq[h] + va[h]
        den = w_inter[h] * qn[h] + a_sum[h]
        r1 = 1.0 / jnp.maximum(jnp.abs(den), jnp.exp(-m_t[h]))
        ms = jnp.mean(hu * hu, axis=0, keepdims=True)
        hn = hu * (r1 * lax.rsqrt(r1 * r1 * ms + EPS)) * _lane_tile(gnb_ref[dv(h), :], t)
        a_ref[0, :, dv(h)] = (ogt_ref[0, dv(h), :] * hn).T.astype(BF16)
    for j in pairs:
        d2 = jnp.where(lane1 < ML_DQK, decay[2 * j], decay[2 * j + 1])
        ct_sc[j] = d2 * ct[j] + c_add[2 * j] + c_add[2 * j + 1]
        n_sc[j] = d2 * n2[j] + n_add[2 * j] + n_add[2 * j + 1]
    m_sc[...] = m_out

    @pl.when(pl.program_id(1) == pl.num_programs(1) - 1)
    def _():
        for j in range(ML_HEADS // 2):
            c_ref[0, j] = ct_sc[j].T
            n_ref[0, j:j + 1, :] = n_sc[j][0:1, :]
        m_ref[0] = m_out


def _ml_step_kernel(q_ref, k_ref, v_ref, og_ref, gt_ref, gtt_ref, gn_ref, c0_ref, n0_ref, m0_ref,
                    a_ref, c_ref, n_ref, m_ref):
    g, t = q_ref.shape[0], q_ref.shape[1]
    gates = gt_ref[...]
    gates_t = gtt_ref[...]
    ri = lax.broadcasted_iota(jnp.int32, (g, t, t), 1)
    ci = lax.broadcasted_iota(jnp.int32, (g, t, t), 2)
    causal = ci <= ri
    eye = jnp.broadcast_to(
        (lax.broadcasted_iota(jnp.int32, (ML_DQK, ML_DQK), 0)
         == lax.broadcasted_iota(jnp.int32, (ML_DQK, ML_DQK), 1)).astype(BF16)[None], (g, ML_DQK, ML_DQK))
    head_lane = lax.broadcasted_iota(jnp.int32, (g, 1, ML_HEADS), 2)
    m_all = m0_ref[...]
    m_out = jnp.zeros((g, 1, ML_HEADS), F32)

    for h in range(ML_HEADS):
        sl = slice(h * ML_DV, (h + 1) * ML_DV)
        sq = slice(h * ML_DQK, (h + 1) * ML_DQK)
        fh = ML_HEADS + h
        q_h = q_ref[:, :, sq]
        k_h = k_ref[:, :, sq]
        v_h = v_ref[:, :, sl]
        c_h = c0_ref[:, h]
        n_h = n0_ref[:, h:h + 1, :]
        i_c, f_c = gates[:, :, h:h + 1], gates[:, :, fh:fh + 1]
        i_r, f_r = gates_t[:, h:h + 1, :], gates_t[:, fh:fh + 1, :]
        m_prev = m_all[:, :, h:h + 1]
        b_c = jnp.sum(jnp.where(causal, f_r, 0.0), axis=2, keepdims=True)
        b_r = jnp.sum(jnp.where(ri <= ci, f_c, 0.0), axis=1, keepdims=True)
        logw = jnp.where(causal, b_c - b_r + i_r, NEG_INF)
        inter = b_c + m_prev
        m_t = jnp.maximum(inter, jnp.max(logw, axis=2, keepdims=True))
        w_inter = jnp.exp(inter - m_t)
        a = jnp.exp(logw - m_t) * jnp.einsum("gtd,gsd->gts", q_h, k_h, preferred_element_type=F32)
        num = (w_inter * jnp.einsum("gtd,gde->gte", q_h, c_h.astype(BF16), preferred_element_type=F32)
               + jnp.einsum("gts,gse->gte", a.astype(BF16), v_h, preferred_element_type=F32))
        qn = jnp.sum(q_h.astype(F32) * n_h, axis=2, keepdims=True)
        den = w_inter * qn + jnp.sum(a, axis=2, keepdims=True)
        hs = num / jnp.maximum(jnp.abs(den), jnp.exp(-m_t))
        hn = hs * lax.rsqrt(jnp.mean(hs * hs, axis=2, keepdims=True) + EPS) * gn_ref[:, sl]
        a_ref[:, :, sl] = (og_ref[:, :, sl] * hn).astype(BF16)
        b_last = b_c[:, t - 1:t, :]
        m_new = m_t[:, t - 1:t, :]
        decay = jnp.exp(b_last + m_prev - m_new)
        kw = jnp.exp(b_last - b_c + i_c - m_new) * k_h.astype(F32)
        kt_h = jnp.einsum("gdk,gsk->gds", eye, k_h, preferred_element_type=F32)
        kwt = (kt_h * jnp.exp(b_last - b_r + i_r - m_new)).astype(BF16)
        c_ref[:, h] = decay * c_h + jnp.einsum("gds,gse->gde", kwt, v_h, preferred_element_type=F32)
        n_ref[:, h:h + 1, :] = decay * n_h + jnp.sum(kw, axis=1, keepdims=True)
        m_out = jnp.where(head_lane == h, m_new, m_out)
    m_ref[...] = m_out


def _ml_chunk(qt, k, vt, ogt, gates, gates_t, gnb, chunk):
    nb, d, seq_len = vt.shape
    hq = k.shape[2]
    pairs = ML_HEADS // 2
    tok = lambda width: pl.BlockSpec((1, chunk, width), lambda b, c: (b, c, 0))
    tok_t = lambda width: pl.BlockSpec((1, width, chunk), lambda b, c: (b, 0, c))
    return pl.pallas_call(
        _ml_chunk_kernel,
        grid=(nb, seq_len // chunk),
        in_specs=[tok_t(hq), tok(hq), tok_t(d), tok_t(d), tok(2 * ML_HEADS), tok_t(2 * ML_HEADS),
                  pl.BlockSpec((d, LANES), lambda b, c: (0, 0))],
        out_specs=(tok(d), pl.BlockSpec((1, pairs, LANES, LANES), lambda b, c: (b, 0, 0, 0)),
                   pl.BlockSpec((1, pairs, LANES), lambda b, c: (b, 0, 0)),
                   pl.BlockSpec((1, 1, ML_HEADS), lambda b, c: (b, 0, 0))),
        out_shape=(jax.ShapeDtypeStruct((nb, seq_len, d), BF16),
                   jax.ShapeDtypeStruct((nb, pairs, LANES, LANES), F32),
                   jax.ShapeDtypeStruct((nb, pairs, LANES), F32),
                   jax.ShapeDtypeStruct((nb, 1, ML_HEADS), F32)),
        scratch_shapes=[pltpu.VMEM((pairs, ML_DV, LANES), F32), pltpu.VMEM((pairs, SUBLANES, LANES), F32),
                        pltpu.VMEM((1, ML_HEADS), F32)],
        compiler_params=_params("parallel", "arbitrary"),
        name="ml_chunk",
    )(qt, k, vt, ogt, gates, gates_t, gnb)


def _ml_step(q, k, v, og, gates, gates_t, gn, c0, n0, m0):
    nb, t, d = v.shape
    hq = q.shape[2]
    g = ML_STEP_SEQS
    lead = lambda *rest: pl.BlockSpec((g, *rest), lambda i: (i,) + (0,) * len(rest))
    state = (lead(ML_HEADS, ML_DQK, ML_DV), lead(ML_HEADS, ML_DQK), lead(1, ML_HEADS))
    return pl.pallas_call(
        _ml_step_kernel,
        grid=(nb // g,),
        in_specs=[lead(t, hq), lead(t, hq), lead(t, d), lead(t, d), lead(t, 2 * ML_HEADS),
                  lead(2 * ML_HEADS, t), pl.BlockSpec((1, d), lambda i: (0, 0)), *state],
        out_specs=(lead(t, d), *state),
        out_shape=(jax.ShapeDtypeStruct((nb, t, d), BF16), jax.ShapeDtypeStruct(c0.shape, F32),
                   jax.ShapeDtypeStruct(n0.shape, F32), jax.ShapeDtypeStruct(m0.shape, F32)),
        compiler_params=_params("parallel"),
        name="ml_step",
    )(q, k, v, og, gates, gates_t, gn, c0, n0, m0)


def _pad_cols(w, width):
    return jnp.pad(w, ((0, 0), (0, width - w.shape[1])))


def kernel(x_prompt, x_sample, state_conv, cache_k, cache_v, cache_logf, page_table, state_C, state_n, state_m, norm_mix, norm_ffn, norm_final, w_conv_in, w_conv, w_conv_out, w_fox_in, b_fox_f, w_fox_out, w_ml_in, b_ml_gates, g_ml_norm, w_ml_out, w_ffn_gu, w_ffn_down):
    bp, lp, d = x_prompt.shape
    bs, ls, _ = x_sample.shape
    depth = norm_mix.shape[0]
    dff = w_ffn_down.shape[1]
    assert d == D_MODEL and ls == SUBLANES
    xp = x_prompt.reshape(bp * lp, d)
    xs = x_sample.reshape(bs * ls, d)
    gf = norm_final.reshape(1, d)
    hq = ML_HEADS * ML_DQK
    pairs = ML_HEADS // 2
    out = {name: [] for name in ("conv_p", "conv_s", "kp", "vp", "lfp", "ks", "vs", "lfs",
                                 "cp", "np", "mp", "cs", "ns", "ms")}

    for i in range(depth):
        kind, j = i % 3, i // 3
        gm = norm_mix[i].reshape(1, d)
        if kind == 0:
            win = w_conv_in[j].astype(BF16)
            wo = w_conv_out[j].astype(BF16)
            ap, sp = _conv_mixer(xp, gm, win, w_conv[j], jnp.zeros((bp, CONV_W - 1, d), F32), lp)
            as_, ss = _conv_mixer(xs, gm, win, w_conv[j], state_conv[j], ls)
            out["conv_p"].append(sp)
            out["conv_s"].append(ss)
        elif kind == 1:
            wqkv = w_fox_in[j][:, :3 * d].astype(BF16)
            wf = _pad_cols(w_fox_in[j][:, 3 * d:], LANES).astype(BF16)
            bf = _pad_cols(b_fox_f[j].reshape(1, FOX_HEADS), LANES)
            wo = w_fox_out[j].astype(BF16)
            qtb, kt, vt, kb, vtb, lft, c, ct = _fox_proj_prompt(
                xp, gm, wqkv[:, d:2 * d], wqkv.T, wf, wf[:, :FOX_HEADS].T, bf,
                b_fox_f[j].reshape(FOX_HEADS, 1), lp)
            ap = _fox_attn_prompt(qtb, kb, vtb, c, ct, lp)
            heads_last = lambda t: jnp.transpose(t.reshape(bp, FOX_HEADS, FOX_HEAD_DIM, lp), (0, 3, 1, 2))
            out["kp"].append(heads_last(kt))
            out["vp"].append(heads_last(vt))
            out["lfp"].append(jnp.swapaxes(lft, 1, 2))
            q, k, v, lf = _fox_proj_sample(xs, gm, wqkv, wf, bf)
            as_ = _fox_attn_sample(
                q.reshape(bs, ls, d), k.reshape(bs, ls, d), v.reshape(bs, ls, d),
                lf.reshape(bs, ls, FOX_HEADS), jnp.transpose(cache_k[j], (0, 2, 3, 1)),
                jnp.transpose(cache_v[j], (0, 2, 3, 1)), jnp.transpose(cache_logf[j], (0, 2, 1)),
                page_table).reshape(bs * ls, d)
            out["ks"].append(k.reshape(bs, ls, FOX_HEADS, FOX_HEAD_DIM))
            out["vs"].append(v.reshape(bs, ls, FOX_HEADS, FOX_HEAD_DIM))
            out["lfs"].append(lf.reshape(bs, ls, FOX_HEADS))
        else:
            w = w_ml_in[j]
            wk = w[:, hq:2 * hq].astype(BF16)
            wqvo = jnp.concatenate([w[:, :hq], w[:, 2 * hq:2 * hq + 2 * d]], axis=1).astype(BF16)
            wg = _pad_cols(w[:, 2 * hq + 2 * d:], LANES).astype(BF16)
            bg = _pad_cols(b_ml_gates[j].reshape(1, 2 * ML_HEADS), LANES)
            gn = g_ml_norm[j].reshape(1, d)
            wo = w_ml_out[j].astype(BF16)

            def tokens(nb, seq_len, *arrays):
                return [t.reshape(nb, seq_len, t.shape[-1]) for t in arrays]

            qt, k, vt, ogt, gates = _ml_proj(xp, gm, wk, wqvo.T, wg, bg, lp, feature_major=True)
            k, gates = tokens(bp, lp, k, gates)
            a, c1, n1, m1 = _ml_chunk(qt, k, vt, ogt, gates, jnp.swapaxes(gates, 1, 2),
                                      jnp.broadcast_to(gn.reshape(d, 1), (d, LANES)), ML_CHUNK_PROMPT)
            ap = a.reshape(bp * lp, d)
            out["cp"].append(c1.reshape(bp, ML_HEADS, ML_DQK, ML_DV))
            out["np"].append(n1.reshape(bp, ML_HEADS, ML_DQK))
            out["mp"].append(m1.reshape(bp, ML_HEADS))

            q, k, v, og, gates = _ml_proj(xs, gm, wk, wqvo, wg, bg, ROW_TILE, feature_major=False)
            q, k, v, og, gates = tokens(bs, ls, q, k, v, og, gates)
            a, c1, n1, m1 = _ml_step(q, k, v, og, gates, jnp.swapaxes(gates, 1, 2), gn,
                                     state_C[j], state_n[j], state_m[j].reshape(bs, 1, ML_HEADS))
            as_ = a.reshape(bs * ls, d)
            out["cs"].append(c1); out["ns"].append(n1); out["ms"].append(m1.reshape(bs, ML_HEADS))

        gffn = norm_ffn[i].reshape(1, d)
        wg_ffn = w_ffn_gu[i][:, :dff].astype(BF16)
        wu_ffn = w_ffn_gu[i][:, dff:].astype(BF16)
        wd_ffn = w_ffn_down[i].astype(BF16)
        last = i == depth - 1
        xp = _post(xp, ap, wo, gffn, wg_ffn, wu_ffn, wd_ffn, gf, final_norm=last)
        xs = _post(xs, as_, wo, gffn, wg_ffn, wu_ffn, wd_ffn, gf, final_norm=last)

    st = lambda name: jnp.stack(out[name])
    return (xp.reshape(bp, lp, d), xs.reshape(bs, ls, d), st("conv_p"), st("conv_s"),
            st("kp"), st("vp"), st("lfp"), st("ks"), st("vs"), st("lfs"),
            st("cp"), st("np"), st("mp"), st("cs"), st("ns"), st("ms"))
```

```python
import functools

import jax
import jax.numpy as jnp
from jax import lax
from jax.experimental import pallas as pl
from jax.experimental.pallas import tpu as pltpu

F32 = jnp.float32
BF16 = jnp.bfloat16
HIGHEST = lax.Precision.HIGHEST

D_MODEL = 1024
CONV_W = 3
FOX_HEADS = 16
FOX_HEAD_DIM = D_MODEL // FOX_HEADS
ML_HEADS = 8
ML_DV = D_MODEL // ML_HEADS
ML_DQK = ML_DV // 2
GATE_CAP = 15.0
EPS = 1e-6
NEG_INF = -1e30
LOG2E = 1.4426950408889634

LANES = 128
SUBLANES = 8
ROW_TILE = 1024
FOX_PROJ_TILE = 512
POST_ROW_TILE = 1024
FFN_CHUNK = 256
CONV_CHUNK = 256
ATTN_TQ = 512
ML_CHUNK_PROMPT = 128
ML_STEP_SEQS = 16
VMEM_LIMIT = 56 * 1024 * 1024


def _params(*sem):
    return pltpu.CompilerParams(dimension_semantics=sem, vmem_limit_bytes=VMEM_LIMIT)


def _resident(shape):
    return pl.BlockSpec(shape, lambda *_: (0, 0), pipeline_mode=pl.Buffered(1))


def _rms(x, g):
    return x * lax.rsqrt(jnp.mean(x * x, axis=-1, keepdims=True) + EPS) * g


def _log_sigmoid(x):
    return jnp.minimum(x, 0.0) - jnp.log1p(jnp.exp(-jnp.abs(x)))


def _dot(a, b):
    return jnp.dot(a, b, preferred_element_type=F32)


def _dot_hi(a, b):
    return jnp.dot(a, b, precision=HIGHEST, preferred_element_type=F32)


def _dot_nt(a, b, precision=None):
    return lax.dot_general(a, b, (((1,), (1,)), ((), ())), precision=precision,
                           preferred_element_type=F32)


def _dot_tn(a, b):
    return lax.dot_general(a, b, (((0,), (0,)), ((), ())), preferred_element_type=F32)


def _lane_tile(x, width):
    return jnp.tile(x, (1, width // LANES))


def _post_kernel(x_ref, a_ref, wo_ref, g_ref, wg_ref, wu_ref, wd_ref, gf_ref, o_ref, *, final_norm):
    x1 = x_ref[...] + _dot(a_ref[...], wo_ref[...])
    h = _rms(x1, g_ref[...]).astype(BF16)
    y = x1
    for c in range(0, wg_ref.shape[1], FFN_CHUNK):
        g = _dot(h, wg_ref[:, c:c + FFN_CHUNK])
        u = _dot(h, wu_ref[:, c:c + FFN_CHUNK])
        y = y + _dot(((g * jax.nn.sigmoid(g)) * u).astype(BF16), wd_ref[c:c + FFN_CHUNK, :])
    o_ref[...] = _rms(y, gf_ref[...]) if final_norm else y


def _post(x, a, wo, g, wg, wu, wd, gf, *, final_norm):
    m, d = x.shape
    dff = wg.shape[1]
    tm = min(POST_ROW_TILE, m)
    assert dff % FFN_CHUNK == 0 and m % tm == 0
    row = pl.BlockSpec((tm, d), lambda i: (i, 0))
    return pl.pallas_call(
        functools.partial(_post_kernel, final_norm=final_norm),
        grid=(m // tm,),
        in_specs=[row, row, _resident((d, d)), _resident((1, d)), _resident((d, dff)), _resident((d, dff)),
                  _resident((dff, d)), _resident((1, d))],
        out_specs=row,
        out_shape=jax.ShapeDtypeStruct((m, d), F32),
        compiler_params=_params("parallel"),
        name="post_ffn",
    )(x, a, wo, g, wg, wu, wd, gf)


def _conv_columns(x_ref, g_ref, win_ref, wc_ref, z_ref, rig, prev_rows, keep_tail):
    d = x_ref.shape[-1]
    h = _rms(x_ref[...], g_ref[...]).astype(BF16)
    for c in range(0, d, CONV_CHUNK):
        cols = slice(c, c + CONV_CHUNK)
        bg = _dot(h, win_ref[:, c:c + CONV_CHUNK])
        u = (_dot(h, win_ref[:, d + c:d + c + CONV_CHUNK])
             * _dot(h, win_ref[:, 2 * d + c:2 * d + c + CONV_CHUNK]))
        p0, p1 = prev_rows(cols)
        s1 = jnp.where(rig == 0, p1, pltpu.roll(u, 1, axis=0))
        s2 = jnp.where(rig == 0, p0, jnp.where(rig == 1, p1, pltpu.roll(u, 2, axis=0)))
        conv = wc_ref[0:1, cols] * s2 + wc_ref[1:2, cols] * s1 + wc_ref[2:3, cols] * u
        z_ref[:, cols] = (bg * conv).astype(BF16)
        keep_tail(cols, u)


def _conv_prompt_kernel(x_ref, g_ref, win_ref, wc_ref, prev_ref, z_ref, st_ref, carry_sc):
    @pl.when(pl.program_id(1) == 0)
    def _():
        carry_sc[...] = prev_ref[0]

    r = x_ref.shape[0]
    rig = lax.broadcasted_iota(jnp.int32, (r, CONV_CHUNK), 0)

    def keep_tail(cols, u):
        tail = u[r - (CONV_W - 1):, :]
        carry_sc[:, cols] = tail
        st_ref[0, :, cols] = tail

    _conv_columns(x_ref, g_ref, win_ref, wc_ref, z_ref, rig,
                  lambda cols: (carry_sc[0:1, cols], carry_sc[1:2, cols]), keep_tail)


def _conv_sample_kernel(x_ref, g_ref, win_ref, wc_ref, prev_ref, z_ref, st_ref):
    r = x_ref.shape[0]
    n_seq = r // SUBLANES
    rig = lax.broadcasted_iota(jnp.int32, (r, CONV_CHUNK), 0) & (SUBLANES - 1)

    def prev_rows(cols):
        prev = prev_ref[:, :, cols]
        rows = lambda i: jnp.broadcast_to(prev[:, i:i + 1, :], (n_seq, SUBLANES, CONV_CHUNK)).reshape(r, CONV_CHUNK)
        return rows(0), rows(1)

    def keep_tail(cols, u):
        st_ref[:, :, cols] = u.reshape(n_seq, SUBLANES, CONV_CHUNK)[:, SUBLANES - (CONV_W - 1):, :]

    _conv_columns(x_ref, g_ref, win_ref, wc_ref, z_ref, rig, prev_rows, keep_tail)


def _conv_mixer(x, g, win, wc, prev, seq_len):
    m, d = x.shape
    n_seq = m // seq_len
    common_in = [_resident((d, 3 * d)), _resident((CONV_W, d))]
    out_shape = (jax.ShapeDtypeStruct((m, d), BF16), jax.ShapeDtypeStruct((n_seq, CONV_W - 1, d), F32))
    if seq_len == SUBLANES:
        per = ROW_TILE // SUBLANES
        return pl.pallas_call(
            _conv_sample_kernel,
            grid=(m // ROW_TILE,),
            in_specs=[pl.BlockSpec((ROW_TILE, d), lambda i: (i, 0)), pl.BlockSpec((1, d), lambda i: (0, 0)),
                      *common_in, pl.BlockSpec((per, CONV_W - 1, d), lambda i: (i, 0, 0))],
            out_specs=(pl.BlockSpec((ROW_TILE, d), lambda i: (i, 0)),
                       pl.BlockSpec((per, CONV_W - 1, d), lambda i: (i, 0, 0))),
            out_shape=out_shape,
            compiler_params=_params("parallel"),
            name="conv_sample",
        )(x, g, win, wc, prev)
    assert seq_len % ROW_TILE == 0
    nl = seq_len // ROW_TILE
    return pl.pallas_call(
        _conv_prompt_kernel,
        grid=(n_seq, nl),
        in_specs=[pl.BlockSpec((ROW_TILE, d), lambda b, l: (b * nl + l, 0)),
                  pl.BlockSpec((1, d), lambda b, l: (0, 0)),
                  *common_in, pl.BlockSpec((1, CONV_W - 1, d), lambda b, l: (b, 0, 0))],
        out_specs=(pl.BlockSpec((ROW_TILE, d), lambda b, l: (b * nl + l, 0)),
                   pl.BlockSpec((1, CONV_W - 1, d), lambda b, l: (b, 0, 0))),
        out_shape=out_shape,
        scratch_shapes=[pltpu.VMEM((CONV_W - 1, d), F32)],
        compiler_params=_params("parallel", "arbitrary"),
        name="conv_prompt",
    )(x, g, win, wc, prev)


def _fox_proj_kernel(x_ref, g_ref, wqkv_ref, wf_ref, bf_ref, q_ref, k_ref, v_ref, lf_ref):
    d = x_ref.shape[-1]
    h = _rms(x_ref[...], g_ref[...]).astype(BF16)
    p = _dot(h, wqkv_ref[...])
    logf = _log_sigmoid(_dot(h, wf_ref[...]) + bf_ref[...])
    q_ref[...] = p[:, :d] * (FOX_HEAD_DIM ** -0.5)
    k_ref[...] = p[:, d:2 * d]
    v_ref[...] = p[:, 2 * d:]
    lf_ref[...] = logf[:, :FOX_HEADS]


def _fox_proj_sample(x, g, wqkv, wf, bf):
    m, d = x.shape
    row = lambda i: (i, 0)
    const = lambda i: (0, 0)
    big = pl.BlockSpec((ROW_TILE, d), row)
    f32_out = jax.ShapeDtypeStruct((m, d), F32)
    return pl.pallas_call(
        _fox_proj_kernel,
        grid=(m // ROW_TILE,),
        in_specs=[big, pl.BlockSpec((1, d), const), _resident((d, 3 * d)),
                  pl.BlockSpec((d, LANES), const), pl.BlockSpec((1, LANES), const)],
        out_specs=(big, big, big, pl.BlockSpec((ROW_TILE, FOX_HEADS), row)),
        out_shape=(f32_out, f32_out, f32_out, jax.ShapeDtypeStruct((m, FOX_HEADS), F32)),
        compiler_params=_params("parallel"),
        name="fox_proj_sample",
    )(x, g, wqkv, wf, bf)


def _fox_proj_prompt_kernel(x_ref, g_ref, wqkvt_ref, wf_ref, wft_ref, bf_ref, bft_ref,
                            qtb_ref, kt_ref, vt_ref, kb_ref, vtb_ref, lft_ref, c_ref, ct_ref,
                            carry_sc, carryt_sc):
    d = x_ref.shape[-1]
    r = x_ref.shape[0]
    h = _rms(x_ref[...], g_ref[...]).astype(BF16)
    qkv = _dot_nt(wqkvt_ref[...], h)
    qtb_ref[0] = (qkv[:d] * (FOX_HEAD_DIM ** -0.5 * LOG2E)).astype(BF16)
    kt_ref[0] = qkv[d:2 * d]
    vt_ref[0] = qkv[2 * d:]
    kb_ref[...] = qkv[d:2 * d].T.astype(BF16)
    vtb_ref[0] = qkv[2 * d:].astype(BF16)
    logf = _log_sigmoid(_dot(h, wf_ref[...]) + bf_ref[...])
    logf_t = _log_sigmoid(_dot_nt(wft_ref[...], h) + bft_ref[...])
    lft_ref[0] = logf_t

    @pl.when(pl.program_id(1) == 0)
    def _():
        carry_sc[...] = jnp.zeros_like(carry_sc)
        carryt_sc[...] = jnp.zeros_like(carryt_sc)

    ri = lax.broadcasted_iota(jnp.int32, (r, r), 0)
    ci = lax.broadcasted_iota(jnp.int32, (r, r), 1)
    c = carry_sc[...] + _dot_mask(ci <= ri, logf)
    carry_sc[...] = c[r - 1:r, :]
    c_ref[...] = c[:, :FOX_HEADS]
    ct = _lane_tile(carryt_sc[...], r) + _mask_dot(logf_t, ri <= ci)
    carryt_sc[...] = jnp.broadcast_to(ct[:, r - 1:r], carryt_sc.shape)
    ct_ref[0] = ct


def _fox_proj_prompt(x, g, wqkvt, wf, wft, bf, bft, seq_len):
    m, d = x.shape
    n_seq = m // seq_len
    tile = FOX_PROJ_TILE
    nl = seq_len // tile
    row = lambda b, l: (b * nl + l, 0)
    const = lambda b, l: (0, 0)
    feat = lambda width: pl.BlockSpec((1, width, tile), lambda b, l: (b, 0, l))
    feat_shape = lambda width, dtype: jax.ShapeDtypeStruct((n_seq, width, seq_len), dtype)
    return pl.pallas_call(
        _fox_proj_prompt_kernel,
        grid=(n_seq, nl),
        in_specs=[pl.BlockSpec((tile, d), row), pl.BlockSpec((1, d), const), _resident((3 * d, d)),
                  pl.BlockSpec((d, LANES), const), pl.BlockSpec((FOX_HEADS, d), const),
                  pl.BlockSpec((1, LANES), const), pl.BlockSpec((FOX_HEADS, 1), const)],
        out_specs=(feat(d), feat(d), feat(d), pl.BlockSpec((tile, d), row), feat(d),
                   feat(FOX_HEADS), pl.BlockSpec((tile, FOX_HEADS), row), feat(FOX_HEADS)),
        out_shape=(feat_shape(d, BF16), feat_shape(d, F32), feat_shape(d, F32),
                   jax.ShapeDtypeStruct((m, d), BF16), feat_shape(d, BF16),
                   feat_shape(FOX_HEADS, F32), jax.ShapeDtypeStruct((m, FOX_HEADS), F32),
                   feat_shape(FOX_HEADS, F32)),
        scratch_shapes=[pltpu.VMEM((1, LANES), F32), pltpu.VMEM((FOX_HEADS, LANES), F32)],
        compiler_params=_params("parallel", "arbitrary"),
        name="fox_proj_prompt",
    )(x, g, wqkvt, wf, wft, bf, bft)


def _fox_attn_kernel(qt_ref, k_ref, vt_ref, c_ref, ct_ref, o_ref, kaug_sc, vaug_sc, m_sc, acc_sc, *, tq):
    hp = pl.program_id(1)
    qi = pl.program_id(2)
    seq = k_ref.shape[0]
    hd = FOX_HEAD_DIM
    n_aug = 2 * SUBLANES
    n_bias = 3
    one_row = jnp.where(lax.broadcasted_iota(jnp.int32, (n_aug, seq), 0) == 0, 1.0, 0.0).astype(BF16)

    @pl.when(qi == 0)
    def _():
        for hh in range(2):
            vaug_sc[hh] = jnp.concatenate([vt_ref[0, hh * hd:(hh + 1) * hd, :], one_row], axis=0)
        c3 = jnp.concatenate([p.astype(F32) for p in _split3(c_ref[...] * LOG2E)]
                             + [jnp.zeros((seq, LANES - n_bias * FOX_HEADS), F32)], axis=1).astype(BF16)
        src = lax.broadcasted_iota(jnp.int32, (LANES, LANES), 0)
        dst = lax.broadcasted_iota(jnp.int32, (LANES, LANES), 1)
        term = src >> (FOX_HEADS.bit_length() - 1)
        lane = lax.broadcasted_iota(jnp.int32, (seq, LANES), 1)
        k2 = k_ref[...]
        for hh in range(2):
            off = hd * (1 - hh)
            place = jnp.where((src & (FOX_HEADS - 1)) == 2 * hp + hh,
                              jnp.where(dst == off + term, -1.0, 0.0), 0.0).astype(BF16)
            ones = jnp.where(lane >= off + n_bias, jnp.where(lane < off + 2 * n_bias, 1.0, 0.0), 0.0)
            aug = (_dot(c3, place) + ones).astype(BF16)
            own = (lane >= hd) if hh else (lane < hd)
            kaug_sc[hh] = jnp.where(own, k2, aug)

    qt2 = qt_ref[0]
    sub = lax.broadcasted_iota(jnp.int32, (n_aug, tq), 0)
    qa = []
    for hh in range(2):
        c_hi, c_mid, c_lo = (p.astype(F32) for p in _split3(ct_ref[0, pl.ds(2 * hp + hh, 1), :] * LOG2E))
        piece = jnp.where(sub < n_bias, 1.0, jnp.where(sub == n_bias, c_hi, jnp.where(
            sub == n_bias + 1, c_mid, jnp.where(sub == n_bias + 2, c_lo, 0.0)))).astype(BF16)
        rest = jnp.zeros((hd - n_aug, tq), BF16)
        qa.append(jnp.concatenate([qt2[:hd], piece, rest] if hh == 0 else [piece, rest, qt2[hd:]], axis=0))
        m_sc[hh] = jnp.full((1, tq), NEG_INF, F32)
    acc_sc[...] = jnp.zeros_like(acc_sc)

    visible = (lax.broadcasted_iota(jnp.int32, (tq, tq), 0)
               <= lax.broadcasted_iota(jnp.int32, (tq, tq), 1))

    def scores(ki):
        return [_dot(kaug_sc[hh, ki * tq:(ki + 1) * tq, :], qa[hh]) for hh in range(2)]

    def absorb(ki, s2, diagonal):
        p2, alpha2 = [], []
        for hh in range(2):
            s = jnp.where(visible, s2[hh], NEG_INF) if diagonal else s2[hh]
            m_prev = m_sc[hh]
            m_new = jnp.maximum(m_prev, jnp.max(s, axis=0, keepdims=True))
            alpha2.append(jnp.exp2(m_prev - m_new))
            p2.append(jnp.exp2(s - m_new).astype(BF16))
            m_sc[hh] = m_new
        pv2 = [_dot(vaug_sc[hh, :, ki * tq:(ki + 1) * tq], p2[hh]) for hh in range(2)]
        for hh in range(2):
            acc_sc[hh] = alpha2[hh] * acc_sc[hh] + pv2[hh]

    for n in range(seq // tq):
        @pl.when(qi == n)
        def _():
            s_next = scores(0)
            for ki in range(n + 1):
                s_cur = s_next
                if ki < n:
                    s_next = scores(ki + 1)
                absorb(ki, s_cur, ki == n)

    o_t = jnp.concatenate([acc_sc[hh, :hd, :] / acc_sc[hh, hd:hd + 1, :] for hh in range(2)], axis=0)
    o_ref[...] = o_t.T.astype(BF16)


def _fox_attn_prompt(qtb, kb, vtb, c, ct, seq_len):
    n_seq, d, _ = qtb.shape
    tq = ATTN_TQ
    nq = seq_len // tq
    return pl.pallas_call(
        functools.partial(_fox_attn_kernel, tq=tq),
        grid=(n_seq, d // LANES, nq),
        in_specs=[pl.BlockSpec((1, LANES, tq), lambda b, hp, qi: (b, hp, qi)),
                  pl.BlockSpec((seq_len, LANES), lambda b, hp, qi: (b, hp)),
                  pl.BlockSpec((1, LANES, seq_len), lambda b, hp, qi: (b, hp, 0)),
                  pl.BlockSpec((seq_len, FOX_HEADS), lambda b, hp, qi: (b, 0)),
                  pl.BlockSpec((1, FOX_HEADS, tq), lambda b, hp, qi: (b, 0, qi))],
        out_specs=pl.BlockSpec((tq, LANES), lambda b, hp, qi: (b * nq + qi, hp)),
        out_shape=jax.ShapeDtypeStruct((n_seq * seq_len, d), BF16),
        scratch_shapes=[pltpu.VMEM((2, seq_len, LANES), BF16),
                        pltpu.VMEM((2, FOX_HEAD_DIM + 2 * SUBLANES, seq_len), BF16),
                        pltpu.VMEM((2, 1, tq), F32),
                        pltpu.VMEM((2, FOX_HEAD_DIM + 2 * SUBLANES, tq), F32)],
        compiler_params=_params("parallel", "parallel", "arbitrary"),
        name="fox_attn_prompt",
    )(qtb, kb, vtb, c, ct)


def _split3(x):
    hi = x.astype(BF16)
    r1 = x - hi.astype(F32)
    mid = r1.astype(BF16)
    return hi, mid, (r1 - mid.astype(F32)).astype(BF16)


def _mask_dot(x, mask):
    mask = mask.astype(BF16)
    hi, mid, lo = _split3(x)
    return _dot(hi, mask) + _dot(mid, mask) + _dot(lo, mask)


def _dot_mask(mask, x, nt=False):
    mask = mask.astype(BF16)
    dot = _dot_nt if nt else _dot
    hi, mid, lo = _split3(x)
    return dot(mask, hi) + dot(mask, mid) + dot(mask, lo)


def _fox_decode_kernel(pt_ref, q_ref, kn_ref, vn_ref, lfn_ref, *refs, n_pages):
    del pt_ref
    kt_refs, vt_refs, lft_refs = refs[:n_pages], refs[n_pages:2 * n_pages], refs[2 * n_pages:3 * n_pages]
    o_ref = refs[3 * n_pages]
    page = kt_refs[0].shape[-1]
    n_new, d = q_ref.shape[1], q_ref.shape[2]
    rows = FOX_HEADS * n_new
    assert rows == page == LANES and n_new == SUBLANES
    ri = lax.broadcasted_iota(jnp.int32, (rows, page), 0)
    ki = lax.broadcasted_iota(jnp.int32, (rows, page), 1)
    tok_bits = n_new.bit_length() - 1
    dim_bits = FOX_HEAD_DIM.bit_length() - 1
    head_of_row = ((lax.broadcasted_iota(jnp.int32, (rows, FOX_HEADS), 0) >> tok_bits)
                   == lax.broadcasted_iota(jnp.int32, (rows, FOX_HEADS), 1))
    own_cols = ((lax.broadcasted_iota(jnp.int32, (rows, d), 0) >> tok_bits)
                == (lax.broadcasted_iota(jnp.int32, (rows, d), 1) >> dim_bits))
    qbd = jnp.where(own_cols, jnp.tile(q_ref[0], (FOX_HEADS, 1)), 0.0).astype(BF16)

    pad = page - n_new
    kn = jnp.concatenate([kn_ref[0], jnp.zeros((pad, d), F32)], axis=0).astype(BF16)
    vn = jnp.concatenate([vn_ref[0], jnp.zeros((pad, d), F32)], axis=0).astype(BF16)
    lfn = jnp.concatenate([lfn_ref[0], jnp.zeros((pad, FOX_HEADS), F32)], axis=0)
    lf_rows = _dot_mask(head_of_row, lfn, nt=True)
    cum = _mask_dot(lf_rows, ri <= ki)
    t_of_row = ri & (n_new - 1)
    a = jnp.sum(jnp.where(ki == t_of_row, cum, 0.0), axis=1, keepdims=True)
    s_new = jnp.where(ki <= t_of_row, _dot_nt(qbd, kn) + a - cum, NEG_INF)

    lft = jnp.concatenate([r[...] for r in lft_refs], axis=0)
    sfx_in = _mask_dot(lft, ri >= ki)
    sfx_ex = sfx_in - lft
    later = jnp.zeros((FOX_HEADS, 1), F32)
    bias = [None] * n_pages
    for i in reversed(range(n_pages)):
        bias[i] = sfx_ex[i * FOX_HEADS:(i + 1) * FOX_HEADS, :] + later
        later = later + sfx_in[i * FOX_HEADS:(i + 1) * FOX_HEADS, 0:1]
    bias = _dot_mask(head_of_row, jnp.concatenate(bias, axis=1))
    s_old = jnp.concatenate(
        [_dot(qbd, r[...].reshape(d, page).astype(BF16)) for r in kt_refs], axis=1) + bias + a

    m = jnp.maximum(jnp.max(s_new, axis=1, keepdims=True), jnp.max(s_old, axis=1, keepdims=True))
    p_new = jnp.exp(s_new - m)
    p_old = jnp.exp(s_old - m)
    denom = jnp.sum(p_new, axis=1, keepdims=True) + jnp.sum(p_old, axis=1, keepdims=True)
    p_old = p_old.astype(BF16)
    acc = _dot(p_new.astype(BF16), vn)
    for i, r in enumerate(vt_refs):
        acc = acc + _dot_nt(p_old[:, i * page:(i + 1) * page], r[...].reshape(d, page).astype(BF16))
    o = jnp.where(own_cols, acc / denom, 0.0).reshape(FOX_HEADS, n_new, d)
    o_ref[0] = jnp.sum(o, axis=0).astype(BF16)


def _fox_attn_sample(q, k_new, v_new, lf_new, cache_kt, cache_vt, cache_lft, page_table):
    nb, n_new, d = q.shape
    n_pages = page_table.shape[1]
    page = cache_kt.shape[-1]

    def new_map(b, pt):
        return (b, 0, 0)

    def kv_spec(i):
        return pl.BlockSpec((None, FOX_HEADS, FOX_HEAD_DIM, page), lambda b, pt: (pt[b * n_pages + i], 0, 0, 0))

    def lf_spec(i):
        return pl.BlockSpec((None, FOX_HEADS, page), lambda b, pt: (pt[b * n_pages + i], 0, 0))

    pages = range(n_pages)
    grid_spec = pltpu.PrefetchScalarGridSpec(
        num_scalar_prefetch=1,
        grid=(nb,),
        in_specs=[pl.BlockSpec((1, n_new, d), new_map), pl.BlockSpec((1, n_new, d), new_map),
                  pl.BlockSpec((1, n_new, d), new_map), pl.BlockSpec((1, n_new, FOX_HEADS), new_map),
                  *[kv_spec(i) for i in pages], *[kv_spec(i) for i in pages], *[lf_spec(i) for i in pages]],
        out_specs=pl.BlockSpec((1, n_new, d), new_map),
    )
    return pl.pallas_call(
        functools.partial(_fox_decode_kernel, n_pages=n_pages),
        grid_spec=grid_spec,
        out_shape=jax.ShapeDtypeStruct((nb, n_new, d), BF16),
        compiler_params=_params("parallel"),
        name="fox_attn_sample",
    )(page_table.reshape(-1), q, k_new, v_new, lf_new,
      *([cache_kt] * n_pages), *([cache_vt] * n_pages), *([cache_lft] * n_pages))


def _ml_proj_kernel(x_ref, g_ref, wk_ref, wqvo_ref, wg_ref, bg_ref,
                    q_ref, k_ref, v_ref, o_ref, gt_ref, *, feature_major):
    d = x_ref.shape[-1]
    hq = ML_HEADS * ML_DQK
    h = _rms(x_ref[...], g_ref[...]).astype(BF16)
    k_ref[...] = (_dot(h, wk_ref[...]) * (ML_DQK ** -0.5)).astype(BF16)
    if feature_major:
        p = _dot_nt(wqvo_ref[...], h)
        q_ref[0] = p[:hq].astype(BF16)
        v_ref[0] = p[hq:hq + d].astype(BF16)
        o_ref[0] = jax.nn.sigmoid(p[hq + d:])
    else:
        p = _dot(h, wqvo_ref[...])
        q_ref[...] = p[:, :hq].astype(BF16)
        v_ref[...] = p[:, hq:hq + d].astype(BF16)
        o_ref[...] = jax.nn.sigmoid(p[:, hq + d:])
    gates = _dot(h, wg_ref[...]) + bg_ref[...]
    gates = GATE_CAP * jnp.tanh(gates / GATE_CAP)
    lane = lax.broadcasted_iota(jnp.int32, gates.shape, 1)
    gates = jnp.where(lane < ML_HEADS, gates, _log_sigmoid(gates))
    gt_ref[...] = gates[:, :2 * ML_HEADS]


def _ml_proj(x, g, wk, wqvo, wg, bg, seq_len, *, feature_major):
    m, d = x.shape
    hq = ML_HEADS * ML_DQK
    n_seq = m // seq_len
    nl = seq_len // ROW_TILE
    row = lambda b, l: (b * nl + l, 0)
    const = lambda b, l: (0, 0)

    def out(width, dtype):
        if feature_major:
            return (pl.BlockSpec((1, width, ROW_TILE), lambda b, l: (b, 0, l)),
                    jax.ShapeDtypeStruct((n_seq, width, seq_len), dtype))
        return pl.BlockSpec((ROW_TILE, width), row), jax.ShapeDtypeStruct((m, width), dtype)

    (q_spec, q_shape), (v_spec, v_shape), (o_spec, o_shape) = out(hq, BF16), out(d, BF16), out(d, F32)
    return pl.pallas_call(
        functools.partial(_ml_proj_kernel, feature_major=feature_major),
        grid=(n_seq, nl),
        in_specs=[pl.BlockSpec((ROW_TILE, d), row), pl.BlockSpec((1, d), const),
                  _resident(wk.shape), _resident(wqvo.shape),
                  pl.BlockSpec((d, LANES), const), pl.BlockSpec((1, LANES), const)],
        out_specs=(q_spec, pl.BlockSpec((ROW_TILE, hq), row), v_spec, o_spec,
                   pl.BlockSpec((ROW_TILE, 2 * ML_HEADS), row)),
        out_shape=(q_shape, jax.ShapeDtypeStruct((m, hq), BF16), v_shape, o_shape,
                   jax.ShapeDtypeStruct((m, 2 * ML_HEADS), F32)),
        compiler_params=_params("parallel", "parallel"),
        name="ml_proj_prompt" if feature_major else "ml_proj_sample",
    )(x, g, wk, wqvo, wg, bg)


def _ml_chunk_kernel(qt_ref, k_ref, vt_ref, ogt_ref, gt_ref, gtt_ref, gnb_ref,
                     a_ref, c_ref, n_ref, m_ref, ct_sc, n_sc, m_sc):
    t = k_ref.shape[1]

    @pl.when(pl.program_id(1) == 0)
    def _():
        ct_sc[...] = jnp.zeros_like(ct_sc)
        n_sc[...] = jnp.zeros_like(n_sc)
        m_sc[...] = jnp.zeros_like(m_sc)

    gates = gt_ref[0]
    gates_t = gtt_ref[0]
    n_g = 2 * ML_HEADS
    ri = lax.broadcasted_iota(jnp.int32, (t, t), 0)
    ci = lax.broadcasted_iota(jnp.int32, (t, t), 1)
    causal_t = ri <= ci
    b_cols = _dot_mask(ci <= ri, gates)
    b_rows = _mask_dot(gates_t, causal_t)
    w_cols = jnp.concatenate([gates[:, :ML_HEADS] - b_cols[:, ML_HEADS:], jnp.zeros((t, ML_HEADS), F32)],
                             axis=1)
    left = jnp.concatenate([p.astype(F32) for p in _split3(w_cols)]
                           + [jnp.ones((t, n_g), F32), jnp.zeros((t, LANES - 4 * n_g), F32)],
                           axis=1).astype(BF16)
    right_pad = jnp.zeros((LANES - 4 * n_g, t), F32)
    sub = lax.broadcasted_iota(jnp.int32, (n_g, t), 0)
    sub8 = lax.broadcasted_iota(jnp.int32, (SUBLANES, t), 0)
    lane = lax.broadcasted_iota(jnp.int32, (t, LANES), 1)
    half = (lane < ML_DQK, lane >= ML_DQK)
    row_h = lax.broadcasted_iota(jnp.int32, (LANES, t), 0)
    half_rows = (row_h < ML_DQK, row_h >= ML_DQK)
    lane1 = lax.broadcasted_iota(jnp.int32, (1, LANES), 1)
    head_lane = lax.broadcasted_iota(jnp.int32, (1, ML_HEADS), 1)
    m_all = m_sc[...]
    m_out = jnp.zeros((1, ML_HEADS), F32)

    heads = range(ML_HEADS)
    pairs = range(ML_HEADS // 2)
    pj = lambda h: slice((h // 2) * LANES, (h // 2 + 1) * LANES)
    dv = lambda h: slice(h * ML_DV, (h + 1) * ML_DV)
    i_r = [gates_t[h:h + 1, :] for h in heads]
    b_r = [b_rows[ML_HEADS + h:ML_HEADS + h + 1, :] for h in heads]
    m_prev = [m_all[:, h:h + 1] for h in heads]
    ct = [ct_sc[j] for j in pairs]
    n2 = [n_sc[j] for j in pairs]
    km = [jnp.where(half[h % 2], k_ref[0, :, pj(h)], jnp.zeros((t, LANES), BF16)) for h in heads]

    logw, kq, cq, qn = [], [], [], []
    for h in heads:
        qt2 = qt_ref[0, pj(h), :]
        qtm = jnp.where(half_rows[h % 2], qt2, jnp.zeros_like(qt2))
        u_hi, u_mid, u_lo = (p.astype(F32) for p in _split3(b_r[h]))
        tail = jnp.where(sub == 0, u_hi, jnp.where(sub == 1, u_mid, jnp.where(sub == 2, u_lo, 0.0)))
        pick = (sub == h).astype(F32)
        right = jnp.concatenate([pick, pick, pick, tail, right_pad], axis=0).astype(BF16)
        logw.append(_dot(left, right))
        kq.append(_dot(km[h], qt2))
        cq.append(_dot(ct[h // 2].astype(BF16), qtm))
        qn.append(_dot(n2[h // 2].astype(BF16), qtm)[0:1, :])

    m_t, w_inter, a, a_sum, vtw, wk8, decay = [], [], [], [], [], [], []
    for h in heads:
        lw = jnp.where(causal_t, logw[h], NEG_INF)
        inter = b_r[h] + m_prev[h]
        m_t.append(jnp.maximum(inter, jnp.max(lw, axis=0, keepdims=True)))
        w_inter.append(jnp.exp(inter - m_t[h]))
        a_h = jnp.exp(lw - m_t[h]) * kq[h]
        a_sum.append(jnp.sum(a_h, axis=0, keepdims=True))
        a.append(a_h.astype(BF16))
        b_last = b_r[h][:, t - 1:t]
        m_new = m_t[h][:, t - 1:t]
        decay.append(jnp.exp(b_last + m_prev[h] - m_new))
        wk = jnp.exp(b_last - b_r[h] + i_r[h] - m_new)
        vtw.append((vt_ref[0, dv(h), :].astype(F32) * wk).astype(BF16))
        wk8.append(jnp.where(sub8 == 0, wk, 0.0).astype(BF16))
        m_out = jnp.where(head_lane == h, m_new, m_out)

    va = [_dot(vt_ref[0, dv(h), :], a[h]) for h in heads]
    c_add = [_dot(vtw[h], km[h]) for h in heads]
    n_add = [_dot(wk8[h], km[h]) for h in heads]

    for h in heads:
        hu = w_inter[h] * cq[h] + va[h]
        den = w_inter[h] * qn[h] + a_sum[h]
        r1 = 1.0 / jnp.maximum(jnp.abs(den), jnp.exp(-m_t[h]))
        ms = jnp.mean(hu * hu, axis=0, keepdims=True)
        hn = hu * (r1 * lax.rsqrt(r1 * r1 * ms + EPS)) * _lane_tile(gnb_ref[dv(h), :], t)
        a_ref[0, :, dv(h)] = (ogt_ref[0, dv(h), :] * hn).T.astype(BF16)
    for j in pairs:
        d2 = jnp.where(lane1 < ML_DQK, decay[2 * j], decay[2 * j + 1])
        ct_sc[j] = d2 * ct[j] + c_add[2 * j] + c_add[2 * j + 1]
        n_sc[j] = d2 * n2[j] + n_add[2 * j] + n_add[2 * j + 1]
    m_sc[...] = m_out

    @pl.when(pl.program_id(1) == pl.num_programs(1) - 1)
    def _():
        for j in range(ML_HEADS // 2):
            c_ref[0, j] = ct_sc[j].T
            n_ref[0, j:j + 1, :] = n_sc[j][0:1, :]
        m_ref[0] = m_out


def _ml_step_kernel(q_ref, k_ref, v_ref, og_ref, gt_ref, gtt_ref, gn_ref, c0_ref, n0_ref, m0_ref,
                    a_ref, c_ref, n_ref, m_ref):
    g, t = q_ref.shape[0], q_ref.shape[1]
    gates = gt_ref[...]
    gates_t = gtt_ref[...]
    ri = lax.broadcasted_iota(jnp.int32, (g, t, t), 1)
    ci = lax.broadcasted_iota(jnp.int32, (g, t, t), 2)
    causal = ci <= ri
    eye = jnp.broadcast_to(
        (lax.broadcasted_iota(jnp.int32, (ML_DQK, ML_DQK), 0)
         == lax.broadcasted_iota(jnp.int32, (ML_DQK, ML_DQK), 1)).astype(BF16)[None], (g, ML_DQK, ML_DQK))
    head_lane = lax.broadcasted_iota(jnp.int32, (g, 1, ML_HEADS), 2)
    m_all = m0_ref[...]
    m_out = jnp.zeros((g, 1, ML_HEADS), F32)

    heads = range(ML_HEADS)
    dv = lambda h: slice(h * ML_DV, (h + 1) * ML_DV)
    bmm = functools.partial(jnp.einsum, preferred_element_type=F32)
    q = [q_ref[:, :, h * ML_DQK:(h + 1) * ML_DQK] for h in heads]
    k = [k_ref[:, :, h * ML_DQK:(h + 1) * ML_DQK] for h in heads]
    v = [v_ref[:, :, dv(h)] for h in heads]
    c0 = [c0_ref[:, h] for h in heads]
    n0 = [n0_ref[:, h:h + 1, :] for h in heads]
    m_prev = [m_all[:, :, h:h + 1] for h in heads]

    qk = [bmm("gtd,gsd->gts", q[h], k[h]) for h in heads]
    qc = [bmm("gtd,gde->gte", q[h], c0[h].astype(BF16)) for h in heads]
    kt = [bmm("gdk,gsk->gds", eye, k[h]) for h in heads]

    m_t, w_inter, a, a_sum, kwt, kw_sum, decay = [], [], [], [], [], [], []
    for h in heads:
        fh = ML_HEADS + h
        i_c, f_c = gates[:, :, h:h + 1], gates[:, :, fh:fh + 1]
        i_r, f_r = gates_t[:, h:h + 1, :], gates_t[:, fh:fh + 1, :]
        b_c = jnp.sum(jnp.where(causal, f_r, 0.0), axis=2, keepdims=True)
        b_r = jnp.sum(jnp.where(ri <= ci, f_c, 0.0), axis=1, keepdims=True)
        logw = jnp.where(causal, b_c - b_r + i_r, NEG_INF)
        inter = b_c + m_prev[h]
        m_t.append(jnp.maximum(inter, jnp.max(logw, axis=2, keepdims=True)))
        w_inter.append(jnp.exp(inter - m_t[h]))
        a_h = jnp.exp(logw - m_t[h]) * qk[h]
        a_sum.append(jnp.sum(a_h, axis=2, keepdims=True))
        a.append(a_h.astype(BF16))
        b_last = b_c[:, t - 1:t, :]
        m_new = m_t[h][:, t - 1:t, :]
        decay.append(jnp.exp(b_last + m_prev[h] - m_new))
        kw = jnp.exp(b_last - b_c + i_c - m_new) * k[h].astype(F32)
        kw_sum.append(jnp.sum(kw, axis=1, keepdims=True))
        kwt.append((kt[h] * jnp.exp(b_last - b_r + i_r - m_new)).astype(BF16))
        m_out = jnp.where(head_lane == h, m_new, m_out)

    av = [bmm("gts,gse->gte", a[h], v[h]) for h in heads]
    c_add = [bmm("gds,gse->gde", kwt[h], v[h]) for h in heads]

    for h in heads:
        num = w_inter[h] * qc[h] + av[h]
        qn = jnp.sum(q[h].astype(F32) * n0[h], axis=2, keepdims=True)
        den = w_inter[h] * qn + a_sum[h]
        hs = num / jnp.maximum(jnp.abs(den), jnp.exp(-m_t[h]))
        hn = hs * lax.rsqrt(jnp.mean(hs * hs, axis=2, keepdims=True) + EPS) * gn_ref[:, dv(h)]
        a_ref[:, :, dv(h)] = (og_ref[:, :, dv(h)] * hn).astype(BF16)
        c_ref[:, h] = decay[h] * c0[h] + c_add[h]
        n_ref[:, h:h + 1, :] = decay[h] * n0[h] + kw_sum[h]
    m_ref[...] = m_out


def _ml_chunk(qt, k, vt, ogt, gates, gates_t, gnb, chunk):
    nb, d, seq_len = vt.shape
    hq = k.shape[2]
    pairs = ML_HEADS // 2
    tok = lambda width: pl.BlockSpec((1, chunk, width), lambda b, c: (b, c, 0))
    tok_t = lambda width: pl.BlockSpec((1, width, chunk), lambda b, c: (b, 0, c))
    return pl.pallas_call(
        _ml_chunk_kernel,
        grid=(nb, seq_len // chunk),
        in_specs=[tok_t(hq), tok(hq), tok_t(d), tok_t(d), tok(2 * ML_HEADS), tok_t(2 * ML_HEADS),
                  pl.BlockSpec((d, LANES), lambda b, c: (0, 0))],
        out_specs=(tok(d), pl.BlockSpec((1, pairs, LANES, LANES), lambda b, c: (b, 0, 0, 0)),
                   pl.BlockSpec((1, pairs, LANES), lambda b, c: (b, 0, 0)),
                   pl.BlockSpec((1, 1, ML_HEADS), lambda b, c: (b, 0, 0))),
        out_shape=(jax.ShapeDtypeStruct((nb, seq_len, d), BF16),
                   jax.ShapeDtypeStruct((nb, pairs, LANES, LANES), F32),
                   jax.ShapeDtypeStruct((nb, pairs, LANES), F32),
                   jax.ShapeDtypeStruct((nb, 1, ML_HEADS), F32)),
        scratch_shapes=[pltpu.VMEM((pairs, ML_DV, LANES), F32), pltpu.VMEM((pairs, SUBLANES, LANES), F32),
                        pltpu.VMEM((1, ML_HEADS), F32)],
        compiler_params=_params("parallel", "arbitrary"),
        name="ml_chunk",
    )(qt, k, vt, ogt, gates, gates_t, gnb)


def _ml_step(q, k, v, og, gates, gates_t, gn, c0, n0, m0):
    nb, t, d = v.shape
    hq = q.shape[2]
    g = ML_STEP_SEQS
    lead = lambda *rest: pl.BlockSpec((g, *rest), lambda i: (i,) + (0,) * len(rest))
    state = (lead(ML_HEADS, ML_DQK, ML_DV), lead(ML_HEADS, ML_DQK), lead(1, ML_HEADS))
    return pl.pallas_call(
        _ml_step_kernel,
        grid=(nb // g,),
        in_specs=[lead(t, hq), lead(t, hq), lead(t, d), lead(t, d), lead(t, 2 * ML_HEADS),
                  lead(2 * ML_HEADS, t), pl.BlockSpec((1, d), lambda i: (0, 0)), *state],
        out_specs=(lead(t, d), *state),
        out_shape=(jax.ShapeDtypeStruct((nb, t, d), BF16), jax.ShapeDtypeStruct(c0.shape, F32),
                   jax.ShapeDtypeStruct(n0.shape, F32), jax.ShapeDtypeStruct(m0.shape, F32)),
        compiler_params=_params("parallel"),
        name="ml_step",
    )(q, k, v, og, gates, gates_t, gn, c0, n0, m0)


def _pad_cols(w, width):
    return jnp.pad(w, ((0, 0), (0, width - w.shape[1])))


def kernel(x_prompt, x_sample, state_conv, cache_k, cache_v, cache_logf, page_table, state_C, state_n, state_m, norm_mix, norm_ffn, norm_final, w_conv_in, w_conv, w_conv_out, w_fox_in, b_fox_f, w_fox_out, w_ml_in, b_ml_gates, g_ml_norm, w_ml_out, w_ffn_gu, w_ffn_down):
    bp, lp, d = x_prompt.shape
    bs, ls, _ = x_sample.shape
    depth = norm_mix.shape[0]
    dff = w_ffn_down.shape[1]
    assert d == D_MODEL and ls == SUBLANES
    xp = x_prompt.reshape(bp * lp, d)
    xs = x_sample.reshape(bs * ls, d)
    gf = norm_final.reshape(1, d)
    hq = ML_HEADS * ML_DQK
    pairs = ML_HEADS // 2
    out = {name: [] for name in ("conv_p", "conv_s", "kp", "vp", "lfp", "ks", "vs", "lfs",
                                 "cp", "np", "mp", "cs", "ns", "ms")}

    for i in range(depth):
        kind, j = i % 3, i // 3
        gm = norm_mix[i].reshape(1, d)
        if kind == 0:
            win = w_conv_in[j].astype(BF16)
            wo = w_conv_out[j].astype(BF16)
            ap, sp = _conv_mixer(xp, gm, win, w_conv[j], jnp.zeros((bp, CONV_W - 1, d), F32), lp)
            as_, ss = _conv_mixer(xs, gm, win, w_conv[j], state_conv[j], ls)
            out["conv_p"].append(sp)
            out["conv_s"].append(ss)
        elif kind == 1:
            wqkv = w_fox_in[j][:, :3 * d].astype(BF16)
            wf = _pad_cols(w_fox_in[j][:, 3 * d:], LANES).astype(BF16)
            bf = _pad_cols(b_fox_f[j].reshape(1, FOX_HEADS), LANES)
            wo = w_fox_out[j].astype(BF16)
            qtb, kt, vt, kb, vtb, lft, c, ct = _fox_proj_prompt(
                xp, gm, wqkv.T, wf, wf[:, :FOX_HEADS].T, bf, b_fox_f[j].reshape(FOX_HEADS, 1), lp)
            ap = _fox_attn_prompt(qtb, kb, vtb, c, ct, lp)
            heads_last = lambda t: jnp.transpose(t.reshape(bp, FOX_HEADS, FOX_HEAD_DIM, lp), (0, 3, 1, 2))
            out["kp"].append(heads_last(kt))
            out["vp"].append(heads_last(vt))
            out["lfp"].append(jnp.swapaxes(lft, 1, 2))
            q, k, v, lf = _fox_proj_sample(xs, gm, wqkv, wf, bf)
            as_ = _fox_attn_sample(
                q.reshape(bs, ls, d), k.reshape(bs, ls, d), v.reshape(bs, ls, d),
                lf.reshape(bs, ls, FOX_HEADS), jnp.transpose(cache_k[j], (0, 2, 3, 1)),
                jnp.transpose(cache_v[j], (0, 2, 3, 1)), jnp.transpose(cache_logf[j], (0, 2, 1)),
                page_table).reshape(bs * ls, d)
            out["ks"].append(k.reshape(bs, ls, FOX_HEADS, FOX_HEAD_DIM))
            out["vs"].append(v.reshape(bs, ls, FOX_HEADS, FOX_HEAD_DIM))
            out["lfs"].append(lf.reshape(bs, ls, FOX_HEADS))
        else:
            w = w_ml_in[j]
            wk = w[:, hq:2 * hq].astype(BF16)
            wqvo = jnp.concatenate([w[:, :hq], w[:, 2 * hq:2 * hq + 2 * d]], axis=1).astype(BF16)
            wg = _pad_cols(w[:, 2 * hq + 2 * d:], LANES).astype(BF16)
            bg = _pad_cols(b_ml_gates[j].reshape(1, 2 * ML_HEADS), LANES)
            gn = g_ml_norm[j].reshape(1, d)
            wo = w_ml_out[j].astype(BF16)

            def tokens(nb, seq_len, *arrays):
                return [t.reshape(nb, seq_len, t.shape[-1]) for t in arrays]

            qt, k, vt, ogt, gates = _ml_proj(xp, gm, wk, wqvo.T, wg, bg, lp, feature_major=True)
            k, gates = tokens(bp, lp, k, gates)
            a, c1, n1, m1 = _ml_chunk(qt, k, vt, ogt, gates, jnp.swapaxes(gates, 1, 2),
                                      jnp.broadcast_to(gn.reshape(d, 1), (d, LANES)), ML_CHUNK_PROMPT)
            ap = a.reshape(bp * lp, d)
            out["cp"].append(c1.reshape(bp, ML_HEADS, ML_DQK, ML_DV))
            out["np"].append(n1.reshape(bp, ML_HEADS, ML_DQK))
            out["mp"].append(m1.reshape(bp, ML_HEADS))

            q, k, v, og, gates = _ml_proj(xs, gm, wk, wqvo, wg, bg, ROW_TILE, feature_major=False)
            q, k, v, og, gates = tokens(bs, ls, q, k, v, og, gates)
            a, c1, n1, m1 = _ml_step(q, k, v, og, gates, jnp.swapaxes(gates, 1, 2), gn,
                                     state_C[j], state_n[j], state_m[j].reshape(bs, 1, ML_HEADS))
            as_ = a.reshape(bs * ls, d)
            out["cs"].append(c1); out["ns"].append(n1); out["ms"].append(m1.reshape(bs, ML_HEADS))

        gffn = norm_ffn[i].reshape(1, d)
        wg_ffn = w_ffn_gu[i][:, :dff].astype(BF16)
        wu_ffn = w_ffn_gu[i][:, dff:].astype(BF16)
        wd_ffn = w_ffn_down[i].astype(BF16)
        last = i == depth - 1
        xp = _post(xp, ap, wo, gffn, wg_ffn, wu_ffn, wd_ffn, gf, final_norm=last)
        xs = _post(xs, as_, wo, gffn, wg_ffn, wu_ffn, wd_ffn, gf, final_norm=last)

    st = lambda name: jnp.stack(out[name])
    return (xp.reshape(bp, lp, d), xs.reshape(bs, ls, d), st("conv_p"), st("conv_s"),
            st("kp"), st("vp"), st("lfp"), st("ks"), st("vs"), st("lfs"),
            st("cp"), st("np"), st("mp"), st("cs"), st("ns"), st("ms"))
```

```python
import functools

import jax
import jax.numpy as jnp
from jax import lax
from jax.experimental import pallas as pl
from jax.experimental.pallas import tpu as pltpu

F32 = jnp.float32
BF16 = jnp.bfloat16
HIGHEST = lax.Precision.HIGHEST

D_MODEL = 1024
CONV_W = 3
FOX_HEADS = 16
FOX_HEAD_DIM = D_MODEL // FOX_HEADS
ML_HEADS = 8
ML_DV = D_MODEL // ML_HEADS
ML_DQK = ML_DV // 2
GATE_CAP = 15.0
EPS = 1e-6
NEG_INF = -1e30
LOG2E = 1.4426950408889634

LANES = 128
SUBLANES = 8
ROW_TILE = 1024
FOX_PROJ_TILE = 512
POST_ROW_TILE = 1024
FFN_CHUNK = 256
CONV_CHUNK = 256
ATTN_TQ = 512
ML_CHUNK_PROMPT = 128
ML_STEP_SEQS = 16
VMEM_LIMIT = 56 * 1024 * 1024


def _params(*sem):
    return pltpu.CompilerParams(dimension_semantics=sem, vmem_limit_bytes=VMEM_LIMIT)


def _resident(shape):
    return pl.BlockSpec(shape, lambda *_: (0, 0), pipeline_mode=pl.Buffered(1))


def _rms(x, g):
    return x * lax.rsqrt(jnp.mean(x * x, axis=-1, keepdims=True) + EPS) * g


def _log_sigmoid(x):
    return jnp.minimum(x, 0.0) - jnp.log1p(jnp.exp(-jnp.abs(x)))


def _dot(a, b):
    return jnp.dot(a, b, preferred_element_type=F32)


def _dot_hi(a, b):
    return jnp.dot(a, b, precision=HIGHEST, preferred_element_type=F32)


def _dot_nt(a, b, precision=None):
    return lax.dot_general(a, b, (((1,), (1,)), ((), ())), precision=precision,
                           preferred_element_type=F32)


def _dot_tn(a, b):
    return lax.dot_general(a, b, (((0,), (0,)), ((), ())), preferred_element_type=F32)


def _lane_tile(x, width):
    return jnp.tile(x, (1, width // LANES))


def _post_kernel(x_ref, a_ref, wo_ref, g_ref, wg_ref, wu_ref, wd_ref, gf_ref, o_ref, *, final_norm):
    x1 = x_ref[...] + _dot(a_ref[...], wo_ref[...])
    h = _rms(x1, g_ref[...]).astype(BF16)
    y = x1
    for c in range(0, wg_ref.shape[1], FFN_CHUNK):
        g = _dot(h, wg_ref[:, c:c + FFN_CHUNK])
        u = _dot(h, wu_ref[:, c:c + FFN_CHUNK])
        y = y + _dot(((g * jax.nn.sigmoid(g)) * u).astype(BF16), wd_ref[c:c + FFN_CHUNK, :])
    o_ref[...] = _rms(y, gf_ref[...]) if final_norm else y


def _post(x, a, wo, g, wg, wu, wd, gf, *, final_norm):
    m, d = x.shape
    dff = wg.shape[1]
    tm = min(POST_ROW_TILE, m)
    assert dff % FFN_CHUNK == 0 and m % tm == 0
    row = pl.BlockSpec((tm, d), lambda i: (i, 0))
    return pl.pallas_call(
        functools.partial(_post_kernel, final_norm=final_norm),
        grid=(m // tm,),
        in_specs=[row, row, _resident((d, d)), _resident((1, d)), _resident((d, dff)), _resident((d, dff)),
                  _resident((dff, d)), _resident((1, d))],
        out_specs=row,
        out_shape=jax.ShapeDtypeStruct((m, d), F32),
        compiler_params=_params("parallel"),
        name="post_ffn",
    )(x, a, wo, g, wg, wu, wd, gf)


def _conv_columns(x_ref, g_ref, win_ref, wc_ref, z_ref, rig, prev_rows, keep_tail):
    d = x_ref.shape[-1]
    h = _rms(x_ref[...], g_ref[...]).astype(BF16)
    for c in range(0, d, CONV_CHUNK):
        cols = slice(c, c + CONV_CHUNK)
        bg = _dot(h, win_ref[:, c:c + CONV_CHUNK])
        u = (_dot(h, win_ref[:, d + c:d + c + CONV_CHUNK])
             * _dot(h, win_ref[:, 2 * d + c:2 * d + c + CONV_CHUNK]))
        p0, p1 = prev_rows(cols)
        s1 = jnp.where(rig == 0, p1, pltpu.roll(u, 1, axis=0))
        s2 = jnp.where(rig == 0, p0, jnp.where(rig == 1, p1, pltpu.roll(u, 2, axis=0)))
        conv = wc_ref[0:1, cols] * s2 + wc_ref[1:2, cols] * s1 + wc_ref[2:3, cols] * u
        z_ref[:, cols] = (bg * conv).astype(BF16)
        keep_tail(cols, u)


def _conv_prompt_kernel(x_ref, g_ref, win_ref, wc_ref, prev_ref, z_ref, st_ref, carry_sc):
    @pl.when(pl.program_id(1) == 0)
    def _():
        carry_sc[...] = prev_ref[0]

    r = x_ref.shape[0]
    rig = lax.broadcasted_iota(jnp.int32, (r, CONV_CHUNK), 0)

    def keep_tail(cols, u):
        tail = u[r - (CONV_W - 1):, :]
        carry_sc[:, cols] = tail
        st_ref[0, :, cols] = tail

    _conv_columns(x_ref, g_ref, win_ref, wc_ref, z_ref, rig,
                  lambda cols: (carry_sc[0:1, cols], carry_sc[1:2, cols]), keep_tail)


def _conv_sample_kernel(x_ref, g_ref, win_ref, wc_ref, prev_ref, z_ref, st_ref):
    r = x_ref.shape[0]
    n_seq = r // SUBLANES
    rig = lax.broadcasted_iota(jnp.int32, (r, CONV_CHUNK), 0) & (SUBLANES - 1)

    def prev_rows(cols):
        prev = prev_ref[:, :, cols]
        rows = lambda i: jnp.broadcast_to(prev[:, i:i + 1, :], (n_seq, SUBLANES, CONV_CHUNK)).reshape(r, CONV_CHUNK)
        return rows(0), rows(1)

    def keep_tail(cols, u):
        st_ref[:, :, cols] = u.reshape(n_seq, SUBLANES, CONV_CHUNK)[:, SUBLANES - (CONV_W - 1):, :]

    _conv_columns(x_ref, g_ref, win_ref, wc_ref, z_ref, rig, prev_rows, keep_tail)


def _conv_mixer(x, g, win, wc, prev, seq_len):
    m, d = x.shape
    n_seq = m // seq_len
    common_in = [_resident((d, 3 * d)), _resident((CONV_W, d))]
    out_shape = (jax.ShapeDtypeStruct((m, d), BF16), jax.ShapeDtypeStruct((n_seq, CONV_W - 1, d), F32))
    if seq_len == SUBLANES:
        per = ROW_TILE // SUBLANES
        return pl.pallas_call(
            _conv_sample_kernel,
            grid=(m // ROW_TILE,),
            in_specs=[pl.BlockSpec((ROW_TILE, d), lambda i: (i, 0)), pl.BlockSpec((1, d), lambda i: (0, 0)),
                      *common_in, pl.BlockSpec((per, CONV_W - 1, d), lambda i: (i, 0, 0))],
            out_specs=(pl.BlockSpec((ROW_TILE, d), lambda i: (i, 0)),
                       pl.BlockSpec((per, CONV_W - 1, d), lambda i: (i, 0, 0))),
            out_shape=out_shape,
            compiler_params=_params("parallel"),
            name="conv_sample",
        )(x, g, win, wc, prev)
    assert seq_len % ROW_TILE == 0
    nl = seq_len // ROW_TILE
    return pl.pallas_call(
        _conv_prompt_kernel,
        grid=(n_seq, nl),
        in_specs=[pl.BlockSpec((ROW_TILE, d), lambda b, l: (b * nl + l, 0)),
                  pl.BlockSpec((1, d), lambda b, l: (0, 0)),
                  *common_in, pl.BlockSpec((1, CONV_W - 1, d), lambda b, l: (b, 0, 0))],
        out_specs=(pl.BlockSpec((ROW_TILE, d), lambda b, l: (b * nl + l, 0)),
                   pl.BlockSpec((1, CONV_W - 1, d), lambda b, l: (b, 0, 0))),
        out_shape=out_shape,
        scratch_shapes=[pltpu.VMEM((CONV_W - 1, d), F32)],
        compiler_params=_params("parallel", "arbitrary"),
        name="conv_prompt",
    )(x, g, win, wc, prev)


def _fox_proj_kernel(x_ref, g_ref, wqkv_ref, wf_ref, bf_ref, q_ref, k_ref, v_ref, lf_ref):
    d = x_ref.shape[-1]
    h = _rms(x_ref[...], g_ref[...]).astype(BF16)
    p = _dot(h, wqkv_ref[...])
    logf = _log_sigmoid(_dot(h, wf_ref[...]) + bf_ref[...])
    q_ref[...] = p[:, :d] * (FOX_HEAD_DIM ** -0.5)
    k_ref[...] = p[:, d:2 * d]
    v_ref[...] = p[:, 2 * d:]
    lf_ref[...] = logf[:, :FOX_HEADS]


def _fox_proj_sample(x, g, wqkv, wf, bf):
    m, d = x.shape
    row = lambda i: (i, 0)
    const = lambda i: (0, 0)
    big = pl.BlockSpec((ROW_TILE, d), row)
    f32_out = jax.ShapeDtypeStruct((m, d), F32)
    return pl.pallas_call(
        _fox_proj_kernel,
        grid=(m // ROW_TILE,),
        in_specs=[big, pl.BlockSpec((1, d), const), _resident((d, 3 * d)),
                  pl.BlockSpec((d, LANES), const), pl.BlockSpec((1, LANES), const)],
        out_specs=(big, big, big, pl.BlockSpec((ROW_TILE, FOX_HEADS), row)),
        out_shape=(f32_out, f32_out, f32_out, jax.ShapeDtypeStruct((m, FOX_HEADS), F32)),
        compiler_params=_params("parallel"),
        name="fox_proj_sample",
    )(x, g, wqkv, wf, bf)


def _fox_proj_prompt_kernel(x_ref, g_ref, wqkvt_ref, wf_ref, wft_ref, bf_ref, bft_ref,
                            qtb_ref, kt_ref, vt_ref, kb_ref, vtb_ref, lft_ref, c_ref, ct_ref,
                            carry_sc, carryt_sc):
    d = x_ref.shape[-1]
    r = x_ref.shape[0]
    h = _rms(x_ref[...], g_ref[...]).astype(BF16)
    qkv = _dot_nt(wqkvt_ref[...], h)
    qtb_ref[0] = (qkv[:d] * (FOX_HEAD_DIM ** -0.5 * LOG2E)).astype(BF16)
    kt_ref[0] = qkv[d:2 * d]
    vt_ref[0] = qkv[2 * d:]
    kb_ref[...] = qkv[d:2 * d].T.astype(BF16)
    vtb_ref[0] = qkv[2 * d:].astype(BF16)
    logf = _log_sigmoid(_dot(h, wf_ref[...]) + bf_ref[...])
    logf_t = _log_sigmoid(_dot_nt(wft_ref[...], h) + bft_ref[...])
    lft_ref[0] = logf_t

    @pl.when(pl.program_id(1) == 0)
    def _():
        carry_sc[...] = jnp.zeros_like(carry_sc)
        carryt_sc[...] = jnp.zeros_like(carryt_sc)

    ri = lax.broadcasted_iota(jnp.int32, (r, r), 0)
    ci = lax.broadcasted_iota(jnp.int32, (r, r), 1)
    c = carry_sc[...] + _dot_mask(ci <= ri, logf)
    carry_sc[...] = c[r - 1:r, :]
    c_ref[...] = c[:, :FOX_HEADS]
    ct = _lane_tile(carryt_sc[...], r) + _mask_dot(logf_t, ri <= ci)
    carryt_sc[...] = jnp.broadcast_to(ct[:, r - 1:r], carryt_sc.shape)
    ct_ref[0] = ct


def _fox_proj_prompt(x, g, wqkvt, wf, wft, bf, bft, seq_len):
    m, d = x.shape
    n_seq = m // seq_len
    tile = FOX_PROJ_TILE
    nl = seq_len // tile
    row = lambda b, l: (b * nl + l, 0)
    const = lambda b, l: (0, 0)
    feat = lambda width: pl.BlockSpec((1, width, tile), lambda b, l: (b, 0, l))
    feat_shape = lambda width, dtype: jax.ShapeDtypeStruct((n_seq, width, seq_len), dtype)
    return pl.pallas_call(
        _fox_proj_prompt_kernel,
        grid=(n_seq, nl),
        in_specs=[pl.BlockSpec((tile, d), row), pl.BlockSpec((1, d), const), _resident((3 * d, d)),
                  pl.BlockSpec((d, LANES), const), pl.BlockSpec((FOX_HEADS, d), const),
                  pl.BlockSpec((1, LANES), const), pl.BlockSpec((FOX_HEADS, 1), const)],
        out_specs=(feat(d), feat(d), feat(d), pl.BlockSpec((tile, d), row), feat(d),
                   feat(FOX_HEADS), pl.BlockSpec((tile, FOX_HEADS), row), feat(FOX_HEADS)),
        out_shape=(feat_shape(d, BF16), feat_shape(d, F32), feat_shape(d, F32),
                   jax.ShapeDtypeStruct((m, d), BF16), feat_shape(d, BF16),
                   feat_shape(FOX_HEADS, F32), jax.ShapeDtypeStruct((m, FOX_HEADS), F32),
                   feat_shape(FOX_HEADS, F32)),
        scratch_shapes=[pltpu.VMEM((1, LANES), F32), pltpu.VMEM((FOX_HEADS, LANES), F32)],
        compiler_params=_params("parallel", "arbitrary"),
        name="fox_proj_prompt",
    )(x, g, wqkvt, wf, wft, bf, bft)


def _fox_attn_kernel(qt_ref, k_ref, vt_ref, c_ref, ct_ref, o_ref, kaug_sc, vaug_sc, m_sc, acc_sc, *, tq):
    hp = pl.program_id(1)
    seq = k_ref.shape[0]
    hd = FOX_HEAD_DIM
    n_aug = 2 * SUBLANES
    n_bias = 3
    one_row = jnp.where(lax.broadcasted_iota(jnp.int32, (n_aug, seq), 0) == 0, 1.0, 0.0).astype(BF16)

    for hh in range(2):
        vaug_sc[hh] = jnp.concatenate([vt_ref[0, hh * hd:(hh + 1) * hd, :], one_row], axis=0)
    c3 = jnp.concatenate([p.astype(F32) for p in _split3(c_ref[...] * LOG2E)]
                         + [jnp.zeros((seq, LANES - n_bias * FOX_HEADS), F32)], axis=1).astype(BF16)
    src = lax.broadcasted_iota(jnp.int32, (LANES, LANES), 0)
    dst = lax.broadcasted_iota(jnp.int32, (LANES, LANES), 1)
    term = src >> (FOX_HEADS.bit_length() - 1)
    lane = lax.broadcasted_iota(jnp.int32, (seq, LANES), 1)
    k2 = k_ref[...]
    for hh in range(2):
        off = hd * (1 - hh)
        place = jnp.where((src & (FOX_HEADS - 1)) == 2 * hp + hh,
                          jnp.where(dst == off + term, -1.0, 0.0), 0.0).astype(BF16)
        ones = jnp.where(lane >= off + n_bias, jnp.where(lane < off + 2 * n_bias, 1.0, 0.0), 0.0)
        aug = (_dot(c3, place) + ones).astype(BF16)
        own = (lane >= hd) if hh else (lane < hd)
        kaug_sc[hh] = jnp.where(own, k2, aug)

    sub = lax.broadcasted_iota(jnp.int32, (n_aug, tq), 0)
    visible = (lax.broadcasted_iota(jnp.int32, (tq, tq), 0)
               <= lax.broadcasted_iota(jnp.int32, (tq, tq), 1))
    rest = jnp.zeros((hd - n_aug, tq), BF16)

    for n in range(seq // tq):
        cols = slice(n * tq, (n + 1) * tq)
        qt2 = qt_ref[0, :, cols]
        qa = []
        for hh in range(2):
            c_hi, c_mid, c_lo = (p.astype(F32)
                                 for p in _split3(ct_ref[0, pl.ds(2 * hp + hh, 1), cols] * LOG2E))
            piece = jnp.where(sub < n_bias, 1.0, jnp.where(sub == n_bias, c_hi, jnp.where(
                sub == n_bias + 1, c_mid, jnp.where(sub == n_bias + 2, c_lo, 0.0)))).astype(BF16)
            qa.append(jnp.concatenate([qt2[:hd], piece, rest] if hh == 0 else [piece, rest, qt2[hd:]], axis=0))
            m_sc[hh] = jnp.full((1, tq), NEG_INF, F32)
        acc_sc[...] = jnp.zeros_like(acc_sc)

        def scores(ki):
            return [_dot(kaug_sc[hh, ki * tq:(ki + 1) * tq, :], qa[hh]) for hh in range(2)]

        def absorb(ki, s2, diagonal):
            p2, alpha2 = [], []
            for hh in range(2):
                s = jnp.where(visible, s2[hh], NEG_INF) if diagonal else s2[hh]
                m_prev = m_sc[hh]
                m_new = jnp.maximum(m_prev, jnp.max(s, axis=0, keepdims=True))
                alpha2.append(jnp.exp2(m_prev - m_new))
                p2.append(jnp.exp2(s - m_new).astype(BF16))
                m_sc[hh] = m_new
            pv2 = [_dot(vaug_sc[hh, :, ki * tq:(ki + 1) * tq], p2[hh]) for hh in range(2)]
            for hh in range(2):
                acc_sc[hh] = alpha2[hh] * acc_sc[hh] + pv2[hh]

        s_next = scores(0)
        for ki in range(n + 1):
            s_cur = s_next
            if ki < n:
                s_next = scores(ki + 1)
            absorb(ki, s_cur, ki == n)

        o_t = jnp.concatenate([acc_sc[hh, :hd, :] / acc_sc[hh, hd:hd + 1, :] for hh in range(2)], axis=0)
        o_ref[cols, :] = o_t.T.astype(BF16)


def _fox_attn_prompt(qtb, kb, vtb, c, ct, seq_len):
    n_seq, d, _ = qtb.shape
    tq = ATTN_TQ
    pair = pl.BlockSpec((1, LANES, seq_len), lambda b, hp: (b, hp, 0))
    return pl.pallas_call(
        functools.partial(_fox_attn_kernel, tq=tq),
        grid=(n_seq, d // LANES),
        in_specs=[pair, pl.BlockSpec((seq_len, LANES), lambda b, hp: (b, hp)), pair,
                  pl.BlockSpec((seq_len, FOX_HEADS), lambda b, hp: (b, 0)),
                  pl.BlockSpec((1, FOX_HEADS, seq_len), lambda b, hp: (b, 0, 0))],
        out_specs=pl.BlockSpec((seq_len, LANES), lambda b, hp: (b, hp)),
        out_shape=jax.ShapeDtypeStruct((n_seq * seq_len, d), BF16),
        scratch_shapes=[pltpu.VMEM((2, seq_len, LANES), BF16),
                        pltpu.VMEM((2, FOX_HEAD_DIM + 2 * SUBLANES, seq_len), BF16),
                        pltpu.VMEM((2, 1, tq), F32),
                        pltpu.VMEM((2, FOX_HEAD_DIM + 2 * SUBLANES, tq), F32)],
        compiler_params=_params("parallel", "parallel"),
        name="fox_attn_prompt",
    )(qtb, kb, vtb, c, ct)


def _split3(x):
    hi = x.astype(BF16)
    r1 = x - hi.astype(F32)
    mid = r1.astype(BF16)
    return hi, mid, (r1 - mid.astype(F32)).astype(BF16)


def _mask_dot(x, mask):
    mask = mask.astype(BF16)
    hi, mid, lo = _split3(x)
    return _dot(hi, mask) + _dot(mid, mask) + _dot(lo, mask)


def _dot_mask(mask, x, nt=False):
    mask = mask.astype(BF16)
    dot = _dot_nt if nt else _dot
    hi, mid, lo = _split3(x)
    return dot(mask, hi) + dot(mask, mid) + dot(mask, lo)


def _fox_decode_kernel(pt_ref, q_ref, kn_ref, vn_ref, lfn_ref, *refs, n_pages):
    del pt_ref
    kt_refs, vt_refs, lft_refs = refs[:n_pages], refs[n_pages:2 * n_pages], refs[2 * n_pages:3 * n_pages]
    o_ref = refs[3 * n_pages]
    page = kt_refs[0].shape[-1]
    n_new, d = q_ref.shape[1], q_ref.shape[2]
    rows = FOX_HEADS * n_new
    assert rows == page == LANES and n_new == SUBLANES
    ri = lax.broadcasted_iota(jnp.int32, (rows, page), 0)
    ki = lax.broadcasted_iota(jnp.int32, (rows, page), 1)
    tok_bits = n_new.bit_length() - 1
    dim_bits = FOX_HEAD_DIM.bit_length() - 1
    head_of_row = ((lax.broadcasted_iota(jnp.int32, (rows, FOX_HEADS), 0) >> tok_bits)
                   == lax.broadcasted_iota(jnp.int32, (rows, FOX_HEADS), 1))
    own_cols = ((lax.broadcasted_iota(jnp.int32, (rows, d), 0) >> tok_bits)
                == (lax.broadcasted_iota(jnp.int32, (rows, d), 1) >> dim_bits))
    qbd = jnp.where(own_cols, jnp.tile(q_ref[0], (FOX_HEADS, 1)), 0.0).astype(BF16)

    pad = page - n_new
    kn = jnp.concatenate([kn_ref[0], jnp.zeros((pad, d), F32)], axis=0).astype(BF16)
    vn = jnp.concatenate([vn_ref[0], jnp.zeros((pad, d), F32)], axis=0).astype(BF16)
    lfn = jnp.concatenate([lfn_ref[0], jnp.zeros((pad, FOX_HEADS), F32)], axis=0)
    lf_rows = _dot_mask(head_of_row, lfn, nt=True)
    cum = _mask_dot(lf_rows, ri <= ki)
    t_of_row = ri & (n_new - 1)
    a = jnp.sum(jnp.where(ki == t_of_row, cum, 0.0), axis=1, keepdims=True)
    s_new = jnp.where(ki <= t_of_row, _dot_nt(qbd, kn) + a - cum, NEG_INF)

    lft = jnp.concatenate([r[...] for r in lft_refs], axis=0)
    sfx_in = _mask_dot(lft, ri >= ki)
    sfx_ex = sfx_in - lft
    later = jnp.zeros((FOX_HEADS, 1), F32)
    bias = [None] * n_pages
    for i in reversed(range(n_pages)):
        bias[i] = sfx_ex[i * FOX_HEADS:(i + 1) * FOX_HEADS, :] + later
        later = later + sfx_in[i * FOX_HEADS:(i + 1) * FOX_HEADS, 0:1]
    bias = _dot_mask(head_of_row, jnp.concatenate(bias, axis=1))
    s_old = jnp.concatenate(
        [_dot(qbd, r[...].reshape(d, page).astype(BF16)) for r in kt_refs], axis=1) + bias + a

    m = jnp.maximum(jnp.max(s_new, axis=1, keepdims=True), jnp.max(s_old, axis=1, keepdims=True))
    p_new = jnp.exp(s_new - m)
    p_old = jnp.exp(s_old - m)
    denom = jnp.sum(p_new, axis=1, keepdims=True) + jnp.sum(p_old, axis=1, keepdims=True)
    p_old = p_old.astype(BF16)
    acc = _dot(p_new.astype(BF16), vn)
    for i, r in enumerate(vt_refs):
        acc = acc + _dot_nt(p_old[:, i * page:(i + 1) * page], r[...].reshape(d, page).astype(BF16))
    o = jnp.where(own_cols, acc / denom, 0.0).reshape(FOX_HEADS, n_new, d)
    o_ref[0] = jnp.sum(o, axis=0).astype(BF16)


def _fox_attn_sample(q, k_new, v_new, lf_new, cache_kt, cache_vt, cache_lft, page_table):
    nb, n_new, d = q.shape
    n_pages = page_table.shape[1]
    page = cache_kt.shape[-1]

    def new_map(b, pt):
        return (b, 0, 0)

    def kv_spec(i):
        return pl.BlockSpec((None, FOX_HEADS, FOX_HEAD_DIM, page), lambda b, pt: (pt[b * n_pages + i], 0, 0, 0))

    def lf_spec(i):
        return pl.BlockSpec((None, FOX_HEADS, page), lambda b, pt: (pt[b * n_pages + i], 0, 0))

    pages = range(n_pages)
    grid_spec = pltpu.PrefetchScalarGridSpec(
        num_scalar_prefetch=1,
        grid=(nb,),
        in_specs=[pl.BlockSpec((1, n_new, d), new_map), pl.BlockSpec((1, n_new, d), new_map),
                  pl.BlockSpec((1, n_new, d), new_map), pl.BlockSpec((1, n_new, FOX_HEADS), new_map),
                  *[kv_spec(i) for i in pages], *[kv_spec(i) for i in pages], *[lf_spec(i) for i in pages]],
        out_specs=pl.BlockSpec((1, n_new, d), new_map),
    )
    return pl.pallas_call(
        functools.partial(_fox_decode_kernel, n_pages=n_pages),
        grid_spec=grid_spec,
        out_shape=jax.ShapeDtypeStruct((nb, n_new, d), BF16),
        compiler_params=_params("parallel"),
        name="fox_attn_sample",
    )(page_table.reshape(-1), q, k_new, v_new, lf_new,
      *([cache_kt] * n_pages), *([cache_vt] * n_pages), *([cache_lft] * n_pages))


def _ml_proj_kernel(x_ref, g_ref, wk_ref, wqvo_ref, wg_ref, bg_ref,
                    q_ref, k_ref, v_ref, o_ref, gt_ref, *, feature_major):
    d = x_ref.shape[-1]
    hq = ML_HEADS * ML_DQK
    h = _rms(x_ref[...], g_ref[...]).astype(BF16)
    k_ref[...] = (_dot(h, wk_ref[...]) * (ML_DQK ** -0.5)).astype(BF16)
    if feature_major:
        p = _dot_nt(wqvo_ref[...], h)
        q_ref[0] = p[:hq].astype(BF16)
        v_ref[0] = p[hq:hq + d].astype(BF16)
        o_ref[0] = jax.nn.sigmoid(p[hq + d:])
    else:
        p = _dot(h, wqvo_ref[...])
        q_ref[...] = p[:, :hq].astype(BF16)
        v_ref[...] = p[:, hq:hq + d].astype(BF16)
        o_ref[...] = jax.nn.sigmoid(p[:, hq + d:])
    gates = _dot(h, wg_ref[...]) + bg_ref[...]
    gates = GATE_CAP * jnp.tanh(gates / GATE_CAP)
    lane = lax.broadcasted_iota(jnp.int32, gates.shape, 1)
    gates = jnp.where(lane < ML_HEADS, gates, _log_sigmoid(gates))
    gt_ref[...] = gates[:, :2 * ML_HEADS]


def _ml_proj(x, g, wk, wqvo, wg, bg, seq_len, *, feature_major):
    m, d = x.shape
    hq = ML_HEADS * ML_DQK
    n_seq = m // seq_len
    nl = seq_len // ROW_TILE
    row = lambda b, l: (b * nl + l, 0)
    const = lambda b, l: (0, 0)

    def out(width, dtype):
        if feature_major:
            return (pl.BlockSpec((1, width, ROW_TILE), lambda b, l: (b, 0, l)),
                    jax.ShapeDtypeStruct((n_seq, width, seq_len), dtype))
        return pl.BlockSpec((ROW_TILE, width), row), jax.ShapeDtypeStruct((m, width), dtype)

    (q_spec, q_shape), (v_spec, v_shape), (o_spec, o_shape) = out(hq, BF16), out(d, BF16), out(d, F32)
    return pl.pallas_call(
        functools.partial(_ml_proj_kernel, feature_major=feature_major),
        grid=(n_seq, nl),
        in_specs=[pl.BlockSpec((ROW_TILE, d), row), pl.BlockSpec((1, d), const),
                  _resident(wk.shape), _resident(wqvo.shape),
                  pl.BlockSpec((d, LANES), const), pl.BlockSpec((1, LANES), const)],
        out_specs=(q_spec, pl.BlockSpec((ROW_TILE, hq), row), v_spec, o_spec,
                   pl.BlockSpec((ROW_TILE, 2 * ML_HEADS), row)),
        out_shape=(q_shape, jax.ShapeDtypeStruct((m, hq), BF16), v_shape, o_shape,
                   jax.ShapeDtypeStruct((m, 2 * ML_HEADS), F32)),
        compiler_params=_params("parallel", "parallel"),
        name="ml_proj_prompt" if feature_major else "ml_proj_sample",
    )(x, g, wk, wqvo, wg, bg)


def _ml_chunk_kernel(qt_ref, k_ref, vt_ref, ogt_ref, gt_ref, gtt_ref, gtn_ref, gttn_ref, gnb_ref,
                     a_ref, c_ref, n_ref, m_ref, ct_sc, n_sc, m_sc, left_sc, brows_sc):
    t = k_ref.shape[1]
    n_g = 2 * ML_HEADS
    ri = lax.broadcasted_iota(jnp.int32, (t, t), 0)
    ci = lax.broadcasted_iota(jnp.int32, (t, t), 1)
    causal_t = ri <= ci

    def gate_terms(gates, gates_t):
        b_cols = _dot_mask(ci <= ri, gates)
        b_rows = _mask_dot(gates_t, causal_t)
        w_cols = jnp.concatenate([gates[:, :ML_HEADS] - b_cols[:, ML_HEADS:], jnp.zeros((t, ML_HEADS), F32)],
                                 axis=1)
        left = jnp.concatenate([p.astype(F32) for p in _split3(w_cols)]
                               + [jnp.ones((t, n_g), F32), jnp.zeros((t, LANES - 4 * n_g), F32)],
                               axis=1).astype(BF16)
        return left, b_rows

    @pl.when(pl.program_id(1) == 0)
    def _():
        ct_sc[...] = jnp.zeros_like(ct_sc)
        n_sc[...] = jnp.zeros_like(n_sc)
        m_sc[...] = jnp.zeros_like(m_sc)
        left_sc[...], brows_sc[...] = gate_terms(gt_ref[0], gtt_ref[0])

    left = left_sc[...]
    b_rows = brows_sc[...]
    gates_t = gtt_ref[0]
    right_pad = jnp.zeros((LANES - 4 * n_g, t), F32)
    sub = lax.broadcasted_iota(jnp.int32, (n_g, t), 0)
    sub8 = lax.broadcasted_iota(jnp.int32, (SUBLANES, t), 0)
    lane = lax.broadcasted_iota(jnp.int32, (t, LANES), 1)
    half = (lane < ML_DQK, lane >= ML_DQK)
    row_h = lax.broadcasted_iota(jnp.int32, (LANES, t), 0)
    half_rows = (row_h < ML_DQK, row_h >= ML_DQK)
    lane1 = lax.broadcasted_iota(jnp.int32, (1, LANES), 1)
    head_lane = lax.broadcasted_iota(jnp.int32, (1, ML_HEADS), 1)
    m_all = m_sc[...]
    m_out = jnp.zeros((1, ML_HEADS), F32)

    heads = range(ML_HEADS)
    pairs = range(ML_HEADS // 2)
    pj = lambda h: slice((h // 2) * LANES, (h // 2 + 1) * LANES)
    dv = lambda h: slice(h * ML_DV, (h + 1) * ML_DV)
    i_r = [gates_t[h:h + 1, :] for h in heads]
    b_r = [b_rows[ML_HEADS + h:ML_HEADS + h + 1, :] for h in heads]
    m_prev = [m_all[:, h:h + 1] for h in heads]
    ct = [ct_sc[j] for j in pairs]
    n2 = [n_sc[j] for j in pairs]
    km = [jnp.where(half[h % 2], k_ref[0, :, pj(h)], jnp.zeros((t, LANES), BF16)) for h in heads]

    logw, kq, cq, qn = [], [], [], []
    for h in heads:
        qt2 = qt_ref[0, pj(h), :]
        qtm = jnp.where(half_rows[h % 2], qt2, jnp.zeros_like(qt2))
        u_hi, u_mid, u_lo = (p.astype(F32) for p in _split3(b_r[h]))
        tail = jnp.where(sub == 0, u_hi, jnp.where(sub == 1, u_mid, jnp.where(sub == 2, u_lo, 0.0)))
        pick = (sub == h).astype(F32)
        right = jnp.concatenate([pick, pick, pick, tail, right_pad], axis=0).astype(BF16)
        logw.append(_dot(left, right))
        kq.append(_dot(km[h], qt2))
        cq.append(_dot(ct[h // 2].astype(BF16), qtm))
        qn.append(_dot(n2[h // 2].astype(BF16), qtm)[0:1, :])

    left_next, brows_next = gate_terms(gtn_ref[0], gttn_ref[0])

    m_t, w_inter, a, a_sum, vtw, wk8, decay = [], [], [], [], [], [], []
    for h in heads:
        lw = jnp.where(causal_t, logw[h], NEG_INF)
        inter = b_r[h] + m_prev[h]
        m_t.append(jnp.maximum(inter, jnp.max(lw, axis=0, keepdims=True)))
        w_inter.append(jnp.exp(inter - m_t[h]))
        a_h = jnp.exp(lw - m_t[h]) * kq[h]
        a_sum.append(jnp.sum(a_h, axis=0, keepdims=True))
        a.append(a_h.astype(BF16))
        b_last = b_r[h][:, t - 1:t]
        m_new = m_t[h][:, t - 1:t]
        decay.append(jnp.exp(b_last + m_prev[h] - m_new))
        wk = jnp.exp(b_last - b_r[h] + i_r[h] - m_new)
        vtw.append((vt_ref[0, dv(h), :].astype(F32) * wk).astype(BF16))
        wk8.append(jnp.where(sub8 == 0, wk, 0.0).astype(BF16))
        m_out = jnp.where(head_lane == h, m_new, m_out)

    va = [_dot(vt_ref[0, dv(h), :], a[h]) for h in heads]
    c_add = [_dot(vtw[h], km[h]) for h in heads]
    n_add = [_dot(wk8[h], km[h]) for h in heads]

    for h in heads:
        hu = w_inter[h] * cq[h] + va[h]
        den = w_inter[h] * qn[h] + a_sum[h]
        r1 = 1.0 / jnp.maximum(jnp.abs(den), jnp.exp(-m_t[h]))
        ms = jnp.mean(hu * hu, axis=0, keepdims=True)
        hn = hu * (r1 * lax.rsqrt(r1 * r1 * ms + EPS)) * _lane_tile(gnb_ref[dv(h), :], t)
        a_ref[0, :, dv(h)] = (ogt_ref[0, dv(h), :] * hn).T.astype(BF16)
    for j in pairs:
        d2 = jnp.where(lane1 < ML_DQK, decay[2 * j], decay[2 * j + 1])
        ct_sc[j] = d2 * ct[j] + c_add[2 * j] + c_add[2 * j + 1]
        n_sc[j] = d2 * n2[j] + n_add[2 * j] + n_add[2 * j + 1]
    m_sc[...] = m_out
    left_sc[...] = left_next
    brows_sc[...] = brows_next

    @pl.when(pl.program_id(1) == pl.num_programs(1) - 1)
    def _():
        for j in range(ML_HEADS // 2):
            c_ref[0, j] = ct_sc[j].T
            n_ref[0, j:j + 1, :] = n_sc[j][0:1, :]
        m_ref[0] = m_out


def _ml_step_kernel(q_ref, k_ref, v_ref, og_ref, gt_ref, gtt_ref, gn_ref, c0_ref, n0_ref, m0_ref,
                    a_ref, c_ref, n_ref, m_ref):
    g, t = q_ref.shape[0], q_ref.shape[1]
    gates = gt_ref[...]
    gates_t = gtt_ref[...]
    ri = lax.broadcasted_iota(jnp.int32, (g, t, t), 1)
    ci = lax.broadcasted_iota(jnp.int32, (g, t, t), 2)
    causal = ci <= ri
    eye = jnp.broadcast_to(
        (lax.broadcasted_iota(jnp.int32, (ML_DQK, ML_DQK), 0)
         == lax.broadcasted_iota(jnp.int32, (ML_DQK, ML_DQK), 1)).astype(BF16)[None], (g, ML_DQK, ML_DQK))
    head_lane = lax.broadcasted_iota(jnp.int32, (g, 1, ML_HEADS), 2)
    m_all = m0_ref[...]
    m_out = jnp.zeros((g, 1, ML_HEADS), F32)

    heads = range(ML_HEADS)
    dv = lambda h: slice(h * ML_DV, (h + 1) * ML_DV)
    bmm = functools.partial(jnp.einsum, preferred_element_type=F32)
    q = [q_ref[:, :, h * ML_DQK:(h + 1) * ML_DQK] for h in heads]
    k = [k_ref[:, :, h * ML_DQK:(h + 1) * ML_DQK] for h in heads]
    v = [v_ref[:, :, dv(h)] for h in heads]
    c0 = [c0_ref[:, h] for h in heads]
    n0 = [n0_ref[:, h:h + 1, :] for h in heads]
    m_prev = [m_all[:, :, h:h + 1] for h in heads]

    qk = [bmm("gtd,gsd->gts", q[h], k[h]) for h in heads]
    qc = [bmm("gtd,gde->gte", q[h], c0[h].astype(BF16)) for h in heads]
    kt = [bmm("gdk,gsk->gds", eye, k[h]) for h in heads]

    m_t, w_inter, a, a_sum, kwt, kw_sum, decay = [], [], [], [], [], [], []
    for h in heads:
        fh = ML_HEADS + h
        i_c, f_c = gates[:, :, h:h + 1], gates[:, :, fh:fh + 1]
        i_r, f_r = gates_t[:, h:h + 1, :], gates_t[:, fh:fh + 1, :]
        b_c = jnp.sum(jnp.where(causal, f_r, 0.0), axis=2, keepdims=True)
        b_r = jnp.sum(jnp.where(ri <= ci, f_c, 0.0), axis=1, keepdims=True)
        logw = jnp.where(causal, b_c - b_r + i_r, NEG_INF)
        inter = b_c + m_prev[h]
        m_t.append(jnp.maximum(inter, jnp.max(logw, axis=2, keepdims=True)))
        w_inter.append(jnp.exp(inter - m_t[h]))
        a_h = jnp.exp(logw - m_t[h]) * qk[h]
        a_sum.append(jnp.sum(a_h, axis=2, keepdims=True))
        a.append(a_h.astype(BF16))
        b_last = b_c[:, t - 1:t, :]
        m_new = m_t[h][:, t - 1:t, :]
        decay.append(jnp.exp(b_last + m_prev[h] - m_new))
        kw = jnp.exp(b_last - b_c + i_c - m_new) * k[h].astype(F32)
        kw_sum.append(jnp.sum(kw, axis=1, keepdims=True))
        kwt.append((kt[h] * jnp.exp(b_last - b_r + i_r - m_new)).astype(BF16))
        m_out = jnp.where(head_lane == h, m_new, m_out)

    av = [bmm("gts,gse->gte", a[h], v[h]) for h in heads]
    c_add = [bmm("gds,gse->gde", kwt[h], v[h]) for h in heads]

    for h in heads:
        num = w_inter[h] * qc[h] + av[h]
        qn = jnp.sum(q[h].astype(F32) * n0[h], axis=2, keepdims=True)
        den = w_inter[h] * qn + a_sum[h]
        hs = num / jnp.maximum(jnp.abs(den), jnp.exp(-m_t[h]))
        hn = hs * lax.rsqrt(jnp.mean(hs * hs, axis=2, keepdims=True) + EPS) * gn_ref[:, dv(h)]
        a_ref[:, :, dv(h)] = (og_ref[:, :, dv(h)] * hn).astype(BF16)
        c_ref[:, h] = decay[h] * c0[h] + c_add[h]
        n_ref[:, h:h + 1, :] = decay[h] * n0[h] + kw_sum[h]
    m_ref[...] = m_out


def _ml_chunk(qt, k, vt, ogt, gates, gates_t, gnb, chunk):
    nb, d, seq_len = vt.shape
    hq = k.shape[2]
    pairs = ML_HEADS // 2
    nc = seq_len // chunk
    ahead = lambda c: jnp.minimum(c + 1, nc - 1)
    tok = lambda width: pl.BlockSpec((1, chunk, width), lambda b, c: (b, c, 0))
    tok_t = lambda width: pl.BlockSpec((1, width, chunk), lambda b, c: (b, 0, c))
    return pl.pallas_call(
        _ml_chunk_kernel,
        grid=(nb, nc),
        in_specs=[tok_t(hq), tok(hq), tok_t(d), tok_t(d), tok(2 * ML_HEADS), tok_t(2 * ML_HEADS),
                  pl.BlockSpec((1, chunk, 2 * ML_HEADS), lambda b, c: (b, ahead(c), 0)),
                  pl.BlockSpec((1, 2 * ML_HEADS, chunk), lambda b, c: (b, 0, ahead(c))),
                  pl.BlockSpec((d, LANES), lambda b, c: (0, 0))],
        out_specs=(tok(d), pl.BlockSpec((1, pairs, LANES, LANES), lambda b, c: (b, 0, 0, 0)),
                   pl.BlockSpec((1, pairs, LANES), lambda b, c: (b, 0, 0)),
                   pl.BlockSpec((1, 1, ML_HEADS), lambda b, c: (b, 0, 0))),
        out_shape=(jax.ShapeDtypeStruct((nb, seq_len, d), BF16),
                   jax.ShapeDtypeStruct((nb, pairs, LANES, LANES), F32),
                   jax.ShapeDtypeStruct((nb, pairs, LANES), F32),
                   jax.ShapeDtypeStruct((nb, 1, ML_HEADS), F32)),
        scratch_shapes=[pltpu.VMEM((pairs, ML_DV, LANES), F32), pltpu.VMEM((pairs, SUBLANES, LANES), F32),
                        pltpu.VMEM((1, ML_HEADS), F32), pltpu.VMEM((chunk, LANES), BF16),
                        pltpu.VMEM((2 * ML_HEADS, chunk), F32)],
        compiler_params=_params("parallel", "arbitrary"),
        name="ml_chunk",
    )(qt, k, vt, ogt, gates, gates_t, gates, gates_t, gnb)


def _ml_step(q, k, v, og, gates, gates_t, gn, c0, n0, m0):
    nb, t, d = v.shape
    hq = q.shape[2]
    g = ML_STEP_SEQS
    lead = lambda *rest: pl.BlockSpec((g, *rest), lambda i: (i,) + (0,) * len(rest))
    state = (lead(ML_HEADS, ML_DQK, ML_DV), lead(ML_HEADS, ML_DQK), lead(1, ML_HEADS))
    return pl.pallas_call(
        _ml_step_kernel,
        grid=(nb // g,),
        in_specs=[lead(t, hq), lead(t, hq), lead(t, d), lead(t, d), lead(t, 2 * ML_HEADS),
                  lead(2 * ML_HEADS, t), pl.BlockSpec((1, d), lambda i: (0, 0)), *state],
        out_specs=(lead(t, d), *state),
        out_shape=(jax.ShapeDtypeStruct((nb, t, d), BF16), jax.ShapeDtypeStruct(c0.shape, F32),
                   jax.ShapeDtypeStruct(n0.shape, F32), jax.ShapeDtypeStruct(m0.shape, F32)),
        compiler_params=_params("parallel"),
        name="ml_step",
    )(q, k, v, og, gates, gates_t, gn, c0, n0, m0)


def _pad_cols(w, width):
    return jnp.pad(w, ((0, 0), (0, width - w.shape[1])))


def kernel(x_prompt, x_sample, state_conv, cache_k, cache_v, cache_logf, page_table, state_C, state_n, state_m, norm_mix, norm_ffn, norm_final, w_conv_in, w_conv, w_conv_out, w_fox_in, b_fox_f, w_fox_out, w_ml_in, b_ml_gates, g_ml_norm, w_ml_out, w_ffn_gu, w_ffn_down):
    bp, lp, d = x_prompt.shape
    bs, ls, _ = x_sample.shape
    depth = norm_mix.shape[0]
    dff = w_ffn_down.shape[1]
    assert d == D_MODEL and ls == SUBLANES
    xp = x_prompt.reshape(bp * lp, d)
    xs = x_sample.reshape(bs * ls, d)
    gf = norm_final.reshape(1, d)
    hq = ML_HEADS * ML_DQK
    pairs = ML_HEADS // 2
    out = {name: [] for name in ("conv_p", "conv_s", "kp", "vp", "lfp", "ks", "vs", "lfs",
                                 "cp", "np", "mp", "cs", "ns", "ms")}

    for i in range(depth):
        kind, j = i % 3, i // 3
        gm = norm_mix[i].reshape(1, d)
        if kind == 0:
            win = w_conv_in[j].astype(BF16)
            wo = w_conv_out[j].astype(BF16)
            ap, sp = _conv_mixer(xp, gm, win, w_conv[j], jnp.zeros((bp, CONV_W - 1, d), F32), lp)
            as_, ss = _conv_mixer(xs, gm, win, w_conv[j], state_conv[j], ls)
            out["conv_p"].append(sp)
            out["conv_s"].append(ss)
        elif kind == 1:
            wqkv = w_fox_in[j][:, :3 * d].astype(BF16)
            wf = _pad_cols(w_fox_in[j][:, 3 * d:], LANES).astype(BF16)
            bf = _pad_cols(b_fox_f[j].reshape(1, FOX_HEADS), LANES)
            wo = w_fox_out[j].astype(BF16)
            qtb, kt, vt, kb, vtb, lft, c, ct = _fox_proj_prompt(
                xp, gm, wqkv.T, wf, wf[:, :FOX_HEADS].T, bf, b_fox_f[j].reshape(FOX_HEADS, 1), lp)
            ap = _fox_attn_prompt(qtb, kb, vtb, c, ct, lp)
            heads_last = lambda t: jnp.transpose(t.reshape(bp, FOX_HEADS, FOX_HEAD_DIM, lp), (0, 3, 1, 2))
            out["kp"].append(heads_last(kt))
            out["vp"].append(heads_last(vt))
            out["lfp"].append(jnp.swapaxes(lft, 1, 2))
            q, k, v, lf = _fox_proj_sample(xs, gm, wqkv, wf, bf)
            as_ = _fox_attn_sample(
                q.reshape(bs, ls, d), k.reshape(bs, ls, d), v.reshape(bs, ls, d),
                lf.reshape(bs, ls, FOX_HEADS), jnp.transpose(cache_k[j], (0, 2, 3, 1)),
                jnp.transpose(cache_v[j], (0, 2, 3, 1)), jnp.transpose(cache_logf[j], (0, 2, 1)),
                page_table).reshape(bs * ls, d)
            out["ks"].append(k.reshape(bs, ls, FOX_HEADS, FOX_HEAD_DIM))
            out["vs"].append(v.reshape(bs, ls, FOX_HEADS, FOX_HEAD_DIM))
            out["lfs"].append(lf.reshape(bs, ls, FOX_HEADS))
        else:
            w = w_ml_in[j]
            wk = w[:, hq:2 * hq].astype(BF16)
            wqvo = jnp.concatenate([w[:, :hq], w[:, 2 * hq:2 * hq + 2 * d]], axis=1).astype(BF16)
            wg = _pad_cols(w[:, 2 * hq + 2 * d:], LANES).astype(BF16)
            bg = _pad_cols(b_ml_gates[j].reshape(1, 2 * ML_HEADS), LANES)
            gn = g_ml_norm[j].reshape(1, d)
            wo = w_ml_out[j].astype(BF16)

            def tokens(nb, seq_len, *arrays):
                return [t.reshape(nb, seq_len, t.shape[-1]) for t in arrays]

            qt, k, vt, ogt, gates = _ml_proj(xp, gm, wk, wqvo.T, wg, bg, lp, feature_major=True)
            k, gates = tokens(bp, lp, k, gates)
            a, c1, n1, m1 = _ml_chunk(qt, k, vt, ogt, gates, jnp.swapaxes(gates, 1, 2),
                                      jnp.broadcast_to(gn.reshape(d, 1), (d, LANES)), ML_CHUNK_PROMPT)
            ap = a.reshape(bp * lp, d)
            out["cp"].append(c1.reshape(bp, ML_HEADS, ML_DQK, ML_DV))
            out["np"].append(n1.reshape(bp, ML_HEADS, ML_DQK))
            out["mp"].append(m1.reshape(bp, ML_HEADS))

            q, k, v, og, gates = _ml_proj(xs, gm, wk, wqvo, wg, bg, ROW_TILE, feature_major=False)
            q, k, v, og, gates = tokens(bs, ls, q, k, v, og, gates)
            a, c1, n1, m1 = _ml_step(q, k, v, og, gates, jnp.swapaxes(gates, 1, 2), gn,
                                     state_C[j], state_n[j], state_m[j].reshape(bs, 1, ML_HEADS))
            as_ = a.reshape(bs * ls, d)
            out["cs"].append(c1); out["ns"].append(n1); out["ms"].append(m1.reshape(bs, ML_HEADS))

        gffn = norm_ffn[i].reshape(1, d)
        wg_ffn = w_ffn_gu[i][:, :dff].astype(BF16)
        wu_ffn = w_ffn_gu[i][:, dff:].astype(BF16)
        wd_ffn = w_ffn_down[i].astype(BF16)
        last = i == depth - 1
        xp = _post(xp, ap, wo, gffn, wg_ffn, wu_ffn, wd_ffn, gf, final_norm=last)
        xs = _post(xs, as_, wo, gffn, wg_ffn, wu_ffn, wd_ffn, gf, final_norm=last)

    st = lambda name: jnp.stack(out[name])
    return (xp.reshape(bp, lp, d), xs.reshape(bs, ls, d), st("conv_p"), st("conv_s"),
            st("kp"), st("vp"), st("lfp"), st("ks"), st("vs"), st("lfs"),
            st("cp"), st("np"), st("mp"), st("cs"), st("ns"), st("ms"))
```

```python
import functools

import jax
import jax.numpy as jnp
from jax import lax
from jax.experimental import pallas as pl
from jax.experimental.pallas import tpu as pltpu

F32 = jnp.float32
BF16 = jnp.bfloat16

D_MODEL = 1024
CONV_W = 3
FOX_HEADS = 16
FOX_HEAD_DIM = D_MODEL // FOX_HEADS
ML_HEADS = 8
ML_DV = D_MODEL // ML_HEADS
ML_DQK = ML_DV // 2
GATE_CAP = 15.0
EPS = 1e-6
NEG_INF = -1e30
LOG2E = 1.4426950408889634

LANES = 128
SUBLANES = 8
ROW_TILE = 1024
FOX_PROJ_TILE = 512
POST_ROW_TILE = 1024
FFN_CHUNK = 256
CONV_CHUNK = 256
ATTN_TQ = 512
ML_CHUNK_PROMPT = 128
ML_STEP_CHUNKS = 4
ML_STEP_SEQS = 16
VMEM_LIMIT = 56 * 1024 * 1024


def _params(*sem):
    return pltpu.CompilerParams(dimension_semantics=sem, vmem_limit_bytes=VMEM_LIMIT)


def _resident(shape):
    return pl.BlockSpec(shape, lambda *_: (0, 0), pipeline_mode=pl.Buffered(1))


def _rms(x, g):
    return x * lax.rsqrt(jnp.mean(x * x, axis=-1, keepdims=True) + EPS) * g


def _log_sigmoid(x):
    return jnp.minimum(x, 0.0) - jnp.log1p(jnp.exp(-jnp.abs(x)))


def _dot(a, b):
    return jnp.dot(a, b, preferred_element_type=F32)


def _dot_nt(a, b, precision=None):
    return lax.dot_general(a, b, (((1,), (1,)), ((), ())), precision=precision,
                           preferred_element_type=F32)


def _lane_tile(x, width):
    return jnp.tile(x, (1, width // LANES))


def _post_kernel(x_ref, a_ref, wo_ref, g_ref, wg_ref, wu_ref, wd_ref, gf_ref, o_ref, *, final_norm):
    x1 = x_ref[...] + _dot(a_ref[...], wo_ref[...])
    h = _rms(x1, g_ref[...]).astype(BF16)
    y = x1
    for c in range(0, wg_ref.shape[1], FFN_CHUNK):
        g = _dot(h, wg_ref[:, c:c + FFN_CHUNK])
        u = _dot(h, wu_ref[:, c:c + FFN_CHUNK])
        y = y + _dot(((g * jax.nn.sigmoid(g)) * u).astype(BF16), wd_ref[c:c + FFN_CHUNK, :])
    o_ref[...] = _rms(y, gf_ref[...]) if final_norm else y


def _post(x, a, wo, g, wg, wu, wd, gf, *, final_norm):
    m, d = x.shape
    dff = wg.shape[1]
    tm = min(POST_ROW_TILE, m)
    assert dff % FFN_CHUNK == 0 and m % tm == 0
    row = pl.BlockSpec((tm, d), lambda i: (i, 0))
    return pl.pallas_call(
        functools.partial(_post_kernel, final_norm=final_norm),
        grid=(m // tm,),
        in_specs=[row, row, _resident((d, d)), _resident((1, d)), _resident((d, dff)), _resident((d, dff)),
                  _resident((dff, d)), _resident((1, d))],
        out_specs=row,
        out_shape=jax.ShapeDtypeStruct((m, d), F32),
        compiler_params=_params("parallel"),
        name="post_ffn",
    )(x, a, wo, g, wg, wu, wd, gf)


def _conv_columns(x_ref, g_ref, win_ref, wc_ref, z_ref, rig, prev_rows, keep_tail):
    d = x_ref.shape[-1]
    h = _rms(x_ref[...], g_ref[...]).astype(BF16)
    for c in range(0, d, CONV_CHUNK):
        cols = slice(c, c + CONV_CHUNK)
        bg = _dot(h, win_ref[:, c:c + CONV_CHUNK])
        u = (_dot(h, win_ref[:, d + c:d + c + CONV_CHUNK])
             * _dot(h, win_ref[:, 2 * d + c:2 * d + c + CONV_CHUNK]))
        p0, p1 = prev_rows(cols)
        s1 = jnp.where(rig == 0, p1, pltpu.roll(u, 1, axis=0))
        s2 = jnp.where(rig == 0, p0, jnp.where(rig == 1, p1, pltpu.roll(u, 2, axis=0)))
        conv = wc_ref[0:1, cols] * s2 + wc_ref[1:2, cols] * s1 + wc_ref[2:3, cols] * u
        z_ref[:, cols] = (bg * conv).astype(BF16)
        keep_tail(cols, u)


def _conv_prompt_kernel(x_ref, g_ref, win_ref, wc_ref, prev_ref, z_ref, st_ref, carry_sc):
    @pl.when(pl.program_id(1) == 0)
    def _():
        carry_sc[...] = prev_ref[0]

    r = x_ref.shape[0]
    rig = lax.broadcasted_iota(jnp.int32, (r, CONV_CHUNK), 0)

    def keep_tail(cols, u):
        tail = u[r - (CONV_W - 1):, :]
        carry_sc[:, cols] = tail
        st_ref[0, :, cols] = tail

    _conv_columns(x_ref, g_ref, win_ref, wc_ref, z_ref, rig,
                  lambda cols: (carry_sc[0:1, cols], carry_sc[1:2, cols]), keep_tail)


def _conv_sample_kernel(x_ref, g_ref, win_ref, wc_ref, prev_ref, z_ref, st_ref):
    r = x_ref.shape[0]
    n_seq = r // SUBLANES
    rig = lax.broadcasted_iota(jnp.int32, (r, CONV_CHUNK), 0) & (SUBLANES - 1)

    def prev_rows(cols):
        prev = prev_ref[:, :, cols]
        rows = lambda i: jnp.broadcast_to(prev[:, i:i + 1, :], (n_seq, SUBLANES, CONV_CHUNK)).reshape(r, CONV_CHUNK)
        return rows(0), rows(1)

    def keep_tail(cols, u):
        st_ref[:, :, cols] = u.reshape(n_seq, SUBLANES, CONV_CHUNK)[:, SUBLANES - (CONV_W - 1):, :]

    _conv_columns(x_ref, g_ref, win_ref, wc_ref, z_ref, rig, prev_rows, keep_tail)


def _conv_mixer(x, g, win, wc, prev, seq_len):
    m, d = x.shape
    n_seq = m // seq_len
    common_in = [_resident((d, 3 * d)), _resident((CONV_W, d))]
    out_shape = (jax.ShapeDtypeStruct((m, d), BF16), jax.ShapeDtypeStruct((n_seq, CONV_W - 1, d), F32))
    if seq_len == SUBLANES:
        per = ROW_TILE // SUBLANES
        return pl.pallas_call(
            _conv_sample_kernel,
            grid=(m // ROW_TILE,),
            in_specs=[pl.BlockSpec((ROW_TILE, d), lambda i: (i, 0)), pl.BlockSpec((1, d), lambda i: (0, 0)),
                      *common_in, pl.BlockSpec((per, CONV_W - 1, d), lambda i: (i, 0, 0))],
            out_specs=(pl.BlockSpec((ROW_TILE, d), lambda i: (i, 0)),
                       pl.BlockSpec((per, CONV_W - 1, d), lambda i: (i, 0, 0))),
            out_shape=out_shape,
            compiler_params=_params("parallel"),
            name="conv_sample",
        )(x, g, win, wc, prev)
    assert seq_len % ROW_TILE == 0
    nl = seq_len // ROW_TILE
    return pl.pallas_call(
        _conv_prompt_kernel,
        grid=(n_seq, nl),
        in_specs=[pl.BlockSpec((ROW_TILE, d), lambda b, l: (b * nl + l, 0)),
                  pl.BlockSpec((1, d), lambda b, l: (0, 0)),
                  *common_in, pl.BlockSpec((1, CONV_W - 1, d), lambda b, l: (b, 0, 0))],
        out_specs=(pl.BlockSpec((ROW_TILE, d), lambda b, l: (b * nl + l, 0)),
                   pl.BlockSpec((1, CONV_W - 1, d), lambda b, l: (b, 0, 0))),
        out_shape=out_shape,
        scratch_shapes=[pltpu.VMEM((CONV_W - 1, d), F32)],
        compiler_params=_params("parallel", "arbitrary"),
        name="conv_prompt",
    )(x, g, win, wc, prev)


def _fox_proj_kernel(x_ref, g_ref, wqkv_ref, wf_ref, bf_ref, q_ref, k_ref, v_ref, lf_ref):
    d = x_ref.shape[-1]
    h = _rms(x_ref[...], g_ref[...]).astype(BF16)
    p = _dot(h, wqkv_ref[...])
    logf = _log_sigmoid(_dot(h, wf_ref[...]) + bf_ref[...])
    q_ref[...] = p[:, :d] * (FOX_HEAD_DIM ** -0.5)
    k_ref[...] = p[:, d:2 * d]
    v_ref[...] = p[:, 2 * d:]
    lf_ref[...] = logf[:, :FOX_HEADS]


def _fox_proj_sample(x, g, wqkv, wf, bf):
    m, d = x.shape
    row = lambda i: (i, 0)
    const = lambda i: (0, 0)
    big = pl.BlockSpec((ROW_TILE, d), row)
    f32_out = jax.ShapeDtypeStruct((m, d), F32)
    return pl.pallas_call(
        _fox_proj_kernel,
        grid=(m // ROW_TILE,),
        in_specs=[big, pl.BlockSpec((1, d), const), _resident((d, 3 * d)),
                  pl.BlockSpec((d, LANES), const), pl.BlockSpec((1, LANES), const)],
        out_specs=(big, big, big, pl.BlockSpec((ROW_TILE, FOX_HEADS), row)),
        out_shape=(f32_out, f32_out, f32_out, jax.ShapeDtypeStruct((m, FOX_HEADS), F32)),
        compiler_params=_params("parallel"),
        name="fox_proj_sample",
    )(x, g, wqkv, wf, bf)


def _fox_proj_prompt_kernel(x_ref, g_ref, wqkvt_ref, wf_ref, wft_ref, bf_ref, bft_ref,
                            qtb_ref, kt_ref, vt_ref, kb_ref, vtb_ref, lft_ref, c_ref, ct_ref,
                            carry_sc, carryt_sc):
    d = x_ref.shape[-1]
    r = x_ref.shape[0]
    h = _rms(x_ref[...], g_ref[...]).astype(BF16)
    qkv = _dot_nt(wqkvt_ref[...], h)
    qtb_ref[0] = (qkv[:d] * (FOX_HEAD_DIM ** -0.5 * LOG2E)).astype(BF16)
    kt_ref[0] = qkv[d:2 * d]
    vt_ref[0] = qkv[2 * d:]
    kb_ref[...] = qkv[d:2 * d].T.astype(BF16)
    vtb_ref[0] = qkv[2 * d:].astype(BF16)
    logf = _log_sigmoid(_dot(h, wf_ref[...]) + bf_ref[...])
    logf_t = _log_sigmoid(_dot_nt(wft_ref[...], h) + bft_ref[...])
    lft_ref[0] = logf_t

    @pl.when(pl.program_id(1) == 0)
    def _():
        carry_sc[...] = jnp.zeros_like(carry_sc)
        carryt_sc[...] = jnp.zeros_like(carryt_sc)

    ri = lax.broadcasted_iota(jnp.int32, (r, r), 0)
    ci = lax.broadcasted_iota(jnp.int32, (r, r), 1)
    c = carry_sc[...] + _dot_mask(ci <= ri, logf)
    carry_sc[...] = c[r - 1:r, :]
    c_ref[...] = c[:, :FOX_HEADS]
    ct = _lane_tile(carryt_sc[...], r) + _mask_dot(logf_t, ri <= ci)
    carryt_sc[...] = jnp.broadcast_to(ct[:, r - 1:r], carryt_sc.shape)
    ct_ref[0] = ct


def _fox_proj_prompt(x, g, wqkvt, wf, wft, bf, bft, seq_len):
    m, d = x.shape
    n_seq = m // seq_len
    tile = FOX_PROJ_TILE
    nl = seq_len // tile
    row = lambda b, l: (b * nl + l, 0)
    const = lambda b, l: (0, 0)
    feat = lambda width: pl.BlockSpec((1, width, tile), lambda b, l: (b, 0, l))
    feat_shape = lambda width, dtype: jax.ShapeDtypeStruct((n_seq, width, seq_len), dtype)
    return pl.pallas_call(
        _fox_proj_prompt_kernel,
        grid=(n_seq, nl),
        in_specs=[pl.BlockSpec((tile, d), row), pl.BlockSpec((1, d), const), _resident((3 * d, d)),
                  pl.BlockSpec((d, LANES), const), pl.BlockSpec((FOX_HEADS, d), const),
                  pl.BlockSpec((1, LANES), const), pl.BlockSpec((FOX_HEADS, 1), const)],
        out_specs=(feat(d), feat(d), feat(d), pl.BlockSpec((tile, d), row), feat(d),
                   feat(FOX_HEADS), pl.BlockSpec((tile, FOX_HEADS), row), feat(FOX_HEADS)),
        out_shape=(feat_shape(d, BF16), feat_shape(d, F32), feat_shape(d, F32),
                   jax.ShapeDtypeStruct((m, d), BF16), feat_shape(d, BF16),
                   feat_shape(FOX_HEADS, F32), jax.ShapeDtypeStruct((m, FOX_HEADS), F32),
                   feat_shape(FOX_HEADS, F32)),
        scratch_shapes=[pltpu.VMEM((1, LANES), F32), pltpu.VMEM((FOX_HEADS, LANES), F32)],
        compiler_params=_params("parallel", "arbitrary"),
        name="fox_proj_prompt",
    )(x, g, wqkvt, wf, wft, bf, bft)


def _fox_attn_kernel(qt_ref, k_ref, vt_ref, c_ref, ct_ref, o_ref, kaug_sc, vaug_sc, m_sc, acc_sc, *, tq):
    hp = pl.program_id(1)
    seq = k_ref.shape[0]
    hd = FOX_HEAD_DIM
    n_aug = 2 * SUBLANES
    n_bias = 3
    one_row = jnp.where(lax.broadcasted_iota(jnp.int32, (n_aug, seq), 0) == 0, 1.0, 0.0).astype(BF16)

    for hh in range(2):
        vaug_sc[hh] = jnp.concatenate([vt_ref[0, hh * hd:(hh + 1) * hd, :], one_row], axis=0)
    c3 = jnp.concatenate([p.astype(F32) for p in _split3(c_ref[...] * LOG2E)]
                         + [jnp.zeros((seq, LANES - n_bias * FOX_HEADS), F32)], axis=1).astype(BF16)
    src = lax.broadcasted_iota(jnp.int32, (LANES, LANES), 0)
    dst = lax.broadcasted_iota(jnp.int32, (LANES, LANES), 1)
    term = src >> (FOX_HEADS.bit_length() - 1)
    lane = lax.broadcasted_iota(jnp.int32, (seq, LANES), 1)
    k2 = k_ref[...]
    for hh in range(2):
        off = hd * (1 - hh)
        place = jnp.where((src & (FOX_HEADS - 1)) == 2 * hp + hh,
                          jnp.where(dst == off + term, -1.0, 0.0), 0.0).astype(BF16)
        ones = jnp.where(lane >= off + n_bias, jnp.where(lane < off + 2 * n_bias, 1.0, 0.0), 0.0)
        aug = (_dot(c3, place) + ones).astype(BF16)
        own = (lane >= hd) if hh else (lane < hd)
        kaug_sc[hh] = jnp.where(own, k2, aug)

    sub = lax.broadcasted_iota(jnp.int32, (n_aug, tq), 0)
    visible = (lax.broadcasted_iota(jnp.int32, (tq, tq), 0)
               <= lax.broadcasted_iota(jnp.int32, (tq, tq), 1))
    rest = jnp.zeros((hd - n_aug, tq), BF16)

    for n in range(seq // tq):
        cols = slice(n * tq, (n + 1) * tq)
        qt2 = qt_ref[0, :, cols]
        qa = []
        for hh in range(2):
            c_hi, c_mid, c_lo = (p.astype(F32)
                                 for p in _split3(ct_ref[0, pl.ds(2 * hp + hh, 1), cols] * LOG2E))
            piece = jnp.where(sub < n_bias, 1.0, jnp.where(sub == n_bias, c_hi, jnp.where(
                sub == n_bias + 1, c_mid, jnp.where(sub == n_bias + 2, c_lo, 0.0)))).astype(BF16)
            qa.append(jnp.concatenate([qt2[:hd], piece, rest] if hh == 0 else [piece, rest, qt2[hd:]], axis=0))
            m_sc[hh] = jnp.full((1, tq), NEG_INF, F32)
        acc_sc[...] = jnp.zeros_like(acc_sc)

        def scores(ki):
            return [_dot(kaug_sc[hh, ki * tq:(ki + 1) * tq, :], qa[hh]) for hh in range(2)]

        def absorb(ki, s2, diagonal):
            p2, alpha2 = [], []
            for hh in range(2):
                s = jnp.where(visible, s2[hh], NEG_INF) if diagonal else s2[hh]
                m_prev = m_sc[hh]
                m_new = jnp.maximum(m_prev, jnp.max(s, axis=0, keepdims=True))
                alpha2.append(jnp.exp2(m_prev - m_new))
                p2.append(jnp.exp2(s - m_new).astype(BF16))
                m_sc[hh] = m_new
            pv2 = [_dot(vaug_sc[hh, :, ki * tq:(ki + 1) * tq], p2[hh]) for hh in range(2)]
            for hh in range(2):
                acc_sc[hh] = alpha2[hh] * acc_sc[hh] + pv2[hh]

        s_next = scores(0)
        for ki in range(n + 1):
            s_cur = s_next
            if ki < n:
                s_next = scores(ki + 1)
            absorb(ki, s_cur, ki == n)

        o_t = jnp.concatenate([acc_sc[hh, :hd, :] / acc_sc[hh, hd:hd + 1, :] for hh in range(2)], axis=0)
        o_ref[cols, :] = o_t.T.astype(BF16)


def _fox_attn_prompt(qtb, kb, vtb, c, ct, seq_len):
    n_seq, d, _ = qtb.shape
    tq = ATTN_TQ
    pair = pl.BlockSpec((1, LANES, seq_len), lambda b, hp: (b, hp, 0))
    return pl.pallas_call(
        functools.partial(_fox_attn_kernel, tq=tq),
        grid=(n_seq, d // LANES),
        in_specs=[pair, pl.BlockSpec((seq_len, LANES), lambda b, hp: (b, hp)), pair,
                  pl.BlockSpec((seq_len, FOX_HEADS), lambda b, hp: (b, 0)),
                  pl.BlockSpec((1, FOX_HEADS, seq_len), lambda b, hp: (b, 0, 0))],
        out_specs=pl.BlockSpec((seq_len, LANES), lambda b, hp: (b, hp)),
        out_shape=jax.ShapeDtypeStruct((n_seq * seq_len, d), BF16),
        scratch_shapes=[pltpu.VMEM((2, seq_len, LANES), BF16),
                        pltpu.VMEM((2, FOX_HEAD_DIM + 2 * SUBLANES, seq_len), BF16),
                        pltpu.VMEM((2, 1, tq), F32),
                        pltpu.VMEM((2, FOX_HEAD_DIM + 2 * SUBLANES, tq), F32)],
        compiler_params=_params("parallel", "parallel"),
        name="fox_attn_prompt",
    )(qtb, kb, vtb, c, ct)


def _split3(x):
    hi = x.astype(BF16)
    r1 = x - hi.astype(F32)
    mid = r1.astype(BF16)
    return hi, mid, (r1 - mid.astype(F32)).astype(BF16)


def _mask_dot(x, mask):
    mask = mask.astype(BF16)
    hi, mid, lo = _split3(x)
    return _dot(hi, mask) + _dot(mid, mask) + _dot(lo, mask)


def _dot_mask(mask, x, nt=False):
    mask = mask.astype(BF16)
    dot = _dot_nt if nt else _dot
    hi, mid, lo = _split3(x)
    return dot(mask, hi) + dot(mask, mid) + dot(mask, lo)


def _fox_decode_kernel(pt_ref, q_ref, kn_ref, vn_ref, lfn_ref, *refs, n_pages):
    del pt_ref
    kt_refs, vt_refs, lft_refs = refs[:n_pages], refs[n_pages:2 * n_pages], refs[2 * n_pages:3 * n_pages]
    o_ref = refs[3 * n_pages]
    page = kt_refs[0].shape[-1]
    n_new, d = q_ref.shape[1], q_ref.shape[2]
    rows = FOX_HEADS * n_new
    assert rows == page == LANES and n_new == SUBLANES
    ri = lax.broadcasted_iota(jnp.int32, (rows, page), 0)
    ki = lax.broadcasted_iota(jnp.int32, (rows, page), 1)
    tok_bits = n_new.bit_length() - 1
    dim_bits = FOX_HEAD_DIM.bit_length() - 1
    head_of_row = ((lax.broadcasted_iota(jnp.int32, (rows, FOX_HEADS), 0) >> tok_bits)
                   == lax.broadcasted_iota(jnp.int32, (rows, FOX_HEADS), 1))
    own_cols = ((lax.broadcasted_iota(jnp.int32, (rows, d), 0) >> tok_bits)
                == (lax.broadcasted_iota(jnp.int32, (rows, d), 1) >> dim_bits))
    qbd = jnp.where(own_cols, jnp.tile(q_ref[0], (FOX_HEADS, 1)), 0.0).astype(BF16)

    pad = page - n_new
    kn = jnp.concatenate([kn_ref[0], jnp.zeros((pad, d), F32)], axis=0).astype(BF16)
    vn = jnp.concatenate([vn_ref[0], jnp.zeros((pad, d), F32)], axis=0).astype(BF16)
    lfn = jnp.concatenate([lfn_ref[0], jnp.zeros((pad, FOX_HEADS), F32)], axis=0)
    lf_rows = _dot_mask(head_of_row, lfn, nt=True)
    cum = _mask_dot(lf_rows, ri <= ki)
    t_of_row = ri & (n_new - 1)
    a = jnp.sum(jnp.where(ki == t_of_row, cum, 0.0), axis=1, keepdims=True)
    s_new = jnp.where(ki <= t_of_row, _dot_nt(qbd, kn) + a - cum, NEG_INF)

    lft = jnp.concatenate([r[...] for r in lft_refs], axis=0)
    sfx_in = _mask_dot(lft, ri >= ki)
    sfx_ex = sfx_in - lft
    later = jnp.zeros((FOX_HEADS, 1), F32)
    bias = [None] * n_pages
    for i in reversed(range(n_pages)):
        bias[i] = sfx_ex[i * FOX_HEADS:(i + 1) * FOX_HEADS, :] + later
        later = later + sfx_in[i * FOX_HEADS:(i + 1) * FOX_HEADS, 0:1]
    bias = _dot_mask(head_of_row, jnp.concatenate(bias, axis=1))
    s_old = jnp.concatenate(
        [_dot(qbd, r[...].reshape(d, page).astype(BF16)) for r in kt_refs], axis=1) + bias + a

    m = jnp.maximum(jnp.max(s_new, axis=1, keepdims=True), jnp.max(s_old, axis=1, keepdims=True))
    p_new = jnp.exp(s_new - m)
    p_old = jnp.exp(s_old - m)
    denom = jnp.sum(p_new, axis=1, keepdims=True) + jnp.sum(p_old, axis=1, keepdims=True)
    p_old = p_old.astype(BF16)
    acc = _dot(p_new.astype(BF16), vn)
    for i, r in enumerate(vt_refs):
        acc = acc + _dot_nt(p_old[:, i * page:(i + 1) * page], r[...].reshape(d, page).astype(BF16))
    o = jnp.where(own_cols, acc / denom, 0.0).reshape(FOX_HEADS, n_new, d)
    o_ref[0] = jnp.sum(o, axis=0).astype(BF16)


def _fox_attn_sample(q, k_new, v_new, lf_new, cache_kt, cache_vt, cache_lft, page_table):
    nb, n_new, d = q.shape
    n_pages = page_table.shape[1]
    page = cache_kt.shape[-1]

    def new_map(b, pt):
        return (b, 0, 0)

    def kv_spec(i):
        return pl.BlockSpec((None, FOX_HEADS, FOX_HEAD_DIM, page), lambda b, pt: (pt[b * n_pages + i], 0, 0, 0))

    def lf_spec(i):
        return pl.BlockSpec((None, FOX_HEADS, page), lambda b, pt: (pt[b * n_pages + i], 0, 0))

    pages = range(n_pages)
    grid_spec = pltpu.PrefetchScalarGridSpec(
        num_scalar_prefetch=1,
        grid=(nb,),
        in_specs=[pl.BlockSpec((1, n_new, d), new_map), pl.BlockSpec((1, n_new, d), new_map),
                  pl.BlockSpec((1, n_new, d), new_map), pl.BlockSpec((1, n_new, FOX_HEADS), new_map),
                  *[kv_spec(i) for i in pages], *[kv_spec(i) for i in pages], *[lf_spec(i) for i in pages]],
        out_specs=pl.BlockSpec((1, n_new, d), new_map),
    )
    return pl.pallas_call(
        functools.partial(_fox_decode_kernel, n_pages=n_pages),
        grid_spec=grid_spec,
        out_shape=jax.ShapeDtypeStruct((nb, n_new, d), BF16),
        compiler_params=_params("parallel"),
        name="fox_attn_sample",
    )(page_table.reshape(-1), q, k_new, v_new, lf_new,
      *([cache_kt] * n_pages), *([cache_vt] * n_pages), *([cache_lft] * n_pages))


def _ml_proj_kernel(x_ref, g_ref, wk_ref, wqvo_ref, wg_ref, bg_ref,
                    q_ref, k_ref, v_ref, o_ref, gt_ref, *, feature_major):
    d = x_ref.shape[-1]
    hq = ML_HEADS * ML_DQK
    h = _rms(x_ref[...], g_ref[...]).astype(BF16)
    k_ref[...] = (_dot(h, wk_ref[...]) * (ML_DQK ** -0.5)).astype(BF16)
    if feature_major:
        p = _dot_nt(wqvo_ref[...], h)
        q_ref[0] = p[:hq].astype(BF16)
        v_ref[0] = p[hq:hq + d].astype(BF16)
        o_ref[0] = jax.nn.sigmoid(p[hq + d:])
    else:
        p = _dot(h, wqvo_ref[...])
        q_ref[...] = p[:, :hq].astype(BF16)
        v_ref[...] = p[:, hq:hq + d].astype(BF16)
        o_ref[...] = jax.nn.sigmoid(p[:, hq + d:])
    gates = _dot(h, wg_ref[...]) + bg_ref[...]
    gates = GATE_CAP * jnp.tanh(gates / GATE_CAP)
    lane = lax.broadcasted_iota(jnp.int32, gates.shape, 1)
    gates = jnp.where(lane < ML_HEADS, gates, _log_sigmoid(gates))
    gt_ref[...] = gates[:, :2 * ML_HEADS]


def _ml_proj(x, g, wk, wqvo, wg, bg, seq_len, *, feature_major):
    m, d = x.shape
    hq = ML_HEADS * ML_DQK
    n_seq = m // seq_len
    nl = seq_len // ROW_TILE
    row = lambda b, l: (b * nl + l, 0)
    const = lambda b, l: (0, 0)

    def out(width, dtype):
        if feature_major:
            return (pl.BlockSpec((1, width, ROW_TILE), lambda b, l: (b, 0, l)),
                    jax.ShapeDtypeStruct((n_seq, width, seq_len), dtype))
        return pl.BlockSpec((ROW_TILE, width), row), jax.ShapeDtypeStruct((m, width), dtype)

    (q_spec, q_shape), (v_spec, v_shape), (o_spec, o_shape) = out(hq, BF16), out(d, BF16), out(d, F32)
    return pl.pallas_call(
        functools.partial(_ml_proj_kernel, feature_major=feature_major),
        grid=(n_seq, nl),
        in_specs=[pl.BlockSpec((ROW_TILE, d), row), pl.BlockSpec((1, d), const),
                  _resident(wk.shape), _resident(wqvo.shape),
                  pl.BlockSpec((d, LANES), const), pl.BlockSpec((1, LANES), const)],
        out_specs=(q_spec, pl.BlockSpec((ROW_TILE, hq), row), v_spec, o_spec,
                   pl.BlockSpec((ROW_TILE, 2 * ML_HEADS), row)),
        out_shape=(q_shape, jax.ShapeDtypeStruct((m, hq), BF16), v_shape, o_shape,
                   jax.ShapeDtypeStruct((m, 2 * ML_HEADS), F32)),
        compiler_params=_params("parallel", "parallel"),
        name="ml_proj_prompt" if feature_major else "ml_proj_sample",
    )(x, g, wk, wqvo, wg, bg)


def _ml_chunk_kernel(qt_ref, k_ref, vt_ref, ogt_ref, gt_ref, gtt_ref, gnb_ref,
                     a_ref, c_ref, n_ref, m_ref, ct_sc, n_sc, m_sc, *, t):
    n_sub = k_ref.shape[1] // t
    n_g = 2 * ML_HEADS
    ri = lax.broadcasted_iota(jnp.int32, (t, t), 0)
    ci = lax.broadcasted_iota(jnp.int32, (t, t), 1)
    causal_t = ri <= ci

    def gate_terms(gates, gates_t):
        b_cols = _dot_mask(ci <= ri, gates)
        b_rows = _mask_dot(gates_t, causal_t)
        w_cols = jnp.concatenate([gates[:, :ML_HEADS] - b_cols[:, ML_HEADS:], jnp.zeros((t, ML_HEADS), F32)],
                                 axis=1)
        left = jnp.concatenate([p.astype(F32) for p in _split3(w_cols)]
                               + [jnp.ones((t, n_g), F32), jnp.zeros((t, LANES - 4 * n_g), F32)],
                               axis=1).astype(BF16)
        return left, b_rows

    @pl.when(pl.program_id(1) == 0)
    def _():
        ct_sc[...] = jnp.zeros_like(ct_sc)
        n_sc[...] = jnp.zeros_like(n_sc)
        m_sc[...] = jnp.zeros_like(m_sc)

    terms = [gate_terms(gt_ref[0, c * t:(c + 1) * t, :], gtt_ref[0, :, c * t:(c + 1) * t])
             for c in range(n_sub)]
    right_pad = jnp.zeros((LANES - 4 * n_g, t), F32)
    sub = lax.broadcasted_iota(jnp.int32, (n_g, t), 0)
    sub8 = lax.broadcasted_iota(jnp.int32, (SUBLANES, t), 0)
    lane = lax.broadcasted_iota(jnp.int32, (t, LANES), 1)
    half = (lane < ML_DQK, lane >= ML_DQK)
    row_h = lax.broadcasted_iota(jnp.int32, (LANES, t), 0)
    half_rows = (row_h < ML_DQK, row_h >= ML_DQK)
    lane1 = lax.broadcasted_iota(jnp.int32, (1, LANES), 1)
    head_lane = lax.broadcasted_iota(jnp.int32, (1, ML_HEADS), 1)
    heads = range(ML_HEADS)
    pairs = range(ML_HEADS // 2)
    pj = lambda h: slice((h // 2) * LANES, (h // 2 + 1) * LANES)
    dv = lambda h: slice(h * ML_DV, (h + 1) * ML_DV)
    m_all = m_sc[...]
    ct = [ct_sc[j] for j in pairs]
    n2 = [n_sc[j] for j in pairs]

    for c in range(n_sub):
        tok = slice(c * t, (c + 1) * t)
        left, b_rows = terms[c]
        gates_t = gtt_ref[0, :, tok]
        m_out = jnp.zeros((1, ML_HEADS), F32)
        i_r = [gates_t[h:h + 1, :] for h in heads]
        b_r = [b_rows[ML_HEADS + h:ML_HEADS + h + 1, :] for h in heads]
        m_prev = [m_all[:, h:h + 1] for h in heads]
        km = [jnp.where(half[h % 2], k_ref[0, tok, pj(h)], jnp.zeros((t, LANES), BF16)) for h in heads]

        logw, kq, cq, qn = [], [], [], []
        for h in heads:
            qt2 = qt_ref[0, pj(h), tok]
            qtm = jnp.where(half_rows[h % 2], qt2, jnp.zeros_like(qt2))
            u_hi, u_mid, u_lo = (p.astype(F32) for p in _split3(b_r[h]))
            tail = jnp.where(sub == 0, u_hi, jnp.where(sub == 1, u_mid, jnp.where(sub == 2, u_lo, 0.0)))
            pick = (sub == h).astype(F32)
            right = jnp.concatenate([pick, pick, pick, tail, right_pad], axis=0).astype(BF16)
            logw.append(_dot(left, right))
            kq.append(_dot(km[h], qt2))
            cq.append(_dot(ct[h // 2].astype(BF16), qtm))
            qn.append(_dot(n2[h // 2].astype(BF16), qtm)[0:1, :])

        m_t, w_inter, a, a_sum, vtw, wk8, decay = [], [], [], [], [], [], []
        for h in heads:
            lw = jnp.where(causal_t, logw[h], NEG_INF)
            inter = b_r[h] + m_prev[h]
            m_t.append(jnp.maximum(inter, jnp.max(lw, axis=0, keepdims=True)))
            w_inter.append(jnp.exp(inter - m_t[h]))
            a_h = jnp.exp(lw - m_t[h]) * kq[h]
            a_sum.append(jnp.sum(a_h, axis=0, keepdims=True))
            a.append(a_h.astype(BF16))
            b_last = b_r[h][:, t - 1:t]
            m_new = m_t[h][:, t - 1:t]
            decay.append(jnp.exp(b_last + m_prev[h] - m_new))
            wk = jnp.exp(b_last - b_r[h] + i_r[h] - m_new)
            vtw.append((vt_ref[0, dv(h), tok].astype(F32) * wk).astype(BF16))
            wk8.append(jnp.where(sub8 == 0, wk, 0.0).astype(BF16))
            m_out = jnp.where(head_lane == h, m_new, m_out)

        va = [_dot(vt_ref[0, dv(h), tok], a[h]) for h in heads]
        c_add = [_dot(vtw[h], km[h]) for h in heads]
        n_add = [_dot(wk8[h], km[h]) for h in heads]

        for h in heads:
            hu = w_inter[h] * cq[h] + va[h]
            den = w_inter[h] * qn[h] + a_sum[h]
            r1 = 1.0 / jnp.maximum(jnp.abs(den), jnp.exp(-m_t[h]))
            ms = jnp.mean(hu * hu, axis=0, keepdims=True)
            hn = hu * (r1 * lax.rsqrt(r1 * r1 * ms + EPS)) * _lane_tile(gnb_ref[dv(h), :], t)
            a_ref[0, tok, dv(h)] = (ogt_ref[0, dv(h), tok] * hn).T.astype(BF16)
        d2 = [jnp.where(lane1 < ML_DQK, decay[2 * j], decay[2 * j + 1]) for j in pairs]
        ct = [d2[j] * ct[j] + c_add[2 * j] + c_add[2 * j + 1] for j in pairs]
        n2 = [d2[j] * n2[j] + n_add[2 * j] + n_add[2 * j + 1] for j in pairs]
        m_all = m_out

    for j in pairs:
        ct_sc[j] = ct[j]
        n_sc[j] = n2[j]
    m_sc[...] = m_all

    @pl.when(pl.program_id(1) == pl.num_programs(1) - 1)
    def _():
        for j in pairs:
            c_ref[0, j] = ct[j].T
            n_ref[0, j:j + 1, :] = n2[j][0:1, :]
        m_ref[0] = m_all


def _ml_step_kernel(q_ref, k_ref, v_ref, og_ref, gt_ref, gtt_ref, gn_ref, c0_ref, n0_ref, m0_ref,
                    a_ref, c_ref, n_ref, m_ref):
    g, t = q_ref.shape[0], q_ref.shape[1]
    gates = gt_ref[...]
    gates_t = gtt_ref[...]
    ri = lax.broadcasted_iota(jnp.int32, (g, t, t), 1)
    ci = lax.broadcasted_iota(jnp.int32, (g, t, t), 2)
    causal = ci <= ri
    eye = jnp.broadcast_to(
        (lax.broadcasted_iota(jnp.int32, (ML_DQK, ML_DQK), 0)
         == lax.broadcasted_iota(jnp.int32, (ML_DQK, ML_DQK), 1)).astype(BF16)[None], (g, ML_DQK, ML_DQK))
    head_lane = lax.broadcasted_iota(jnp.int32, (g, 1, ML_HEADS), 2)
    m_all = m0_ref[...]
    m_out = jnp.zeros((g, 1, ML_HEADS), F32)

    heads = range(ML_HEADS)
    dv = lambda h: slice(h * ML_DV, (h + 1) * ML_DV)
    bmm = functools.partial(jnp.einsum, preferred_element_type=F32)
    q = [q_ref[:, :, h * ML_DQK:(h + 1) * ML_DQK] for h in heads]
    k = [k_ref[:, :, h * ML_DQK:(h + 1) * ML_DQK] for h in heads]
    v = [v_ref[:, :, dv(h)] for h in heads]
    c0 = [c0_ref[:, h] for h in heads]
    n0 = [n0_ref[:, h:h + 1, :] for h in heads]
    m_prev = [m_all[:, :, h:h + 1] for h in heads]

    qk = [bmm("gtd,gsd->gts", q[h], k[h]) for h in heads]
    qc = [bmm("gtd,gde->gte", q[h], c0[h].astype(BF16)) for h in heads]
    kt = [bmm("gdk,gsk->gds", eye, k[h]) for h in heads]

    m_t, w_inter, a, a_sum, kwt, kw_sum, decay = [], [], [], [], [], [], []
    for h in heads:
        fh = ML_HEADS + h
        i_c, f_c = gates[:, :, h:h + 1], gates[:, :, fh:fh + 1]
        i_r, f_r = gates_t[:, h:h + 1, :], gates_t[:, fh:fh + 1, :]
        b_c = jnp.sum(jnp.where(causal, f_r, 0.0), axis=2, keepdims=True)
        b_r = jnp.sum(jnp.where(ri <= ci, f_c, 0.0), axis=1, keepdims=True)
        logw = jnp.where(causal, b_c - b_r + i_r, NEG_INF)
        inter = b_c + m_prev[h]
        m_t.append(jnp.maximum(inter, jnp.max(logw, axis=2, keepdims=True)))
        w_inter.append(jnp.exp(inter - m_t[h]))
        a_h = jnp.exp(logw - m_t[h]) * qk[h]
        a_sum.append(jnp.sum(a_h, axis=2, keepdims=True))
        a.append(a_h.astype(BF16))
        b_last = b_c[:, t - 1:t, :]
        m_new = m_t[h][:, t - 1:t, :]
        decay.append(jnp.exp(b_last + m_prev[h] - m_new))
        kw = jnp.exp(b_last - b_c + i_c - m_new) * k[h].astype(F32)
        kw_sum.append(jnp.sum(kw, axis=1, keepdims=True))
        kwt.append((kt[h] * jnp.exp(b_last - b_r + i_r - m_new)).astype(BF16))
        m_out = jnp.where(head_lane == h, m_new, m_out)

    av = [bmm("gts,gse->gte", a[h], v[h]) for h in heads]
    c_add = [bmm("gds,gse->gde", kwt[h], v[h]) for h in heads]

    for h in heads:
        num = w_inter[h] * qc[h] + av[h]
        qn = jnp.sum(q[h].astype(F32) * n0[h], axis=2, keepdims=True)
        den = w_inter[h] * qn + a_sum[h]
        hs = num / jnp.maximum(jnp.abs(den), jnp.exp(-m_t[h]))
        hn = hs * lax.rsqrt(jnp.mean(hs * hs, axis=2, keepdims=True) + EPS) * gn_ref[:, dv(h)]
        a_ref[:, :, dv(h)] = (og_ref[:, :, dv(h)] * hn).astype(BF16)
        c_ref[:, h] = decay[h] * c0[h] + c_add[h]
        n_ref[:, h:h + 1, :] = decay[h] * n0[h] + kw_sum[h]
    m_ref[...] = m_out


def _ml_chunk(qt, k, vt, ogt, gates, gates_t, gnb, chunk):
    nb, d, seq_len = vt.shape
    hq = k.shape[2]
    pairs = ML_HEADS // 2
    span = chunk * ML_STEP_CHUNKS
    tok = lambda width: pl.BlockSpec((1, span, width), lambda b, c: (b, c, 0))
    tok_t = lambda width: pl.BlockSpec((1, width, span), lambda b, c: (b, 0, c))
    return pl.pallas_call(
        functools.partial(_ml_chunk_kernel, t=chunk),
        grid=(nb, seq_len // span),
        in_specs=[tok_t(hq), tok(hq), tok_t(d), tok_t(d), tok(2 * ML_HEADS), tok_t(2 * ML_HEADS),
                  pl.BlockSpec((d, LANES), lambda b, c: (0, 0))],
        out_specs=(tok(d), pl.BlockSpec((1, pairs, LANES, LANES), lambda b, c: (b, 0, 0, 0)),
                   pl.BlockSpec((1, pairs, LANES), lambda b, c: (b, 0, 0)),
                   pl.BlockSpec((1, 1, ML_HEADS), lambda b, c: (b, 0, 0))),
        out_shape=(jax.ShapeDtypeStruct((nb, seq_len, d), BF16),
                   jax.ShapeDtypeStruct((nb, pairs, LANES, LANES), F32),
                   jax.ShapeDtypeStruct((nb, pairs, LANES), F32),
                   jax.ShapeDtypeStruct((nb, 1, ML_HEADS), F32)),
        scratch_shapes=[pltpu.VMEM((pairs, ML_DV, LANES), F32), pltpu.VMEM((pairs, SUBLANES, LANES), F32),
                        pltpu.VMEM((1, ML_HEADS), F32)],
        compiler_params=_params("parallel", "arbitrary"),
        name="ml_chunk",
    )(qt, k, vt, ogt, gates, gates_t, gnb)


def _ml_step(q, k, v, og, gates, gates_t, gn, c0, n0, m0):
    nb, t, d = v.shape
    hq = q.shape[2]
    g = ML_STEP_SEQS
    lead = lambda *rest: pl.BlockSpec((g, *rest), lambda i: (i,) + (0,) * len(rest))
    state = (lead(ML_HEADS, ML_DQK, ML_DV), lead(ML_HEADS, ML_DQK), lead(1, ML_HEADS))
    return pl.pallas_call(
        _ml_step_kernel,
        grid=(nb // g,),
        in_specs=[lead(t, hq), lead(t, hq), lead(t, d), lead(t, d), lead(t, 2 * ML_HEADS),
                  lead(2 * ML_HEADS, t), pl.BlockSpec((1, d), lambda i: (0, 0)), *state],
        out_specs=(lead(t, d), *state),
        out_shape=(jax.ShapeDtypeStruct((nb, t, d), BF16), jax.ShapeDtypeStruct(c0.shape, F32),
                   jax.ShapeDtypeStruct(n0.shape, F32), jax.ShapeDtypeStruct(m0.shape, F32)),
        compiler_params=_params("parallel"),
        name="ml_step",
    )(q, k, v, og, gates, gates_t, gn, c0, n0, m0)


def _pad_cols(w, width):
    return jnp.pad(w, ((0, 0), (0, width - w.shape[1])))


def kernel(x_prompt, x_sample, state_conv, cache_k, cache_v, cache_logf, page_table, state_C, state_n, state_m, norm_mix, norm_ffn, norm_final, w_conv_in, w_conv, w_conv_out, w_fox_in, b_fox_f, w_fox_out, w_ml_in, b_ml_gates, g_ml_norm, w_ml_out, w_ffn_gu, w_ffn_down):
    bp, lp, d = x_prompt.shape
    bs, ls, _ = x_sample.shape
    depth = norm_mix.shape[0]
    dff = w_ffn_down.shape[1]
    assert d == D_MODEL and ls == SUBLANES
    xp = x_prompt.reshape(bp * lp, d)
    xs = x_sample.reshape(bs * ls, d)
    gf = norm_final.reshape(1, d)
    hq = ML_HEADS * ML_DQK
    out = {name: [] for name in ("conv_p", "conv_s", "kp", "vp", "lfp", "ks", "vs", "lfs",
                                 "cp", "np", "mp", "cs", "ns", "ms")}

    for i in range(depth):
        kind, j = i % 3, i // 3
        gm = norm_mix[i].reshape(1, d)
        if kind == 0:
            win = w_conv_in[j].astype(BF16)
            wo = w_conv_out[j].astype(BF16)
            ap, sp = _conv_mixer(xp, gm, win, w_conv[j], jnp.zeros((bp, CONV_W - 1, d), F32), lp)
            as_, ss = _conv_mixer(xs, gm, win, w_conv[j], state_conv[j], ls)
            out["conv_p"].append(sp)
            out["conv_s"].append(ss)
        elif kind == 1:
            wqkv = w_fox_in[j][:, :3 * d].astype(BF16)
            wf = _pad_cols(w_fox_in[j][:, 3 * d:], LANES).astype(BF16)
            bf = _pad_cols(b_fox_f[j].reshape(1, FOX_HEADS), LANES)
            wo = w_fox_out[j].astype(BF16)
            qtb, kt, vt, kb, vtb, lft, c, ct = _fox_proj_prompt(
                xp, gm, wqkv.T, wf, wf[:, :FOX_HEADS].T, bf, b_fox_f[j].reshape(FOX_HEADS, 1), lp)
            ap = _fox_attn_prompt(qtb, kb, vtb, c, ct, lp)
            heads_last = lambda t: jnp.transpose(t.reshape(bp, FOX_HEADS, FOX_HEAD_DIM, lp), (0, 3, 1, 2))
            out["kp"].append(heads_last(kt))
            out["vp"].append(heads_last(vt))
            out["lfp"].append(jnp.swapaxes(lft, 1, 2))
            q, k, v, lf = _fox_proj_sample(xs, gm, wqkv, wf, bf)
            as_ = _fox_attn_sample(
                q.reshape(bs, ls, d), k.reshape(bs, ls, d), v.reshape(bs, ls, d),
                lf.reshape(bs, ls, FOX_HEADS), jnp.transpose(cache_k[j], (0, 2, 3, 1)),
                jnp.transpose(cache_v[j], (0, 2, 3, 1)), jnp.transpose(cache_logf[j], (0, 2, 1)),
                page_table).reshape(bs * ls, d)
            out["ks"].append(k.reshape(bs, ls, FOX_HEADS, FOX_HEAD_DIM))
            out["vs"].append(v.reshape(bs, ls, FOX_HEADS, FOX_HEAD_DIM))
            out["lfs"].append(lf.reshape(bs, ls, FOX_HEADS))
        else:
            w = w_ml_in[j]
            wk = w[:, hq:2 * hq].astype(BF16)
            wqvo = jnp.concatenate([w[:, :hq], w[:, 2 * hq:2 * hq + 2 * d]], axis=1).astype(BF16)
            wg = _pad_cols(w[:, 2 * hq + 2 * d:], LANES).astype(BF16)
            bg = _pad_cols(b_ml_gates[j].reshape(1, 2 * ML_HEADS), LANES)
            gn = g_ml_norm[j].reshape(1, d)
            wo = w_ml_out[j].astype(BF16)

            def tokens(nb, seq_len, *arrays):
                return [t.reshape(nb, seq_len, t.shape[-1]) for t in arrays]

            qt, k, vt, ogt, gates = _ml_proj(xp, gm, wk, wqvo.T, wg, bg, lp, feature_major=True)
            k, gates = tokens(bp, lp, k, gates)
            a, c1, n1, m1 = _ml_chunk(qt, k, vt, ogt, gates, jnp.swapaxes(gates, 1, 2),
                                      jnp.broadcast_to(gn.reshape(d, 1), (d, LANES)), ML_CHUNK_PROMPT)
            ap = a.reshape(bp * lp, d)
            out["cp"].append(c1.reshape(bp, ML_HEADS, ML_DQK, ML_DV))
            out["np"].append(n1.reshape(bp, ML_HEADS, ML_DQK))
            out["mp"].append(m1.reshape(bp, ML_HEADS))

            q, k, v, og, gates = _ml_proj(xs, gm, wk, wqvo, wg, bg, ROW_TILE, feature_major=False)
            q, k, v, og, gates = tokens(bs, ls, q, k, v, og, gates)
            a, c1, n1, m1 = _ml_step(q, k, v, og, gates, jnp.swapaxes(gates, 1, 2), gn,
                                     state_C[j], state_n[j], state_m[j].reshape(bs, 1, ML_HEADS))
            as_ = a.reshape(bs * ls, d)
            out["cs"].append(c1); out["ns"].append(n1); out["ms"].append(m1.reshape(bs, ML_HEADS))

        gffn = norm_ffn[i].reshape(1, d)
        wg_ffn = w_ffn_gu[i][:, :dff].astype(BF16)
        wu_ffn = w_ffn_gu[i][:, dff:].astype(BF16)
        wd_ffn = w_ffn_down[i].astype(BF16)
        last = i == depth - 1
        xp = _post(xp, ap, wo, gffn, wg_ffn, wu_ffn, wd_ffn, gf, final_norm=last)
        xs = _post(xs, as_, wo, gffn, wg_ffn, wu_ffn, wd_ffn, gf, final_norm=last)

    st = lambda name: jnp.stack(out[name])
    return (xp.reshape(bp, lp, d), xs.reshape(bs, ls, d), st("conv_p"), st("conv_s"),
            st("kp"), st("vp"), st("lfp"), st("ks"), st("vs"), st("lfs"),
            st("cp"), st("np"), st("mp"), st("cs"), st("ns"), st("ms"))
```

```python
import functools

import jax
import jax.numpy as jnp
from jax import lax
from jax.experimental import pallas as pl
from jax.experimental.pallas import tpu as pltpu

F32 = jnp.float32
BF16 = jnp.bfloat16

D_MODEL = 1024
CONV_W = 3
FOX_HEADS = 16
FOX_HEAD_DIM = D_MODEL // FOX_HEADS
ML_HEADS = 8
ML_DV = D_MODEL // ML_HEADS
ML_DQK = ML_DV // 2
GATE_CAP = 15.0
EPS = 1e-6
NEG_INF = -1e30
LOG2E = 1.4426950408889634

LANES = 128
SUBLANES = 8
ROW_TILE = 1024
FOX_PROJ_TILE = 512
POST_ROW_TILE = 1024
FFN_CHUNK = 256
CONV_CHUNK = 256
ATTN_TQ = 512
ML_CHUNK_PROMPT = 128
ML_STEP_CHUNKS = 8
ML_STEP_SEQS = 32
VMEM_LIMIT = 56 * 1024 * 1024


def _params(*sem):
    return pltpu.CompilerParams(dimension_semantics=sem, vmem_limit_bytes=VMEM_LIMIT)


def _resident(shape):
    return pl.BlockSpec(shape, lambda *_: (0, 0), pipeline_mode=pl.Buffered(1))


def _rms(x, g):
    return x * lax.rsqrt(jnp.mean(x * x, axis=-1, keepdims=True) + EPS) * g


def _log_sigmoid(x):
    return jnp.minimum(x, 0.0) - jnp.log1p(jnp.exp(-jnp.abs(x)))


def _dot(a, b):
    return jnp.dot(a, b, preferred_element_type=F32)


def _dot_nt(a, b, precision=None):
    return lax.dot_general(a, b, (((1,), (1,)), ((), ())), precision=precision,
                           preferred_element_type=F32)


def _lane_tile(x, width):
    return jnp.tile(x, (1, width // LANES))


def _post_kernel(x_ref, a_ref, wo_ref, g_ref, wg_ref, wu_ref, wd_ref, gf_ref, o_ref, *, final_norm):
    x1 = x_ref[...] + _dot(a_ref[...], wo_ref[...])
    h = _rms(x1, g_ref[...]).astype(BF16)
    y = x1
    for c in range(0, wg_ref.shape[1], FFN_CHUNK):
        g = _dot(h, wg_ref[:, c:c + FFN_CHUNK])
        u = _dot(h, wu_ref[:, c:c + FFN_CHUNK])
        y = y + _dot(((g * jax.nn.sigmoid(g)) * u).astype(BF16), wd_ref[c:c + FFN_CHUNK, :])
    o_ref[...] = _rms(y, gf_ref[...]) if final_norm else y


def _post(x, a, wo, g, wg, wu, wd, gf, *, final_norm):
    m, d = x.shape
    dff = wg.shape[1]
    tm = min(POST_ROW_TILE, m)
    assert dff % FFN_CHUNK == 0 and m % tm == 0
    row = pl.BlockSpec((tm, d), lambda i: (i, 0))
    return pl.pallas_call(
        functools.partial(_post_kernel, final_norm=final_norm),
        grid=(m // tm,),
        in_specs=[row, row, _resident((d, d)), _resident((1, d)), _resident((d, dff)), _resident((d, dff)),
                  _resident((dff, d)), _resident((1, d))],
        out_specs=row,
        out_shape=jax.ShapeDtypeStruct((m, d), F32),
        compiler_params=_params("parallel"),
        name="post_ffn",
    )(x, a, wo, g, wg, wu, wd, gf)


def _conv_columns(x_ref, g_ref, win_ref, wc_ref, z_ref, rig, prev_rows, keep_tail):
    d = x_ref.shape[-1]
    h = _rms(x_ref[...], g_ref[...]).astype(BF16)
    for c in range(0, d, CONV_CHUNK):
        cols = slice(c, c + CONV_CHUNK)
        bg = _dot(h, win_ref[:, c:c + CONV_CHUNK])
        u = (_dot(h, win_ref[:, d + c:d + c + CONV_CHUNK])
             * _dot(h, win_ref[:, 2 * d + c:2 * d + c + CONV_CHUNK]))
        p0, p1 = prev_rows(cols)
        s1 = jnp.where(rig == 0, p1, pltpu.roll(u, 1, axis=0))
        s2 = jnp.where(rig == 0, p0, jnp.where(rig == 1, p1, pltpu.roll(u, 2, axis=0)))
        conv = wc_ref[0:1, cols] * s2 + wc_ref[1:2, cols] * s1 + wc_ref[2:3, cols] * u
        z_ref[:, cols] = (bg * conv).astype(BF16)
        keep_tail(cols, u)


def _conv_prompt_kernel(x_ref, g_ref, win_ref, wc_ref, prev_ref, z_ref, st_ref, carry_sc):
    @pl.when(pl.program_id(1) == 0)
    def _():
        carry_sc[...] = prev_ref[0]

    r = x_ref.shape[0]
    rig = lax.broadcasted_iota(jnp.int32, (r, CONV_CHUNK), 0)

    def keep_tail(cols, u):
        tail = u[r - (CONV_W - 1):, :]
        carry_sc[:, cols] = tail
        st_ref[0, :, cols] = tail

    _conv_columns(x_ref, g_ref, win_ref, wc_ref, z_ref, rig,
                  lambda cols: (carry_sc[0:1, cols], carry_sc[1:2, cols]), keep_tail)


def _conv_sample_kernel(x_ref, g_ref, win_ref, wc_ref, prev_ref, z_ref, st_ref):
    r = x_ref.shape[0]
    n_seq = r // SUBLANES
    rig = lax.broadcasted_iota(jnp.int32, (r, CONV_CHUNK), 0) & (SUBLANES - 1)

    def prev_rows(cols):
        prev = prev_ref[:, :, cols]
        rows = lambda i: jnp.broadcast_to(prev[:, i:i + 1, :], (n_seq, SUBLANES, CONV_CHUNK)).reshape(r, CONV_CHUNK)
        return rows(0), rows(1)

    def keep_tail(cols, u):
        st_ref[:, :, cols] = u.reshape(n_seq, SUBLANES, CONV_CHUNK)[:, SUBLANES - (CONV_W - 1):, :]

    _conv_columns(x_ref, g_ref, win_ref, wc_ref, z_ref, rig, prev_rows, keep_tail)


def _conv_mixer(x, g, win, wc, prev, seq_len):
    m, d = x.shape
    n_seq = m // seq_len
    common_in = [_resident((d, 3 * d)), _resident((CONV_W, d))]
    out_shape = (jax.ShapeDtypeStruct((m, d), BF16), jax.ShapeDtypeStruct((n_seq, CONV_W - 1, d), F32))
    if seq_len == SUBLANES:
        per = ROW_TILE // SUBLANES
        return pl.pallas_call(
            _conv_sample_kernel,
            grid=(m // ROW_TILE,),
            in_specs=[pl.BlockSpec((ROW_TILE, d), lambda i: (i, 0)), pl.BlockSpec((1, d), lambda i: (0, 0)),
                      *common_in, pl.BlockSpec((per, CONV_W - 1, d), lambda i: (i, 0, 0))],
            out_specs=(pl.BlockSpec((ROW_TILE, d), lambda i: (i, 0)),
                       pl.BlockSpec((per, CONV_W - 1, d), lambda i: (i, 0, 0))),
            out_shape=out_shape,
            compiler_params=_params("parallel"),
            name="conv_sample",
        )(x, g, win, wc, prev)
    assert seq_len % ROW_TILE == 0
    nl = seq_len // ROW_TILE
    return pl.pallas_call(
        _conv_prompt_kernel,
        grid=(n_seq, nl),
        in_specs=[pl.BlockSpec((ROW_TILE, d), lambda b, l: (b * nl + l, 0)),
                  pl.BlockSpec((1, d), lambda b, l: (0, 0)),
                  *common_in, pl.BlockSpec((1, CONV_W - 1, d), lambda b, l: (b, 0, 0))],
        out_specs=(pl.BlockSpec((ROW_TILE, d), lambda b, l: (b * nl + l, 0)),
                   pl.BlockSpec((1, CONV_W - 1, d), lambda b, l: (b, 0, 0))),
        out_shape=out_shape,
        scratch_shapes=[pltpu.VMEM((CONV_W - 1, d), F32)],
        compiler_params=_params("parallel", "arbitrary"),
        name="conv_prompt",
    )(x, g, win, wc, prev)


def _fox_proj_kernel(x_ref, g_ref, wqkv_ref, wf_ref, bf_ref, q_ref, k_ref, v_ref, lf_ref):
    d = x_ref.shape[-1]
    h = _rms(x_ref[...], g_ref[...]).astype(BF16)
    p = _dot(h, wqkv_ref[...])
    logf = _log_sigmoid(_dot(h, wf_ref[...]) + bf_ref[...])
    q_ref[...] = p[:, :d] * (FOX_HEAD_DIM ** -0.5)
    k_ref[...] = p[:, d:2 * d]
    v_ref[...] = p[:, 2 * d:]
    lf_ref[...] = logf[:, :FOX_HEADS]


def _fox_proj_sample(x, g, wqkv, wf, bf):
    m, d = x.shape
    row = lambda i: (i, 0)
    const = lambda i: (0, 0)
    big = pl.BlockSpec((ROW_TILE, d), row)
    f32_out = jax.ShapeDtypeStruct((m, d), F32)
    return pl.pallas_call(
        _fox_proj_kernel,
        grid=(m // ROW_TILE,),
        in_specs=[big, pl.BlockSpec((1, d), const), _resident((d, 3 * d)),
                  pl.BlockSpec((d, LANES), const), pl.BlockSpec((1, LANES), const)],
        out_specs=(big, big, big, pl.BlockSpec((ROW_TILE, FOX_HEADS), row)),
        out_shape=(f32_out, f32_out, f32_out, jax.ShapeDtypeStruct((m, FOX_HEADS), F32)),
        compiler_params=_params("parallel"),
        name="fox_proj_sample",
    )(x, g, wqkv, wf, bf)


def _fox_proj_prompt_kernel(x_ref, g_ref, wqkvt_ref, wf_ref, wft_ref, bf_ref, bft_ref,
                            qtb_ref, kt_ref, vt_ref, kb_ref, vtb_ref, lft_ref, c_ref, ct_ref,
                            carry_sc, carryt_sc):
    d = x_ref.shape[-1]
    r = x_ref.shape[0]
    h = _rms(x_ref[...], g_ref[...]).astype(BF16)
    qkv = _dot_nt(wqkvt_ref[...], h)
    qtb_ref[0] = (qkv[:d] * (FOX_HEAD_DIM ** -0.5 * LOG2E)).astype(BF16)
    kt_ref[0] = qkv[d:2 * d]
    vt_ref[0] = qkv[2 * d:]
    kb_ref[...] = qkv[d:2 * d].T.astype(BF16)
    vtb_ref[0] = qkv[2 * d:].astype(BF16)
    logf = _log_sigmoid(_dot(h, wf_ref[...]) + bf_ref[...])
    logf_t = _log_sigmoid(_dot_nt(wft_ref[...], h) + bft_ref[...])
    lft_ref[0] = logf_t

    @pl.when(pl.program_id(1) == 0)
    def _():
        carry_sc[...] = jnp.zeros_like(carry_sc)
        carryt_sc[...] = jnp.zeros_like(carryt_sc)

    ri = lax.broadcasted_iota(jnp.int32, (r, r), 0)
    ci = lax.broadcasted_iota(jnp.int32, (r, r), 1)
    c = carry_sc[...] + _dot_mask(ci <= ri, logf)
    carry_sc[...] = c[r - 1:r, :]
    c_ref[...] = c[:, :FOX_HEADS]
    ct = _lane_tile(carryt_sc[...], r) + _mask_dot(logf_t, ri <= ci)
    carryt_sc[...] = jnp.broadcast_to(ct[:, r - 1:r], carryt_sc.shape)
    ct_ref[0] = ct


def _fox_proj_prompt(x, g, wqkvt, wf, wft, bf, bft, seq_len):
    m, d = x.shape
    n_seq = m // seq_len
    tile = FOX_PROJ_TILE
    nl = seq_len // tile
    row = lambda b, l: (b * nl + l, 0)
    const = lambda b, l: (0, 0)
    feat = lambda width: pl.BlockSpec((1, width, tile), lambda b, l: (b, 0, l))
    feat_shape = lambda width, dtype: jax.ShapeDtypeStruct((n_seq, width, seq_len), dtype)
    return pl.pallas_call(
        _fox_proj_prompt_kernel,
        grid=(n_seq, nl),
        in_specs=[pl.BlockSpec((tile, d), row), pl.BlockSpec((1, d), const), _resident((3 * d, d)),
                  pl.BlockSpec((d, LANES), const), pl.BlockSpec((FOX_HEADS, d), const),
                  pl.BlockSpec((1, LANES), const), pl.BlockSpec((FOX_HEADS, 1), const)],
        out_specs=(feat(d), feat(d), feat(d), pl.BlockSpec((tile, d), row), feat(d),
                   feat(FOX_HEADS), pl.BlockSpec((tile, FOX_HEADS), row), feat(FOX_HEADS)),
        out_shape=(feat_shape(d, BF16), feat_shape(d, F32), feat_shape(d, F32),
                   jax.ShapeDtypeStruct((m, d), BF16), feat_shape(d, BF16),
                   feat_shape(FOX_HEADS, F32), jax.ShapeDtypeStruct((m, FOX_HEADS), F32),
                   feat_shape(FOX_HEADS, F32)),
        scratch_shapes=[pltpu.VMEM((1, LANES), F32), pltpu.VMEM((FOX_HEADS, LANES), F32)],
        compiler_params=_params("parallel", "arbitrary"),
        name="fox_proj_prompt",
    )(x, g, wqkvt, wf, wft, bf, bft)


def _fox_attn_kernel(qt_ref, k_ref, vt_ref, c_ref, ct_ref, o_ref, kaug_sc, vaug_sc, m_sc, acc_sc, *, tq):
    hp = pl.program_id(1)
    seq = k_ref.shape[0]
    hd = FOX_HEAD_DIM
    n_aug = 2 * SUBLANES
    n_bias = 3
    one_row = jnp.where(lax.broadcasted_iota(jnp.int32, (n_aug, seq), 0) == 0, 1.0, 0.0).astype(BF16)

    for hh in range(2):
        vaug_sc[hh] = jnp.concatenate([vt_ref[0, hh * hd:(hh + 1) * hd, :], one_row], axis=0)
    c3 = jnp.concatenate([p.astype(F32) for p in _split3(c_ref[...] * LOG2E)]
                         + [jnp.zeros((seq, LANES - n_bias * FOX_HEADS), F32)], axis=1).astype(BF16)
    src = lax.broadcasted_iota(jnp.int32, (LANES, LANES), 0)
    dst = lax.broadcasted_iota(jnp.int32, (LANES, LANES), 1)
    term = src >> (FOX_HEADS.bit_length() - 1)
    lane = lax.broadcasted_iota(jnp.int32, (seq, LANES), 1)
    k2 = k_ref[...]
    for hh in range(2):
        off = hd * (1 - hh)
        place = jnp.where((src & (FOX_HEADS - 1)) == 2 * hp + hh,
                          jnp.where(dst == off + term, -1.0, 0.0), 0.0).astype(BF16)
        ones = jnp.where(lane >= off + n_bias, jnp.where(lane < off + 2 * n_bias, 1.0, 0.0), 0.0)
        aug = (_dot(c3, place) + ones).astype(BF16)
        own = (lane >= hd) if hh else (lane < hd)
        kaug_sc[hh] = jnp.where(own, k2, aug)

    sub = lax.broadcasted_iota(jnp.int32, (n_aug, tq), 0)
    visible = (lax.broadcasted_iota(jnp.int32, (tq, tq), 0)
               <= lax.broadcasted_iota(jnp.int32, (tq, tq), 1))
    rest = jnp.zeros((hd - n_aug, tq), BF16)

    for n in range(seq // tq):
        cols = slice(n * tq, (n + 1) * tq)
        qt2 = qt_ref[0, :, cols]
        qa = []
        for hh in range(2):
            c_hi, c_mid, c_lo = (p.astype(F32)
                                 for p in _split3(ct_ref[0, pl.ds(2 * hp + hh, 1), cols] * LOG2E))
            piece = jnp.where(sub < n_bias, 1.0, jnp.where(sub == n_bias, c_hi, jnp.where(
                sub == n_bias + 1, c_mid, jnp.where(sub == n_bias + 2, c_lo, 0.0)))).astype(BF16)
            qa.append(jnp.concatenate([qt2[:hd], piece, rest] if hh == 0 else [piece, rest, qt2[hd:]], axis=0))
            m_sc[hh] = jnp.full((1, tq), NEG_INF, F32)
        acc_sc[...] = jnp.zeros_like(acc_sc)

        def scores(ki):
            return [_dot(kaug_sc[hh, ki * tq:(ki + 1) * tq, :], qa[hh]) for hh in range(2)]

        def absorb(ki, s2, diagonal):
            p2, alpha2 = [], []
            for hh in range(2):
                s = jnp.where(visible, s2[hh], NEG_INF) if diagonal else s2[hh]
                m_prev = m_sc[hh]
                m_new = jnp.maximum(m_prev, jnp.max(s, axis=0, keepdims=True))
                alpha2.append(jnp.exp2(m_prev - m_new))
                p2.append(jnp.exp2(s - m_new).astype(BF16))
                m_sc[hh] = m_new
            pv2 = [_dot(vaug_sc[hh, :, ki * tq:(ki + 1) * tq], p2[hh]) for hh in range(2)]
            for hh in range(2):
                acc_sc[hh] = alpha2[hh] * acc_sc[hh] + pv2[hh]

        s_next = scores(0)
        for ki in range(n + 1):
            s_cur = s_next
            if ki < n:
                s_next = scores(ki + 1)
            absorb(ki, s_cur, ki == n)

        o_t = jnp.concatenate([acc_sc[hh, :hd, :] / acc_sc[hh, hd:hd + 1, :] for hh in range(2)], axis=0)
        o_ref[cols, :] = o_t.T.astype(BF16)


def _fox_attn_prompt(qtb, kb, vtb, c, ct, seq_len):
    n_seq, d, _ = qtb.shape
    tq = ATTN_TQ
    pair = pl.BlockSpec((1, LANES, seq_len), lambda b, hp: (b, hp, 0))
    return pl.pallas_call(
        functools.partial(_fox_attn_kernel, tq=tq),
        grid=(n_seq, d // LANES),
        in_specs=[pair, pl.BlockSpec((seq_len, LANES), lambda b, hp: (b, hp)), pair,
                  pl.BlockSpec((seq_len, FOX_HEADS), lambda b, hp: (b, 0)),
                  pl.BlockSpec((1, FOX_HEADS, seq_len), lambda b, hp: (b, 0, 0))],
        out_specs=pl.BlockSpec((seq_len, LANES), lambda b, hp: (b, hp)),
        out_shape=jax.ShapeDtypeStruct((n_seq * seq_len, d), BF16),
        scratch_shapes=[pltpu.VMEM((2, seq_len, LANES), BF16),
                        pltpu.VMEM((2, FOX_HEAD_DIM + 2 * SUBLANES, seq_len), BF16),
                        pltpu.VMEM((2, 1, tq), F32),
                        pltpu.VMEM((2, FOX_HEAD_DIM + 2 * SUBLANES, tq), F32)],
        compiler_params=_params("parallel", "parallel"),
        name="fox_attn_prompt",
    )(qtb, kb, vtb, c, ct)


def _split3(x):
    hi = x.astype(BF16)
    r1 = x - hi.astype(F32)
    mid = r1.astype(BF16)
    return hi, mid, (r1 - mid.astype(F32)).astype(BF16)


def _mask_dot(x, mask):
    mask = mask.astype(BF16)
    hi, mid, lo = _split3(x)
    return _dot(hi, mask) + _dot(mid, mask) + _dot(lo, mask)


def _dot_mask(mask, x, nt=False):
    mask = mask.astype(BF16)
    dot = _dot_nt if nt else _dot
    hi, mid, lo = _split3(x)
    return dot(mask, hi) + dot(mask, mid) + dot(mask, lo)


def _fox_decode_kernel(pt_ref, q_ref, kn_ref, vn_ref, lfn_ref, *refs, n_pages):
    del pt_ref
    kt_refs, vt_refs, lft_refs = refs[:n_pages], refs[n_pages:2 * n_pages], refs[2 * n_pages:3 * n_pages]
    o_ref = refs[3 * n_pages]
    page = kt_refs[0].shape[-1]
    n_new, d = q_ref.shape[1], q_ref.shape[2]
    rows = FOX_HEADS * n_new
    assert rows == page == LANES and n_new == SUBLANES
    ri = lax.broadcasted_iota(jnp.int32, (rows, page), 0)
    ki = lax.broadcasted_iota(jnp.int32, (rows, page), 1)
    tok_bits = n_new.bit_length() - 1
    dim_bits = FOX_HEAD_DIM.bit_length() - 1
    head_of_row = ((lax.broadcasted_iota(jnp.int32, (rows, FOX_HEADS), 0) >> tok_bits)
                   == lax.broadcasted_iota(jnp.int32, (rows, FOX_HEADS), 1))
    own_cols = ((lax.broadcasted_iota(jnp.int32, (rows, d), 0) >> tok_bits)
                == (lax.broadcasted_iota(jnp.int32, (rows, d), 1) >> dim_bits))
    qbd = jnp.where(own_cols, jnp.tile(q_ref[0], (FOX_HEADS, 1)), 0.0).astype(BF16)

    pad = page - n_new
    kn = jnp.concatenate([kn_ref[0], jnp.zeros((pad, d), F32)], axis=0).astype(BF16)
    vn = jnp.concatenate([vn_ref[0], jnp.zeros((pad, d), F32)], axis=0).astype(BF16)
    lfn = jnp.concatenate([lfn_ref[0], jnp.zeros((pad, FOX_HEADS), F32)], axis=0)
    lf_rows = _dot_mask(head_of_row, lfn, nt=True)
    cum = _mask_dot(lf_rows, ri <= ki)
    t_of_row = ri & (n_new - 1)
    a = jnp.sum(jnp.where(ki == t_of_row, cum, 0.0), axis=1, keepdims=True)
    s_new = jnp.where(ki <= t_of_row, _dot_nt(qbd, kn) + a - cum, NEG_INF)

    lft = jnp.concatenate([r[...] for r in lft_refs], axis=0)
    sfx_in = _mask_dot(lft, ri >= ki)
    sfx_ex = sfx_in - lft
    later = jnp.zeros((FOX_HEADS, 1), F32)
    bias = [None] * n_pages
    for i in reversed(range(n_pages)):
        bias[i] = sfx_ex[i * FOX_HEADS:(i + 1) * FOX_HEADS, :] + later
        later = later + sfx_in[i * FOX_HEADS:(i + 1) * FOX_HEADS, 0:1]
    bias = _dot_mask(head_of_row, jnp.concatenate(bias, axis=1))
    s_old = jnp.concatenate(
        [_dot(qbd, r[...].reshape(d, page).astype(BF16)) for r in kt_refs], axis=1) + bias + a

    m = jnp.maximum(jnp.max(s_new, axis=1, keepdims=True), jnp.max(s_old, axis=1, keepdims=True))
    p_new = jnp.exp(s_new - m)
    p_old = jnp.exp(s_old - m)
    denom = jnp.sum(p_new, axis=1, keepdims=True) + jnp.sum(p_old, axis=1, keepdims=True)
    p_old = p_old.astype(BF16)
    acc = _dot(p_new.astype(BF16), vn)
    for i, r in enumerate(vt_refs):
        acc = acc + _dot_nt(p_old[:, i * page:(i + 1) * page], r[...].reshape(d, page).astype(BF16))
    o = jnp.where(own_cols, acc / denom, 0.0).reshape(FOX_HEADS, n_new, d)
    o_ref[0] = jnp.sum(o, axis=0).astype(BF16)


def _fox_attn_sample(q, k_new, v_new, lf_new, cache_kt, cache_vt, cache_lft, page_table):
    nb, n_new, d = q.shape
    n_pages = page_table.shape[1]
    page = cache_kt.shape[-1]

    def new_map(b, pt):
        return (b, 0, 0)

    def kv_spec(i):
        return pl.BlockSpec((None, FOX_HEADS, FOX_HEAD_DIM, page), lambda b, pt: (pt[b * n_pages + i], 0, 0, 0))

    def lf_spec(i):
        return pl.BlockSpec((None, FOX_HEADS, page), lambda b, pt: (pt[b * n_pages + i], 0, 0))

    pages = range(n_pages)
    grid_spec = pltpu.PrefetchScalarGridSpec(
        num_scalar_prefetch=1,
        grid=(nb,),
        in_specs=[pl.BlockSpec((1, n_new, d), new_map), pl.BlockSpec((1, n_new, d), new_map),
                  pl.BlockSpec((1, n_new, d), new_map), pl.BlockSpec((1, n_new, FOX_HEADS), new_map),
                  *[kv_spec(i) for i in pages], *[kv_spec(i) for i in pages], *[lf_spec(i) for i in pages]],
        out_specs=pl.BlockSpec((1, n_new, d), new_map),
    )
    return pl.pallas_call(
        functools.partial(_fox_decode_kernel, n_pages=n_pages),
        grid_spec=grid_spec,
        out_shape=jax.ShapeDtypeStruct((nb, n_new, d), BF16),
        compiler_params=_params("parallel"),
        name="fox_attn_sample",
    )(page_table.reshape(-1), q, k_new, v_new, lf_new,
      *([cache_kt] * n_pages), *([cache_vt] * n_pages), *([cache_lft] * n_pages))


def _ml_proj_kernel(x_ref, g_ref, wk_ref, wqvo_ref, wg_ref, bg_ref,
                    q_ref, k_ref, v_ref, o_ref, gt_ref, *, feature_major):
    d = x_ref.shape[-1]
    hq = ML_HEADS * ML_DQK
    h = _rms(x_ref[...], g_ref[...]).astype(BF16)
    k_ref[...] = (_dot(h, wk_ref[...]) * (ML_DQK ** -0.5)).astype(BF16)
    if feature_major:
        p = _dot_nt(wqvo_ref[...], h)
        q_ref[0] = p[:hq].astype(BF16)
        v_ref[0] = p[hq:hq + d].astype(BF16)
        o_ref[0] = jax.nn.sigmoid(p[hq + d:])
    else:
        p = _dot(h, wqvo_ref[...])
        q_ref[...] = p[:, :hq].astype(BF16)
        v_ref[...] = p[:, hq:hq + d].astype(BF16)
        o_ref[...] = jax.nn.sigmoid(p[:, hq + d:])
    gates = _dot(h, wg_ref[...]) + bg_ref[...]
    gates = GATE_CAP * jnp.tanh(gates / GATE_CAP)
    lane = lax.broadcasted_iota(jnp.int32, gates.shape, 1)
    gates = jnp.where(lane < ML_HEADS, gates, _log_sigmoid(gates))
    gt_ref[...] = gates[:, :2 * ML_HEADS]


def _ml_proj(x, g, wk, wqvo, wg, bg, seq_len, *, feature_major):
    m, d = x.shape
    hq = ML_HEADS * ML_DQK
    n_seq = m // seq_len
    nl = seq_len // ROW_TILE
    row = lambda b, l: (b * nl + l, 0)
    const = lambda b, l: (0, 0)

    def out(width, dtype):
        if feature_major:
            return (pl.BlockSpec((1, width, ROW_TILE), lambda b, l: (b, 0, l)),
                    jax.ShapeDtypeStruct((n_seq, width, seq_len), dtype))
        return pl.BlockSpec((ROW_TILE, width), row), jax.ShapeDtypeStruct((m, width), dtype)

    (q_spec, q_shape), (v_spec, v_shape), (o_spec, o_shape) = out(hq, BF16), out(d, BF16), out(d, F32)
    return pl.pallas_call(
        functools.partial(_ml_proj_kernel, feature_major=feature_major),
        grid=(n_seq, nl),
        in_specs=[pl.BlockSpec((ROW_TILE, d), row), pl.BlockSpec((1, d), const),
                  _resident(wk.shape), _resident(wqvo.shape),
                  pl.BlockSpec((d, LANES), const), pl.BlockSpec((1, LANES), const)],
        out_specs=(q_spec, pl.BlockSpec((ROW_TILE, hq), row), v_spec, o_spec,
                   pl.BlockSpec((ROW_TILE, 2 * ML_HEADS), row)),
        out_shape=(q_shape, jax.ShapeDtypeStruct((m, hq), BF16), v_shape, o_shape,
                   jax.ShapeDtypeStruct((m, 2 * ML_HEADS), F32)),
        compiler_params=_params("parallel", "parallel"),
        name="ml_proj_prompt" if feature_major else "ml_proj_sample",
    )(x, g, wk, wqvo, wg, bg)


def _ml_chunk_kernel(qt_ref, k_ref, vt_ref, ogt_ref, gt_ref, gtt_ref, gnb_ref,
                     a_ref, c_ref, n_ref, m_ref, ct_sc, n_sc, m_sc, *, t):
    n_sub = k_ref.shape[1] // t
    n_g = 2 * ML_HEADS
    ri = lax.broadcasted_iota(jnp.int32, (t, t), 0)
    ci = lax.broadcasted_iota(jnp.int32, (t, t), 1)
    causal_t = ri <= ci

    def gate_terms(gates, gates_t):
        b_cols = _dot_mask(ci <= ri, gates)
        b_rows = _mask_dot(gates_t, causal_t)
        w_cols = jnp.concatenate([gates[:, :ML_HEADS] - b_cols[:, ML_HEADS:], jnp.zeros((t, ML_HEADS), F32)],
                                 axis=1)
        left = jnp.concatenate([p.astype(F32) for p in _split3(w_cols)]
                               + [jnp.ones((t, n_g), F32), jnp.zeros((t, LANES - 4 * n_g), F32)],
                               axis=1).astype(BF16)
        return left, b_rows

    @pl.when(pl.program_id(1) == 0)
    def _():
        ct_sc[...] = jnp.zeros_like(ct_sc)
        n_sc[...] = jnp.zeros_like(n_sc)
        m_sc[...] = jnp.zeros_like(m_sc)

    terms = [gate_terms(gt_ref[0, c * t:(c + 1) * t, :], gtt_ref[0, :, c * t:(c + 1) * t])
             for c in range(n_sub)]
    right_pad = jnp.zeros((LANES - 4 * n_g, t), F32)
    sub = lax.broadcasted_iota(jnp.int32, (n_g, t), 0)
    sub8 = lax.broadcasted_iota(jnp.int32, (SUBLANES, t), 0)
    lane = lax.broadcasted_iota(jnp.int32, (t, LANES), 1)
    half = (lane < ML_DQK, lane >= ML_DQK)
    row_h = lax.broadcasted_iota(jnp.int32, (LANES, t), 0)
    half_rows = (row_h < ML_DQK, row_h >= ML_DQK)
    lane1 = lax.broadcasted_iota(jnp.int32, (1, LANES), 1)
    head_lane = lax.broadcasted_iota(jnp.int32, (1, ML_HEADS), 1)
    heads = range(ML_HEADS)
    pairs = range(ML_HEADS // 2)
    pj = lambda h: slice((h // 2) * LANES, (h // 2 + 1) * LANES)
    dv = lambda h: slice(h * ML_DV, (h + 1) * ML_DV)
    m_all = m_sc[...]
    ct = [ct_sc[j] for j in pairs]
    n2 = [n_sc[j] for j in pairs]

    for c in range(n_sub):
        tok = slice(c * t, (c + 1) * t)
        left, b_rows = terms[c]
        gates_t = gtt_ref[0, :, tok]
        m_out = jnp.zeros((1, ML_HEADS), F32)
        i_r = [gates_t[h:h + 1, :] for h in heads]
        b_r = [b_rows[ML_HEADS + h:ML_HEADS + h + 1, :] for h in heads]
        m_prev = [m_all[:, h:h + 1] for h in heads]
        km = [jnp.where(half[h % 2], k_ref[0, tok, pj(h)], jnp.zeros((t, LANES), BF16)) for h in heads]

        logw, kq, cq, qn = [], [], [], []
        for h in heads:
            qt2 = qt_ref[0, pj(h), tok]
            qtm = jnp.where(half_rows[h % 2], qt2, jnp.zeros_like(qt2))
            u_hi, u_mid, u_lo = (p.astype(F32) for p in _split3(b_r[h]))
            tail = jnp.where(sub == 0, u_hi, jnp.where(sub == 1, u_mid, jnp.where(sub == 2, u_lo, 0.0)))
            pick = (sub == h).astype(F32)
            right = jnp.concatenate([pick, pick, pick, tail, right_pad], axis=0).astype(BF16)
            logw.append(_dot(left, right))
            kq.append(_dot(km[h], qt2))
            cq.append(_dot(ct[h // 2].astype(BF16), qtm))
            qn.append(_dot(n2[h // 2].astype(BF16), qtm)[0:1, :])

        m_t, w_inter, a, a_sum, vtw, wk8, decay = [], [], [], [], [], [], []
        for h in heads:
            lw = jnp.where(causal_t, logw[h], NEG_INF)
            inter = b_r[h] + m_prev[h]
            m_t.append(jnp.maximum(inter, jnp.max(lw, axis=0, keepdims=True)))
            w_inter.append(jnp.exp(inter - m_t[h]))
            a_h = jnp.exp(lw - m_t[h]) * kq[h]
            a_sum.append(jnp.sum(a_h, axis=0, keepdims=True))
            a.append(a_h.astype(BF16))
            b_last = b_r[h][:, t - 1:t]
            m_new = m_t[h][:, t - 1:t]
            decay.append(jnp.exp(b_last + m_prev[h] - m_new))
            wk = jnp.exp(b_last - b_r[h] + i_r[h] - m_new)
            vtw.append((vt_ref[0, dv(h), tok].astype(F32) * wk).astype(BF16))
            wk8.append(jnp.where(sub8 == 0, wk, 0.0).astype(BF16))
            m_out = jnp.where(head_lane == h, m_new, m_out)

        va = [_dot(vt_ref[0, dv(h), tok], a[h]) for h in heads]
        c_add = [_dot(vtw[h], km[h]) for h in heads]
        n_add = [_dot(wk8[h], km[h]) for h in heads]

        for h in heads:
            hu = w_inter[h] * cq[h] + va[h]
            den = w_inter[h] * qn[h] + a_sum[h]
            r1 = 1.0 / jnp.maximum(jnp.abs(den), jnp.exp(-m_t[h]))
            ms = jnp.mean(hu * hu, axis=0, keepdims=True)
            hn = hu * (r1 * lax.rsqrt(r1 * r1 * ms + EPS)) * _lane_tile(gnb_ref[dv(h), :], t)
            a_ref[0, tok, dv(h)] = (ogt_ref[0, dv(h), tok] * hn).T.astype(BF16)
        d2 = [jnp.where(lane1 < ML_DQK, decay[2 * j], decay[2 * j + 1]) for j in pairs]
        ct = [d2[j] * ct[j] + c_add[2 * j] + c_add[2 * j + 1] for j in pairs]
        n2 = [d2[j] * n2[j] + n_add[2 * j] + n_add[2 * j + 1] for j in pairs]
        m_all = m_out

    for j in pairs:
        ct_sc[j] = ct[j]
        n_sc[j] = n2[j]
    m_sc[...] = m_all

    @pl.when(pl.program_id(1) == pl.num_programs(1) - 1)
    def _():
        for j in pairs:
            c_ref[0, j] = ct[j].T
            n_ref[0, j:j + 1, :] = n2[j][0:1, :]
        m_ref[0] = m_all


def _ml_step_kernel(q_ref, k_ref, v_ref, og_ref, gt_ref, gtt_ref, gn_ref, c0_ref, n0_ref, m0_ref,
                    a_ref, c_ref, n_ref, m_ref):
    g, t = q_ref.shape[0], q_ref.shape[1]
    gates = gt_ref[...]
    gates_t = gtt_ref[...]
    ri = lax.broadcasted_iota(jnp.int32, (g, t, t), 1)
    ci = lax.broadcasted_iota(jnp.int32, (g, t, t), 2)
    causal = ci <= ri
    eye = jnp.broadcast_to(
        (lax.broadcasted_iota(jnp.int32, (ML_DQK, ML_DQK), 0)
         == lax.broadcasted_iota(jnp.int32, (ML_DQK, ML_DQK), 1)).astype(BF16)[None], (g, ML_DQK, ML_DQK))
    head_lane = lax.broadcasted_iota(jnp.int32, (g, 1, ML_HEADS), 2)
    m_all = m0_ref[...]
    m_out = jnp.zeros((g, 1, ML_HEADS), F32)

    heads = range(ML_HEADS)
    dv = lambda h: slice(h * ML_DV, (h + 1) * ML_DV)
    bmm = functools.partial(jnp.einsum, preferred_element_type=F32)
    q = [q_ref[:, :, h * ML_DQK:(h + 1) * ML_DQK] for h in heads]
    k = [k_ref[:, :, h * ML_DQK:(h + 1) * ML_DQK] for h in heads]
    v = [v_ref[:, :, dv(h)] for h in heads]
    c0 = [c0_ref[:, h] for h in heads]
    n0 = [n0_ref[:, h:h + 1, :] for h in heads]
    m_prev = [m_all[:, :, h:h + 1] for h in heads]

    qk = [bmm("gtd,gsd->gts", q[h], k[h]) for h in heads]
    qc = [bmm("gtd,gde->gte", q[h], c0[h].astype(BF16)) for h in heads]
    kt = [bmm("gdk,gsk->gds", eye, k[h]) for h in heads]

    m_t, w_inter, a, a_sum, kwt, kw_sum, decay = [], [], [], [], [], [], []
    for h in heads:
        fh = ML_HEADS + h
        i_c, f_c = gates[:, :, h:h + 1], gates[:, :, fh:fh + 1]
        i_r, f_r = gates_t[:, h:h + 1, :], gates_t[:, fh:fh + 1, :]
        b_c = jnp.sum(jnp.where(causal, f_r, 0.0), axis=2, keepdims=True)
        b_r = jnp.sum(jnp.where(ri <= ci, f_c, 0.0), axis=1, keepdims=True)
        logw = jnp.where(causal, b_c - b_r + i_r, NEG_INF)
        inter = b_c + m_prev[h]
        m_t.append(jnp.maximum(inter, jnp.max(logw, axis=2, keepdims=True)))
        w_inter.append(jnp.exp(inter - m_t[h]))
        a_h = jnp.exp(logw - m_t[h]) * qk[h]
        a_sum.append(jnp.sum(a_h, axis=2, keepdims=True))
        a.append(a_h.astype(BF16))
        b_last = b_c[:, t - 1:t, :]
        m_new = m_t[h][:, t - 1:t, :]
        decay.append(jnp.exp(b_last + m_prev[h] - m_new))
        kw = jnp.exp(b_last - b_c + i_c - m_new) * k[h].astype(F32)
        kw_sum.append(jnp.sum(kw, axis=1, keepdims=True))
        kwt.append((kt[h] * jnp.exp(b_last - b_r + i_r - m_new)).astype(BF16))
        m_out = jnp.where(head_lane == h, m_new, m_out)

    av = [bmm("gts,gse->gte", a[h], v[h]) for h in heads]
    c_add = [bmm("gds,gse->gde", kwt[h], v[h]) for h in heads]

    for h in heads:
        num = w_inter[h] * qc[h] + av[h]
        qn = jnp.sum(q[h].astype(F32) * n0[h], axis=2, keepdims=True)
        den = w_inter[h] * qn + a_sum[h]
        hs = num / jnp.maximum(jnp.abs(den), jnp.exp(-m_t[h]))
        hn = hs * lax.rsqrt(jnp.mean(hs * hs, axis=2, keepdims=True) + EPS) * gn_ref[:, dv(h)]
        a_ref[:, :, dv(h)] = (og_ref[:, :, dv(h)] * hn).astype(BF16)
        c_ref[:, h] = decay[h] * c0[h] + c_add[h]
        n_ref[:, h:h + 1, :] = decay[h] * n0[h] + kw_sum[h]
    m_ref[...] = m_out


def _ml_chunk(qt, k, vt, ogt, gates, gates_t, gnb, chunk):
    nb, d, seq_len = vt.shape
    hq = k.shape[2]
    pairs = ML_HEADS // 2
    span = chunk * ML_STEP_CHUNKS
    tok = lambda width: pl.BlockSpec((1, span, width), lambda b, c: (b, c, 0))
    tok_t = lambda width: pl.BlockSpec((1, width, span), lambda b, c: (b, 0, c))
    return pl.pallas_call(
        functools.partial(_ml_chunk_kernel, t=chunk),
        grid=(nb, seq_len // span),
        in_specs=[tok_t(hq), tok(hq), tok_t(d), tok_t(d), tok(2 * ML_HEADS), tok_t(2 * ML_HEADS),
                  pl.BlockSpec((d, LANES), lambda b, c: (0, 0))],
        out_specs=(tok(d), pl.BlockSpec((1, pairs, LANES, LANES), lambda b, c: (b, 0, 0, 0)),
                   pl.BlockSpec((1, pairs, LANES), lambda b, c: (b, 0, 0)),
                   pl.BlockSpec((1, 1, ML_HEADS), lambda b, c: (b, 0, 0))),
        out_shape=(jax.ShapeDtypeStruct((nb, seq_len, d), BF16),
                   jax.ShapeDtypeStruct((nb, pairs, LANES, LANES), F32),
                   jax.ShapeDtypeStruct((nb, pairs, LANES), F32),
                   jax.ShapeDtypeStruct((nb, 1, ML_HEADS), F32)),
        scratch_shapes=[pltpu.VMEM((pairs, ML_DV, LANES), F32), pltpu.VMEM((pairs, SUBLANES, LANES), F32),
                        pltpu.VMEM((1, ML_HEADS), F32)],
        compiler_params=_params("parallel", "arbitrary"),
        name="ml_chunk",
    )(qt, k, vt, ogt, gates, gates_t, gnb)


def _ml_step(q, k, v, og, gates, gates_t, gn, c0, n0, m0):
    nb, t, d = v.shape
    hq = q.shape[2]
    g = ML_STEP_SEQS
    lead = lambda *rest: pl.BlockSpec((g, *rest), lambda i: (i,) + (0,) * len(rest))
    state = (lead(ML_HEADS, ML_DQK, ML_DV), lead(ML_HEADS, ML_DQK), lead(1, ML_HEADS))
    return pl.pallas_call(
        _ml_step_kernel,
        grid=(nb // g,),
        in_specs=[lead(t, hq), lead(t, hq), lead(t, d), lead(t, d), lead(t, 2 * ML_HEADS),
                  lead(2 * ML_HEADS, t), pl.BlockSpec((1, d), lambda i: (0, 0)), *state],
        out_specs=(lead(t, d), *state),
        out_shape=(jax.ShapeDtypeStruct((nb, t, d), BF16), jax.ShapeDtypeStruct(c0.shape, F32),
                   jax.ShapeDtypeStruct(n0.shape, F32), jax.ShapeDtypeStruct(m0.shape, F32)),
        compiler_params=_params("parallel"),
        name="ml_step",
    )(q, k, v, og, gates, gates_t, gn, c0, n0, m0)


def _pad_cols(w, width):
    return jnp.pad(w, ((0, 0), (0, width - w.shape[1])))


def kernel(x_prompt, x_sample, state_conv, cache_k, cache_v, cache_logf, page_table, state_C, state_n, state_m, norm_mix, norm_ffn, norm_final, w_conv_in, w_conv, w_conv_out, w_fox_in, b_fox_f, w_fox_out, w_ml_in, b_ml_gates, g_ml_norm, w_ml_out, w_ffn_gu, w_ffn_down):
    bp, lp, d = x_prompt.shape
    bs, ls, _ = x_sample.shape
    depth = norm_mix.shape[0]
    dff = w_ffn_down.shape[1]
    assert d == D_MODEL and ls == SUBLANES
    xp = x_prompt.reshape(bp * lp, d)
    xs = x_sample.reshape(bs * ls, d)
    gf = norm_final.reshape(1, d)
    hq = ML_HEADS * ML_DQK
    out = {name: [] for name in ("conv_p", "conv_s", "kp", "vp", "lfp", "ks", "vs", "lfs",
                                 "cp", "np", "mp", "cs", "ns", "ms")}

    for i in range(depth):
        kind, j = i % 3, i // 3
        gm = norm_mix[i].reshape(1, d)
        if kind == 0:
            win = w_conv_in[j].astype(BF16)
            wo = w_conv_out[j].astype(BF16)
            ap, sp = _conv_mixer(xp, gm, win, w_conv[j], jnp.zeros((bp, CONV_W - 1, d), F32), lp)
            as_, ss = _conv_mixer(xs, gm, win, w_conv[j], state_conv[j], ls)
            out["conv_p"].append(sp)
            out["conv_s"].append(ss)
        elif kind == 1:
            wqkv = w_fox_in[j][:, :3 * d].astype(BF16)
            wf = _pad_cols(w_fox_in[j][:, 3 * d:], LANES).astype(BF16)
            bf = _pad_cols(b_fox_f[j].reshape(1, FOX_HEADS), LANES)
            wo = w_fox_out[j].astype(BF16)
            qtb, kt, vt, kb, vtb, lft, c, ct = _fox_proj_prompt(
                xp, gm, wqkv.T, wf, wf[:, :FOX_HEADS].T, bf, b_fox_f[j].reshape(FOX_HEADS, 1), lp)
            ap = _fox_attn_prompt(qtb, kb, vtb, c, ct, lp)
            heads_last = lambda t: jnp.transpose(t.reshape(bp, FOX_HEADS, FOX_HEAD_DIM, lp), (0, 3, 1, 2))
            out["kp"].append(heads_last(kt))
            out["vp"].append(heads_last(vt))
            out["lfp"].append(jnp.swapaxes(lft, 1, 2))
            q, k, v, lf = _fox_proj_sample(xs, gm, wqkv, wf, bf)
            as_ = _fox_attn_sample(
                q.reshape(bs, ls, d), k.reshape(bs, ls, d), v.reshape(bs, ls, d),
                lf.reshape(bs, ls, FOX_HEADS), jnp.transpose(cache_k[j], (0, 2, 3, 1)),
                jnp.transpose(cache_v[j], (0, 2, 3, 1)), jnp.transpose(cache_logf[j], (0, 2, 1)),
                page_table).reshape(bs * ls, d)
            out["ks"].append(k.reshape(bs, ls, FOX_HEADS, FOX_HEAD_DIM))
            out["vs"].append(v.reshape(bs, ls, FOX_HEADS, FOX_HEAD_DIM))
            out["lfs"].append(lf.reshape(bs, ls, FOX_HEADS))
        else:
            w = w_ml_in[j]
            wk = w[:, hq:2 * hq].astype(BF16)
            wqvo = jnp.concatenate([w[:, :hq], w[:, 2 * hq:2 * hq + 2 * d]], axis=1).astype(BF16)
            wg = _pad_cols(w[:, 2 * hq + 2 * d:], LANES).astype(BF16)
            bg = _pad_cols(b_ml_gates[j].reshape(1, 2 * ML_HEADS), LANES)
            gn = g_ml_norm[j].reshape(1, d)
            wo = w_ml_out[j].astype(BF16)

            def tokens(nb, seq_len, *arrays):
                return [t.reshape(nb, seq_len, t.shape[-1]) for t in arrays]

            qt, k, vt, ogt, gates = _ml_proj(xp, gm, wk, wqvo.T, wg, bg, lp, feature_major=True)
            k, gates = tokens(bp, lp, k, gates)
            a, c1, n1, m1 = _ml_chunk(qt, k, vt, ogt, gates, jnp.swapaxes(gates, 1, 2),
                                      jnp.broadcast_to(gn.reshape(d, 1), (d, LANES)), ML_CHUNK_PROMPT)
            ap = a.reshape(bp * lp, d)
            out["cp"].append(c1.reshape(bp, ML_HEADS, ML_DQK, ML_DV))
            out["np"].append(n1.reshape(bp, ML_HEADS, ML_DQK))
            out["mp"].append(m1.reshape(bp, ML_HEADS))

            q, k, v, og, gates = _ml_proj(xs, gm, wk, wqvo, wg, bg, ROW_TILE, feature_major=False)
            q, k, v, og, gates = tokens(bs, ls, q, k, v, og, gates)
            a, c1, n1, m1 = _ml_step(q, k, v, og, gates, jnp.swapaxes(gates, 1, 2), gn,
                                     state_C[j], state_n[j], state_m[j].reshape(bs, 1, ML_HEADS))
            as_ = a.reshape(bs * ls, d)
            out["cs"].append(c1); out["ns"].append(n1); out["ms"].append(m1.reshape(bs, ML_HEADS))

        gffn = norm_ffn[i].reshape(1, d)
        wg_ffn = w_ffn_gu[i][:, :dff].astype(BF16)
        wu_ffn = w_ffn_gu[i][:, dff:].astype(BF16)
        wd_ffn = w_ffn_down[i].astype(BF16)
        last = i == depth - 1
        xp = _post(xp, ap, wo, gffn, wg_ffn, wu_ffn, wd_ffn, gf, final_norm=last)
        xs = _post(xs, as_, wo, gffn, wg_ffn, wu_ffn, wd_ffn, gf, final_norm=last)

    st = lambda name: jnp.stack(out[name])
    return (xp.reshape(bp, lp, d), xs.reshape(bs, ls, d), st("conv_p"), st("conv_s"),
            st("kp"), st("vp"), st("lfp"), st("ks"), st("vs"), st("lfs"),
            st("cp"), st("np"), st("mp"), st("cs"), st("ns"), st("ms"))
```
